```python
import math
import jax, jax.numpy as jnp
from jax import lax
import numpy as np

D_MODEL = 1024
BATCH = 16
SEQ = 4096
DEPTH = 2

GRID_W = 64
CTX_LEN = 256
EPS = 1e-6

MLA_WIDTH = D_MODEL // 2
POOL_WIDTH = D_MODEL // 4
HY_WIDTH = D_MODEL // 4

MLA_HEADS = 8
MLA_V = MLA_WIDTH // MLA_HEADS
MLA_NOPE = 64
MLA_ROPE = 32
MLA_Q_RANK = D_MODEL // 4
MLA_KV_RANK = D_MODEL // 8
MLA_SCALE = (MLA_NOPE + MLA_ROPE) ** -0.5
ROPE_BASE = 10000.0
Q_BLOCK = 128

POOL_WINDOWS = (2, 4, 8, 16)
POOL_GROUP = POOL_WIDTH // len(POOL_WINDOWS)

HY_ORDER = 2
HY_EMB = 33
HY_FFN = 64
HY_DECAY_TARGET = 1e-2
HY_DECAY_SHORT_PCT = 0.3
HY_DECAY_LONG_PCT = 1.5

D_FF = 4 * D_MODEL

COL_KV = 0
COL_KR = COL_KV + MLA_KV_RANK
COL_Q = COL_KR + MLA_ROPE
COL_POOL = COL_Q + MLA_Q_RANK
COL_HY = COL_POOL + POOL_WIDTH
N_IN = COL_HY + (HY_ORDER + 1) * HY_WIDTH

kernel_name = "hybrid_mla_pool_hyena_dit_prefix"


def rmsnorm(x, g):
    x32 = x.astype(jnp.float32)
    y = x32 * lax.rsqrt(jnp.mean(x32 * x32, axis=-1, keepdims=True) + EPS)
    return (y * g.astype(jnp.float32)).astype(x.dtype)


def modulate(x, g, shift, scale):
    return rmsnorm(x, g) * (1 + scale) + shift


def adaln(cond, w, b):
    return jnp.split(jax.nn.silu(cond) @ w + b, 6, axis=-1)


def axial_rope_tables(n_tokens, dtype):
    n_rows = n_tokens // GRID_W
    r, cidx = jnp.meshgrid(jnp.arange(n_rows), jnp.arange(GRID_W), indexing="ij")
    r = r.reshape(-1).astype(jnp.float32)
    cidx = cidx.reshape(-1).astype(jnp.float32)
    n_freq = MLA_ROPE // 4
    inv = ROPE_BASE ** (-jnp.arange(n_freq, dtype=jnp.float32) / n_freq)
    ang = jnp.stack([r[:, None] * inv, cidx[:, None] * inv], axis=1)
    return jnp.cos(ang).astype(dtype), jnp.sin(ang).astype(dtype)


def apply_rope(x, cos, sin):
    xr = x.reshape(x.shape[:-1] + (2, 2, MLA_ROPE // 4))
    x1 = xr[..., 0, :]
    x2 = xr[..., 1, :]
    out = jnp.stack([x1 * cos - x2 * sin, x2 * cos + x1 * sin], axis=-2)
    return out.reshape(x.shape)


def mla_kv(p_kv, g_kv, w_kv_up):
    kv = rmsnorm(p_kv, g_kv) @ w_kv_up
    kv = kv.reshape(kv.shape[:-1] + (MLA_HEADS, MLA_NOPE + MLA_V))
    return kv[..., :MLA_NOPE], kv[..., MLA_NOPE:]


def mla_q(p_q, g_q, w_q_up):
    q = rmsnorm(p_q, g_q) @ w_q_up
    q = q.reshape(q.shape[:-1] + (MLA_HEADS, MLA_NOPE + MLA_ROPE))
    return q[..., :MLA_NOPE], q[..., MLA_NOPE:]


def attend(qn, qr, kn, kr, v):
    s = jnp.einsum("bqhd,bkhd->bhqk", qn, kn) + jnp.einsum("bqhr,bkr->bhqk", qr, kr)
    p = jax.nn.softmax(s.astype(jnp.float32) * MLA_SCALE, axis=-1).astype(v.dtype)
    return jnp.einsum("bhqk,bkhd->bqhd", p, v)


def blocked_attend(qn, qr, kn, kr, v):
    b, n = qn.shape[:2]
    nb = n // Q_BLOCK

    def split(t):
        return t.reshape((b, nb, Q_BLOCK) + t.shape[2:]).swapaxes(0, 1)

    out = lax.map(lambda qs: attend(qs[0], qs[1], kn, kr, v), (split(qn), split(qr)))
    return out.swapaxes(0, 1).reshape(b, n, MLA_HEADS * MLA_V)


def pool_mixer(u, pool_w, pool_scale):
    n = u.shape[1]
    t = jnp.arange(n)
    cs = jnp.pad(jnp.cumsum(u.astype(jnp.float32), axis=1), ((0, 0), (1, 0), (0, 0)))
    outs = []
    for g, w in enumerate(POOL_WINDOWS):
        lo = jnp.clip(t - w // 2, 0, n)
        hi = jnp.clip(t + w // 2, 0, n)
        sl = slice(g * POOL_GROUP, (g + 1) * POOL_GROUP)
        csg = cs[..., sl]
        mean = (csg[:, hi] - csg[:, lo]) / (hi - lo).astype(jnp.float32)[:, None]
        outs.append((mean.astype(u.dtype) - u[..., sl]) @ pool_w[g])
    return jnp.concatenate(outs, axis=-1) * pool_scale


def short_conv(u, w, b):
    n = u.shape[1]
    up = jnp.pad(u, ((0, 0), (1, 1), (0, 0)))
    return up[:, :n] * w[0] + up[:, 1:n + 1] * w[1] + up[:, 2:] * w[2] + b


def hyena_filter_spectra(n, p):
    f32 = jnp.float32
    t = jnp.linspace(0.0, 1.0, n, dtype=f32)[:, None]
    bands = (HY_EMB - 1) // 2
    freqs = jnp.linspace(1e-4, bands - 1, bands, dtype=f32)[None, :]
    wpos = 2.0 * math.pi * jnp.arange(n, dtype=f32)[:, None] / n
    z = jnp.concatenate([t, jnp.cos(freqs * wpos), -jnp.sin(freqs * wpos)], axis=-1)
    h = jnp.sin(p["hy_f_freq1"].astype(f32) * (z @ p["hy_f_w1"].astype(f32) + p["hy_f_b1"].astype(f32)))
    h = jnp.sin(p["hy_f_freq2"].astype(f32) * (h @ p["hy_f_w2"].astype(f32) + p["hy_f_b2"].astype(f32)))
    h = (h @ p["hy_f_w3"].astype(f32)).reshape(n, HY_ORDER, 2, HY_WIDTH)
    deltas = jnp.abs(jnp.linspace(math.log(HY_DECAY_TARGET) / HY_DECAY_LONG_PCT,
                                  math.log(HY_DECAY_TARGET) / HY_DECAY_SHORT_PCT, HY_WIDTH, dtype=f32))
    h = h * jnp.exp(-t[:, :, None] * deltas)[:, :, None, :]
    fwd = h[:, :, 0]
    bwd = h[1:, :, 1][::-1]
    circ = jnp.concatenate([fwd, jnp.zeros((1, HY_ORDER, HY_WIDTH), f32), bwd], axis=0)
    circ = circ / jnp.sum(jnp.abs(circ), axis=0, keepdims=True)
    return jnp.fft.rfft(circ, axis=0)


def fftconv(u, spec, bias):
    n = u.shape[1]
    u32 = u.astype(jnp.float32)
    y = jnp.fft.irfft(jnp.fft.rfft(u32, n=2 * n, axis=1) * spec, n=2 * n, axis=1)[:, :n]
    return (y + u32 * bias.astype(jnp.float32)).astype(u.dtype)


def hyena_mixer(u, p):
    uc = short_conv(u, p["hy_conv_w"], p["hy_conv_b"])
    v, x1, x2 = jnp.split(uc, 3, axis=-1)
    spec = hyena_filter_spectra(u.shape[1], p)
    z = x1 * fftconv(v, spec[:, 0], p["hy_bias"][0])
    return x2 * fftconv(z, spec[:, 1], p["hy_bias"][1])


def merge_groups(proj, attn, p):
    pool = pool_mixer(proj[..., COL_POOL:COL_HY], p["pool_w"], p["pool_scale"])
    hy = hyena_mixer(proj[..., COL_HY:], p)
    g = p["g_out"]
    parts = [rmsnorm(attn, g[:MLA_WIDTH]),
             rmsnorm(pool, g[MLA_WIDTH:MLA_WIDTH + POOL_WIDTH]),
             rmsnorm(hy, g[MLA_WIDTH + POOL_WIDTH:])]
    return jnp.concatenate(parts, axis=-1) @ p["w_out"]


def sq_relu_mlp(h, w1, w2):
    return jnp.square(jax.nn.relu(h @ w1)) @ w2


def hybrid_layer(x, xc, c, c_ctx, p, last):
    sh1, sc1, g1, sh2, sc2, g2 = [m[:, None, :] for m in adaln(c, p["w_mod"], p["b_mod"])]
    csh1, csc1, cg1, csh2, csc2, cg2 = adaln(c_ctx, p["w_mod"], p["b_mod"])
    h = modulate(x, p["g_mix"], sh1, sc1)
    hc = modulate(xc, p["g_mix"], csh1, csc1)
    proj = h @ p["w_in"]
    proj_c = hc @ (p["w_in"][:, :COL_Q] if last else p["w_in"])
    cos, sin = axial_rope_tables(x.shape[1], x.dtype)
    kn_c, v_c = mla_kv(proj_c[..., COL_KV:COL_KR], p["g_kv"], p["w_kv_up"])
    kr_c = proj_c[..., COL_KR:COL_Q]
    kn_l, v_l = mla_kv(proj[..., COL_KV:COL_KR], p["g_kv"], p["w_kv_up"])
    kr_l = apply_rope(proj[..., COL_KR:COL_Q], cos, sin)
    qn, qr = mla_q(proj[..., COL_Q:COL_POOL], p["g_q"], p["w_q_up"])
    qr = apply_rope(qr, cos[:, None], sin[:, None])
    attn = blocked_attend(qn, qr,
                          jnp.concatenate([kn_c, kn_l], axis=1),
                          jnp.concatenate([kr_c, kr_l], axis=1),
                          jnp.concatenate([v_c, v_l], axis=1))
    x = x + g1 * merge_groups(proj, attn, p)
    x = x + g2 * sq_relu_mlp(modulate(x, p["g_mlp"], sh2, sc2), p["w_mlp1"], p["w_mlp2"])
    if last:
        return x, xc
    qn_c, qr_c = mla_q(proj_c[..., COL_Q:COL_POOL], p["g_q"], p["w_q_up"])
    attn_c = attend(qn_c, qr_c, kn_c, kr_c, v_c).reshape(xc.shape[0], xc.shape[1], MLA_WIDTH)
    xc = xc + cg1 * merge_groups(proj_c, attn_c, p)
    xc = xc + cg2 * sq_relu_mlp(modulate(xc, p["g_mlp"], csh2, csc2), p["w_mlp1"], p["w_mlp2"])
    return x, xc


def setup_inputs(seed: int = 0) -> dict:
    key = jax.random.key(seed)
    ks = iter(jax.random.split(key, 40))

    def nrm(shape, scale=1.0):
        return jax.random.normal(next(ks), shape, jnp.float32) * scale

    def gain(shape):
        return 1.0 + nrm(shape, 0.05)

    L = DEPTH
    return {
        "x": nrm((BATCH, SEQ, D_MODEL)),
        "c": nrm((BATCH, D_MODEL)),
        "ctx": nrm((BATCH, CTX_LEN, D_MODEL)),
        "c_ctx": nrm((D_MODEL,)),
        "w_mod": nrm((L, D_MODEL, 6 * D_MODEL), 0.5 * D_MODEL ** -0.5),
        "b_mod": nrm((L, 6 * D_MODEL), 0.02),
        "g_mix": gain((L, D_MODEL)),
        "g_mlp": gain((L, D_MODEL)),
        "w_in": nrm((L, D_MODEL, N_IN), D_MODEL ** -0.5),
        "g_q": gain((L, MLA_Q_RANK)),
        "w_q_up": nrm((L, MLA_Q_RANK, MLA_HEADS * (MLA_NOPE + MLA_ROPE)), MLA_Q_RANK ** -0.5),
        "g_kv": gain((L, MLA_KV_RANK)),
        "w_kv_up": nrm((L, MLA_KV_RANK, MLA_HEADS * (MLA_NOPE + MLA_V)), MLA_KV_RANK ** -0.5),
        "pool_w": nrm((L, len(POOL_WINDOWS), POOL_GROUP, POOL_GROUP), POOL_GROUP ** -0.5),
        "pool_scale": 1.0 + nrm((L, POOL_WIDTH), 0.1),
        "hy_conv_w": nrm((L, 3, (HY_ORDER + 1) * HY_WIDTH), 3 ** -0.5),
        "hy_conv_b": nrm((L, (HY_ORDER + 1) * HY_WIDTH), 0.02),
        "hy_f_w1": nrm((L, HY_EMB, HY_FFN), HY_EMB ** -0.5),
        "hy_f_b1": nrm((L, HY_FFN), 0.02),
        "hy_f_freq1": 1.0 + nrm((L, HY_FFN), 0.05),
        "hy_f_w2": nrm((L, HY_FFN, HY_FFN), HY_FFN ** -0.5),
        "hy_f_b2": nrm((L, HY_FFN), 0.02),
        "hy_f_freq2": 1.0 + nrm((L, HY_FFN), 0.05),
        "hy_f_w3": nrm((L, HY_FFN, HY_ORDER * 2 * HY_WIDTH), HY_FFN ** -0.5),
        "hy_bias": nrm((L, HY_ORDER, HY_WIDTH), 0.5),
        "g_out": gain((L, D_MODEL)),
        "w_out": nrm((L, D_MODEL, D_MODEL), D_MODEL ** -0.5),
        "w_mlp1": nrm((L, D_MODEL, D_FF), D_MODEL ** -0.5),
        "w_mlp2": nrm((L, D_FF, D_MODEL), D_FF ** -0.5),
        "g_final": gain((D_MODEL,)),
    }


def reference(x, c, ctx, c_ctx, w_mod, b_mod, g_mix, g_mlp, w_in, g_q, w_q_up, g_kv, w_kv_up,
              pool_w, pool_scale, hy_conv_w, hy_conv_b, hy_f_w1, hy_f_b1, hy_f_freq1, hy_f_w2,
              hy_f_b2, hy_f_freq2, hy_f_w3, hy_bias, g_out, w_out, w_mlp1, w_mlp2, g_final):
    xc = ctx
    for l in range(DEPTH):
        p = {
            "w_mod": w_mod[l], "b_mod": b_mod[l], "g_mix": g_mix[l], "g_mlp": g_mlp[l],
            "w_in": w_in[l], "g_q": g_q[l], "w_q_up": w_q_up[l], "g_kv": g_kv[l],
            "w_kv_up": w_kv_up[l], "pool_w": pool_w[l], "pool_scale": pool_scale[l],
            "hy_conv_w": hy_conv_w[l], "hy_conv_b": hy_conv_b[l], "hy_f_w1": hy_f_w1[l],
            "hy_f_b1": hy_f_b1[l], "hy_f_freq1": hy_f_freq1[l], "hy_f_w2": hy_f_w2[l],
            "hy_f_b2": hy_f_b2[l], "hy_f_freq2": hy_f_freq2[l], "hy_f_w3": hy_f_w3[l],
            "hy_bias": hy_bias[l], "g_out": g_out[l], "w_out": w_out[l],
            "w_mlp1": w_mlp1[l], "w_mlp2": w_mlp2[l],
        }
        x, xc = hybrid_layer(x, xc, c, c_ctx, p, last=(l == DEPTH - 1))
    return rmsnorm(x, g_final)
```

```python
import functools
import math

import jax
import jax.numpy as jnp
import numpy as np
from jax import lax
from jax.experimental import pallas as pl
from jax.experimental.pallas import tpu as pltpu

F32 = jnp.float32
BF16 = jnp.bfloat16

D_MODEL = 1024
GRID_W = 64
EPS = 1e-6
N_HEADS = 8
HEAD_V = 64
HEAD_NOPE = 64
HEAD_ROPE = 32
HEAD_PAD = 128
Q_RANK = 256
KV_RANK = 128
MLA_WIDTH = N_HEADS * HEAD_V
POOL_WIDTH = 256
HY_WIDTH = 256
MLA_SCALE = (HEAD_NOPE + HEAD_ROPE) ** -0.5
ROPE_BASE = 10000.0
POOL_WINDOWS = (2, 4, 8, 16)
POOL_GROUP = POOL_WIDTH // len(POOL_WINDOWS)
POOL_PAD = 8
HY_EMB = 33
HY_FFN = 64
HY_DECAY_TARGET = 1e-2
HY_DECAY_SHORT_PCT = 0.3
HY_DECAY_LONG_PCT = 1.5
D_FF = 4 * D_MODEL
COL_KV, COL_KR, COL_Q, COL_POOL, COL_HY, N_IN = 0, 128, 160, 416, 672, 1440
P_KV, P_Q, P_POOL, P_HY, P_KRA, P_KRB, N_IN_P = 0, 128, 384, 640, 1408, 1536, 1664

LANE = 128
VMEM_LIMIT = 56 * 1024 * 1024
TOKEN_TILE = 512
Q_TILE = 256
KEY_CHUNK = 512
CONV_BLOCK = 512
FREQ_BLOCK = 128
FREQ_ROWS = 16
MLP_CHUNK = 1024
HIGHEST = lax.Precision.HIGHEST


def _params(*sem):
    return pltpu.CompilerParams(dimension_semantics=sem, vmem_limit_bytes=VMEM_LIMIT)


def _const_spec(shape):
    zeros = (0,) * len(shape)
    return pl.BlockSpec(shape, lambda *_: zeros, pipeline_mode=pl.Buffered(1))


def _rms(x, g):
    return x * lax.rsqrt(jnp.mean(x * x, axis=-1, keepdims=True) + EPS) * g


def _bdot(a, b):
    return jnp.dot(a, b, preferred_element_type=F32)


def _modulation_kernel(c_ref, w_ref, b_ref, o_ref):
    c = c_ref[...]
    s = c / (1.0 + jnp.exp(-c))
    o_ref[0] = jnp.dot(s, w_ref[0], preferred_element_type=F32, precision=HIGHEST) + b_ref[0]


def _modulation(cond, w_mod, b_mod):
    depth = w_mod.shape[0]
    rows = cond.shape[0]
    return pl.pallas_call(
        _modulation_kernel,
        out_shape=jax.ShapeDtypeStruct((depth, rows, 6 * D_MODEL), F32),
        grid=(depth, 6),
        in_specs=[
            pl.BlockSpec((rows, D_MODEL), lambda l, j: (0, 0)),
            pl.BlockSpec((1, D_MODEL, D_MODEL), lambda l, j: (l, 0, j)),
            pl.BlockSpec((1, 1, D_MODEL), lambda l, j: (l, 0, j)),
        ],
        out_specs=pl.BlockSpec((1, rows, D_MODEL), lambda l, j: (l, 0, j)),
        compiler_params=_params("arbitrary", "arbitrary"),
        name="modulation",
    )(cond, w_mod, b_mod.reshape(depth, 1, 6 * D_MODEL))


def _inproj_kernel(x_ref, sh_ref, sc_ref, gmix_ref, win_ref, gkv_ref, wka_ref, wvt_ref,
                   gq_ref, wab_ref, cos_ref, sin_ref,
                   q_ref, k_ref, vt_ref, pool_ref, hy_ref):
    x = x_ref[0]
    h = (_rms(x, gmix_ref[...]) * (1.0 + sc_ref[0]) + sh_ref[0]).astype(BF16)
    proj = _bdot(h, win_ref[...])
    pool_ref[0] = proj[:, P_POOL:P_HY]
    hy_ref[0] = proj[:, P_HY:P_KRA].astype(hy_ref.dtype)
    kvn = _rms(proj[:, P_KV:P_Q], gkv_ref[...]).astype(BF16)
    kpad = _bdot(kvn, wka_ref[...])
    vt_ref[0, 0] = lax.dot_general(wvt_ref[...], kvn, (((1,), (1,)), ((), ())),
                                   preferred_element_type=F32).astype(vt_ref.dtype)
    qn = _rms(proj[:, P_Q:P_POOL], gq_ref[...]).astype(BF16)
    ab = _bdot(qn, wab_ref[...])
    cos = cos_ref[...]
    sin = sin_ref[...]
    kr = proj[:, P_KRA:P_KRB] * cos + proj[:, P_KRB:N_IN_P] * sin
    width = N_HEADS * HEAD_PAD
    for hd in range(N_HEADS):
        sl = slice(hd * HEAD_PAD, (hd + 1) * HEAD_PAD)
        sl_b = slice(width + hd * HEAD_PAD, width + (hd + 1) * HEAD_PAD)
        k_ref[0, :, sl] = (kpad[:, sl] + kr).astype(k_ref.dtype)
        q_ref[0, :, sl] = ((ab[:, sl] * cos + ab[:, sl_b] * sin) * MLA_SCALE).astype(q_ref.dtype)


def _inproj(x, shift, scale, lw, cos_t, sin_t, tile):
    bsz, n, _ = x.shape
    nt = n // tile
    width = N_HEADS * HEAD_PAD
    mod_spec = lambda k: pl.BlockSpec((1, 1, D_MODEL), lambda b, i, k=k: (b, 0, k))
    return pl.pallas_call(
        _inproj_kernel,
        out_shape=(
            jax.ShapeDtypeStruct((bsz, n, width), BF16),
            jax.ShapeDtypeStruct((bsz, n, width), BF16),
            jax.ShapeDtypeStruct((bsz, nt, MLA_WIDTH, tile), BF16),
            jax.ShapeDtypeStruct((bsz, n, POOL_WIDTH), F32),
            jax.ShapeDtypeStruct((bsz, n, 3 * HY_WIDTH), BF16),
        ),
        grid=(bsz, nt),
        in_specs=[
            pl.BlockSpec((1, tile, D_MODEL), lambda b, i: (b, i, 0)),
            mod_spec(shift[1]), mod_spec(scale[1]),
            _const_spec((1, D_MODEL)),
            _const_spec((D_MODEL, N_IN_P)),
            _const_spec((1, KV_RANK)),
            _const_spec((KV_RANK, width)),
            _const_spec((MLA_WIDTH, KV_RANK)),
            _const_spec((1, Q_RANK)),
            _const_spec((Q_RANK, 2 * width)),
            pl.BlockSpec((tile, HEAD_PAD), lambda b, i: (i, 0)),
            pl.BlockSpec((tile, HEAD_PAD), lambda b, i: (i, 0)),
        ],
        out_specs=(
            pl.BlockSpec((1, tile, width), lambda b, i: (b, i, 0)),
            pl.BlockSpec((1, tile, width), lambda b, i: (b, i, 0)),
            pl.BlockSpec((1, 1, MLA_WIDTH, tile), lambda b, i: (b, i, 0, 0)),
            pl.BlockSpec((1, tile, POOL_WIDTH), lambda b, i: (b, i, 0)),
            pl.BlockSpec((1, tile, 3 * HY_WIDTH), lambda b, i: (b, i, 0)),
        ),
        compiler_params=_params("parallel", "parallel"),
        name="inproj",
    )(x, shift[0], scale[0], lw["g_mix"], lw["w_in"], lw["g_kv"], lw["wka"], lw["wvt"],
      lw["g_q"], lw["wab"], cos_t, sin_t)


def _attention_kernel(*refs, seg_chunks):
    q_ref = refs[0]
    seg_refs = refs[1:1 + 2 * len(seg_chunks)]
    o_ref = refs[1 + 2 * len(seg_chunks)]
    ot_ref = refs[2 + 2 * len(seg_chunks)]
    tq = q_ref.shape[1]

    for hd in range(N_HEADS):
        hs = slice(hd * HEAD_PAD, (hd + 1) * HEAD_PAD)
        vs = slice(hd * HEAD_V, (hd + 1) * HEAD_V)
        q_h = q_ref[0, :, hs]

        def step(k_ref, vt_ref, chunk, c, carry):
            m, l, acc = carry
            off = c * chunk if isinstance(c, int) else pl.multiple_of(c * chunk, chunk)
            kc = k_ref[0, pl.ds(off, chunk), hs]
            s = lax.dot_general(kc, q_h, (((1,), (1,)), ((), ())),
                                preferred_element_type=F32)
            m_new = jnp.maximum(m, jnp.max(s, axis=0, keepdims=True))
            alpha = jnp.exp(m - m_new)
            p = jnp.exp(s - m_new)
            l = alpha * l + jnp.sum(p, axis=0, keepdims=True)
            pv = _bdot(vt_ref[0, c, vs, :], p.astype(BF16))
            return m_new, l, alpha * acc + pv

        carry = (jnp.full((1, tq), -jnp.inf, F32), jnp.zeros((1, tq), F32),
                 jnp.zeros((HEAD_V, tq), F32))
        for si, (n_chunks, chunk) in enumerate(seg_chunks):
            k_ref, vt_ref = seg_refs[2 * si], seg_refs[2 * si + 1]
            body = functools.partial(step, k_ref, vt_ref, chunk)
            if n_chunks == 1:
                carry = body(0, carry)
            else:
                carry = lax.fori_loop(0, n_chunks, body, carry)
        _, l, acc = carry
        ot_ref[vs, :] = acc / l
    o_ref[0] = ot_ref[...].T


def _attention(q, segments):
    bsz, n, width = q.shape
    tq = min(Q_TILE, n)
    in_specs = [pl.BlockSpec((1, tq, width), lambda b, i: (b, i, 0))]
    args = [q]
    seg_chunks = []
    for k, vt in segments:
        nk = k.shape[1]
        n_chunks, chunk = vt.shape[1], vt.shape[3]
        assert n_chunks * chunk == nk
        seg_chunks.append((n_chunks, chunk))
        in_specs.append(pl.BlockSpec((1, nk, width), lambda b, i: (b, 0, 0)))
        in_specs.append(pl.BlockSpec((1, n_chunks, MLA_WIDTH, chunk), lambda b, i: (b, 0, 0, 0)))
        args += [k, vt]
    return pl.pallas_call(
        functools.partial(_attention_kernel, seg_chunks=tuple(seg_chunks)),
        out_shape=jax.ShapeDtypeStruct((bsz, n, MLA_WIDTH), F32),
        grid=(bsz, n // tq),
        in_specs=in_specs,
        out_specs=pl.BlockSpec((1, tq, MLA_WIDTH), lambda b, i: (b, i, 0)),
        scratch_shapes=[pltpu.VMEM((MLA_WIDTH, tq), F32)],
        compiler_params=_params("parallel", "arbitrary"),
        name="attention",
    )(*args)


def _shift_rows(a, k):
    n = a.shape[0]
    return pltpu.roll(a, k % n, 0)


def _pool_kernel(u_ref, w_ref, scale_ref, o_ref, ext_ref):
    n = u_ref.shape[1]
    u = u_ref[0]
    zeros = jnp.zeros((POOL_PAD, POOL_WIDTH), F32)
    ext_ref[0:POOL_PAD, :] = zeros
    ext_ref[POOL_PAD + n:, :] = zeros
    ext_ref[POOL_PAD:POOL_PAD + n, :] = u
    ext = ext_ref[...]
    s2 = ext + _shift_rows(ext, 1)
    s4 = _shift_rows(s2, 1) + _shift_rows(s2, -1)
    s8 = _shift_rows(s4, 2) + _shift_rows(s4, -2)
    s16 = _shift_rows(s8, 4) + _shift_rows(s8, -4)
    t = lax.broadcasted_iota(jnp.int32, (n, POOL_WIDTH), 0)
    lane = lax.broadcasted_iota(jnp.int32, (n, POOL_WIDTH), 1)
    mean = None
    for g, (w, s) in enumerate(zip(POOL_WINDOWS, (s2, s4, s8, s16))):
        lo = jnp.clip(t - w // 2, 0, n)
        hi = jnp.clip(t + w // 2, 0, n)
        mg = s[POOL_PAD:POOL_PAD + n, :] / (hi - lo).astype(F32)
        mean = mg if mean is None else jnp.where(lane >= g * POOL_GROUP, mg, mean)
    diff = (mean - u).astype(BF16)
    o_ref[0] = _bdot(diff, w_ref[...]) * scale_ref[...]


def _pool(u, w_bd, scale):
    bsz, n, _ = u.shape
    return pl.pallas_call(
        _pool_kernel,
        out_shape=jax.ShapeDtypeStruct((bsz, n, POOL_WIDTH), F32),
        grid=(bsz,),
        in_specs=[
            pl.BlockSpec((1, n, POOL_WIDTH), lambda b: (b, 0, 0)),
            _const_spec((POOL_WIDTH, POOL_WIDTH)),
            _const_spec((1, POOL_WIDTH)),
        ],
        out_specs=pl.BlockSpec((1, n, POOL_WIDTH), lambda b: (b, 0, 0)),
        scratch_shapes=[pltpu.VMEM((n + 2 * POOL_PAD, POOL_WIDTH), F32)],
        compiler_params=_params("parallel"),
        name="pool_mixer",
    )(u, w_bd, scale)


def _filter_kernel(z_ref, w1_ref, b1_ref, f1_ref, w2_ref, b2_ref, f2_ref, w3_ref, dl_ref,
                   g_ref, asum_ref, *, n):
    i = pl.program_id(0)
    rows_per = z_ref.shape[0]
    z = z_ref[...]
    h = jnp.sin(f1_ref[...] * (jnp.dot(z, w1_ref[...], preferred_element_type=F32,
                                       precision=HIGHEST) + b1_ref[...]))
    h = jnp.sin(f2_ref[...] * (jnp.dot(h, w2_ref[...], preferred_element_type=F32,
                                       precision=HIGHEST) + b2_ref[...]))
    h = jnp.dot(h, w3_ref[...], preferred_element_type=F32, precision=HIGHEST)
    r = i * rows_per + lax.broadcasted_iota(jnp.int32, (rows_per, HY_WIDTH), 0)
    backward = r < n
    pos = jnp.where(backward, n - r, r - n)
    t = pos.astype(F32) * (1.0 / (n - 1))
    decay = jnp.exp(-t * dl_ref[...])
    valid = r > 0

    @pl.when(i == 0)
    def _():
        asum_ref[...] = jnp.zeros_like(asum_ref)

    for o in range(2):
        fwd = h[:, (2 * o) * HY_WIDTH:(2 * o + 1) * HY_WIDTH]
        bwd = h[:, (2 * o + 1) * HY_WIDTH:(2 * o + 2) * HY_WIDTH]
        g = jnp.where(valid, jnp.where(backward, bwd, fwd) * decay, 0.0)
        g_ref[o] = g
        asum_ref[o] += jnp.sum(jnp.abs(g), axis=0, keepdims=True)


def _hyena_filter(n, lw):
    rows = 2 * n
    tile = min(rows, 1024)
    r = np.arange(rows)
    pos = np.where(r < n, n - r, r - n).astype(np.float64)
    t = pos / (n - 1)
    bands = (HY_EMB - 1) // 2
    freqs = np.linspace(1e-4, bands - 1, bands)[None, :]
    wpos = 2.0 * math.pi * pos[:, None] / n
    z = np.zeros((rows, LANE), np.float32)
    z[:, 0] = t
    z[:, 1:1 + bands] = np.cos(freqs * wpos)
    z[:, 1 + bands:HY_EMB] = -np.sin(freqs * wpos)
    deltas = np.abs(np.linspace(math.log(HY_DECAY_TARGET) / HY_DECAY_LONG_PCT,
                                math.log(HY_DECAY_TARGET) / HY_DECAY_SHORT_PCT, HY_WIDTH))
    deltas = jnp.asarray(deltas[None, :], F32)
    return pl.pallas_call(
        functools.partial(_filter_kernel, n=n),
        out_shape=(jax.ShapeDtypeStruct((2, rows, HY_WIDTH), F32),
                   jax.ShapeDtypeStruct((2, 1, HY_WIDTH), F32)),
        grid=(rows // tile,),
        in_specs=[
            pl.BlockSpec((tile, LANE), lambda i: (i, 0)),
            _const_spec((LANE, LANE)), _const_spec((1, LANE)), _const_spec((1, LANE)),
            _const_spec((LANE, LANE)), _const_spec((1, LANE)), _const_spec((1, LANE)),
            _const_spec((LANE, 4 * HY_WIDTH)), _const_spec((1, HY_WIDTH)),
        ],
        out_specs=(pl.BlockSpec((2, tile, HY_WIDTH), lambda i: (0, i, 0)),
                   pl.BlockSpec((2, 1, HY_WIDTH), lambda i: (0, 0, 0))),
        compiler_params=_params("arbitrary"),
        name="hyena_filter",
    )(jnp.asarray(z), lw["hy_w1"], lw["hy_b1"], lw["hy_f1"], lw["hy_w2"], lw["hy_b2"],
      lw["hy_f2"], lw["hy_w3"], deltas)


def _dft_tables(blk):
    f = np.arange(blk, dtype=np.float64)[:, None]
    m = np.arange(blk, dtype=np.float64)[None, :]
    theta = math.pi * (2.0 * f + 1.0) * m / (2.0 * blk)
    return np.cos(theta), np.sin(theta)


def _spectra_kernel(g_ref, asum_ref, fwd_ref, k_ref, prev_ref):
    e = pl.program_id(1)
    blk = g_ref.shape[1]
    s = jnp.dot(fwd_ref[...], g_ref[0], preferred_element_type=F32, precision=HIGHEST)
    s = s / asum_ref[0]

    @pl.when(e > 0)
    def _():
        prev = prev_ref[...]
        f = lax.broadcasted_iota(jnp.int32, (blk, HY_WIDTH), 0)
        sgn = jnp.where(f % 2 == 0, 1.0, -1.0).astype(F32)
        k_ref[0, 0, 0] = s[:blk] - sgn * prev[blk:]
        k_ref[0, 0, 1] = s[blk:] + sgn * prev[:blk]

    prev_ref[...] = s


def _hyena_spectra(g, asum, blk):
    rows = g.shape[1]
    nseg = rows // blk
    cos, sin = _dft_tables(blk)
    fwd = jnp.asarray(np.concatenate([cos, -sin], axis=0), F32)
    return pl.pallas_call(
        _spectra_kernel,
        out_shape=jax.ShapeDtypeStruct((2, nseg - 1, 2, blk, HY_WIDTH), F32),
        grid=(2, nseg),
        in_specs=[
            pl.BlockSpec((1, blk, HY_WIDTH), lambda o, e: (o, e, 0)),
            pl.BlockSpec((1, 1, HY_WIDTH), lambda o, e: (o, 0, 0)),
            _const_spec((2 * blk, blk)),
        ],
        out_specs=pl.BlockSpec((1, 1, 2, blk, HY_WIDTH),
                               lambda o, e: (o, jnp.maximum(e - 1, 0), 0, 0, 0)),
        scratch_shapes=[pltpu.VMEM((2 * blk, HY_WIDTH), F32)],
        compiler_params=_params("arbitrary", "arbitrary"),
        name="hyena_spectra",
    )(g, asum, fwd)


def _short_conv_rows(st_ref, rows, w_ref, b_ref):
    r0, r1 = rows
    prev = st_ref[POOL_PAD + r0 - 1:POOL_PAD + r1 - 1, :]
    cur = st_ref[POOL_PAD + r0:POOL_PAD + r1, :]
    nxt = st_ref[POOL_PAD + r0 + 1:POOL_PAD + r1 + 1, :]
    return prev * w_ref[0:1, :] + cur * w_ref[1:2, :] + nxt * w_ref[2:3, :] + b_ref[...]


def _stage(st_ref, src_ref, n, blk):
    zeros = jnp.zeros((POOL_PAD, HY_WIDTH), F32)
    st_ref[0:POOL_PAD, :] = zeros
    st_ref[POOL_PAD + n:, :] = zeros
    for j in range(n // blk):
        st_ref[POOL_PAD + j * blk:POOL_PAD + (j + 1) * blk, :] = (
            src_ref[0, j * blk:(j + 1) * blk, :].astype(F32))


def _conv_kernel(u_ref, gate_ref, uw_ref, ub_ref, gw_ref, gb_ref, bias_ref, k_ref, fwd_ref,
                 inv_ref, o_ref, st_ref, u32_ref, u16_ref, acc_ref, uf_ref, yf_ref, *,
                 conv_input):
    f = pl.program_id(1)
    n = u_ref.shape[1]
    blk = fwd_ref.shape[2]
    nb = n // blk
    fb = fwd_ref.shape[1] // 2

    @pl.when(f == 0)
    def _():
        if conv_input:
            _stage(st_ref, u_ref, n, blk)
        for j in range(nb):
            rows = (j * blk, (j + 1) * blk)
            if conv_input:
                u = _short_conv_rows(st_ref, rows, uw_ref, ub_ref)
            else:
                u = u_ref[0, rows[0]:rows[1], :].astype(F32)
            u32_ref[rows[0]:rows[1], :] = u
            u16_ref[rows[0]:rows[1], :] = u.astype(BF16)
        acc_ref[...] = jnp.zeros_like(acc_ref)

    fwd = fwd_ref[0]
    for j in range(nb):
        uf_ref[j] = _bdot(fwd, u16_ref[j * blk:(j + 1) * blk, :])

    def rows_step(c, carry):
        r0 = pl.multiple_of(c * FREQ_ROWS, FREQ_ROWS)
        rs = pl.ds(r0, FREQ_ROWS)
        rs_im = pl.ds(r0 + fb, FREQ_ROWS)
        for i in range(nb):
            yr = jnp.zeros((FREQ_ROWS, HY_WIDTH), F32)
            yi = jnp.zeros((FREQ_ROWS, HY_WIDTH), F32)
            for j in range(nb):
                d = i - j + nb - 1
                kr = k_ref[0, d, 0, rs, :]
                ki = k_ref[0, d, 1, rs, :]
                ur = uf_ref[j, rs, :]
                ui = uf_ref[j, rs_im, :]
                yr = yr + (kr * ur - ki * ui)
                yi = yi + (kr * ui + ki * ur)
            yf_ref[i, rs, :] = yr.astype(BF16)
            yf_ref[i, rs_im, :] = yi.astype(BF16)
        return carry

    lax.fori_loop(0, fb // FREQ_ROWS, rows_step, 0)

    inv = inv_ref[0]
    for i in range(nb):
        acc_ref[i * blk:(i + 1) * blk, :] += _bdot(inv, yf_ref[i])

    @pl.when(f == pl.num_programs(1) - 1)
    def _():
        _stage(st_ref, gate_ref, n, blk)
        bias = bias_ref[0]
        for j in range(nb):
            rows = (j * blk, (j + 1) * blk)
            rs = slice(rows[0], rows[1])
            gate = _short_conv_rows(st_ref, rows, gw_ref, gb_ref)
            conv = acc_ref[rs, :] + u32_ref[rs, :] * bias
            o_ref[0, rs, :] = (gate * conv).astype(o_ref.dtype)


def _hyena_conv(u_arr, u_col, gate_arr, gate_col, lw, khat, order, blk, conv_input, out_dtype):
    bsz, n, _ = gate_arr.shape
    nb = n // blk
    fb = min(FREQ_BLOCK, blk)
    nf = blk // fb
    cos, sin = _dft_tables(blk)
    fwd = np.concatenate([cos.reshape(nf, fb, blk), -sin.reshape(nf, fb, blk)], axis=1)
    inv = np.concatenate([cos.T.reshape(blk, nf, fb), -sin.T.reshape(blk, nf, fb)], axis=2)
    inv = np.transpose(inv, (1, 0, 2)) / blk
    cw, cb = lw["hy_conv_w"], lw["hy_conv_b"]
    col = lambda c: (lambda b, f, c=c: (0, c))
    return pl.pallas_call(
        functools.partial(_conv_kernel, conv_input=conv_input),
        out_shape=jax.ShapeDtypeStruct((bsz, n, HY_WIDTH), out_dtype),
        grid=(bsz, nf),
        in_specs=[
            pl.BlockSpec((1, n, HY_WIDTH), lambda b, f, c=u_col: (b, 0, c)),
            pl.BlockSpec((1, n, HY_WIDTH), lambda b, f, c=gate_col: (b, 0, c)),
            pl.BlockSpec((3, HY_WIDTH), col(u_col if conv_input else 0)),
            pl.BlockSpec((1, HY_WIDTH), col(u_col if conv_input else 0)),
            pl.BlockSpec((3, HY_WIDTH), col(gate_col)),
            pl.BlockSpec((1, HY_WIDTH), col(gate_col)),
            pl.BlockSpec((1, 1, HY_WIDTH), lambda b, f, o=order: (o, 0, 0)),
            pl.BlockSpec((1, 2 * nb - 1, 2, fb, HY_WIDTH), lambda b, f, o=order: (o, 0, 0, f, 0)),
            pl.BlockSpec((1, 2 * fb, blk), lambda b, f: (f, 0, 0)),
            pl.BlockSpec((1, blk, 2 * fb), lambda b, f: (f, 0, 0)),
        ],
        out_specs=pl.BlockSpec((1, n, HY_WIDTH), lambda b, f: (b, 0, 0)),
        scratch_shapes=[
            pltpu.VMEM((n + 2 * POOL_PAD, HY_WIDTH), F32),
            pltpu.VMEM((n, HY_WIDTH), F32),
            pltpu.VMEM((n, HY_WIDTH), BF16),
            pltpu.VMEM((n, HY_WIDTH), F32),
            pltpu.VMEM((nb, 2 * fb, HY_WIDTH), F32),
            pltpu.VMEM((nb, 2 * fb, HY_WIDTH), BF16),
        ],
        compiler_params=_params("parallel", "arbitrary"),
        name="hyena_conv%d" % order,
    )(u_arr, gate_arr, cw, cb, cw, cb, lw["hy_bias"], khat,
      jnp.asarray(fwd, BF16), jnp.asarray(inv, BF16))


def _hyena(hy_u, lw):
    n = hy_u.shape[1]
    blk = min(CONV_BLOCK, n)
    g, asum = _hyena_filter(n, lw)
    khat = _hyena_spectra(g, asum, blk)
    z = _hyena_conv(hy_u, 0, hy_u, 1, lw, khat, 0, blk, True, BF16)
    return _hyena_conv(z, 0, hy_u, 2, lw, khat, 1, blk, False, F32)


def _merge_kernel(x_ref, attn_ref, pool_ref, hy_ref, g1_ref, sh2_ref, sc2_ref, g2_ref,
                  gout_ref, wout_ref, gmlp_ref, w1_ref, w2_ref, gfin_ref, o_ref, *, final_norm):
    gout = gout_ref[...]
    a = _rms(attn_ref[0], gout[:, :MLA_WIDTH]).astype(BF16)
    p = _rms(pool_ref[0], gout[:, MLA_WIDTH:MLA_WIDTH + POOL_WIDTH]).astype(BF16)
    hh = _rms(hy_ref[0], gout[:, MLA_WIDTH + POOL_WIDTH:]).astype(BF16)
    y = (_bdot(a, wout_ref[0:MLA_WIDTH, :])
         + _bdot(p, wout_ref[MLA_WIDTH:MLA_WIDTH + POOL_WIDTH, :])
         + _bdot(hh, wout_ref[MLA_WIDTH + POOL_WIDTH:, :]))
    x1 = x_ref[0] + g1_ref[0] * y
    h2 = (_rms(x1, gmlp_ref[...]) * (1.0 + sc2_ref[0]) + sh2_ref[0]).astype(BF16)
    y2 = jnp.zeros_like(x1)
    for c in range(D_FF // MLP_CHUNK):
        cs = slice(c * MLP_CHUNK, (c + 1) * MLP_CHUNK)
        hid = jnp.maximum(_bdot(h2, w1_ref[:, cs]), 0.0)
        y2 = y2 + _bdot((hid * hid).astype(BF16), w2_ref[cs, :])
    x2 = x1 + g2_ref[0] * y2
    if final_norm:
        x2 = _rms(x2, gfin_ref[...])
    o_ref[0] = x2


def _merge(x, attn, pool, hy, mods, lw, g_final, final_norm, tile):
    bsz, n, _ = x.shape
    nt = n // tile
    tok = lambda w: pl.BlockSpec((1, tile, w), lambda b, i: (b, i, 0))
    mod_spec = lambda k: pl.BlockSpec((1, 1, D_MODEL), lambda b, i, k=k: (b, 0, k))
    return pl.pallas_call(
        functools.partial(_merge_kernel, final_norm=final_norm),
        out_shape=jax.ShapeDtypeStruct((bsz, n, D_MODEL), F32),
        grid=(bsz, nt),
        in_specs=[
            tok(D_MODEL), tok(MLA_WIDTH), tok(POOL_WIDTH), tok(HY_WIDTH),
            mod_spec(2), mod_spec(3), mod_spec(4), mod_spec(5),
            _const_spec((1, D_MODEL)), _const_spec((D_MODEL, D_MODEL)),
            _const_spec((1, D_MODEL)), _const_spec((D_MODEL, D_FF)),
            _const_spec((D_FF, D_MODEL)), _const_spec((1, D_MODEL)),
        ],
        out_specs=tok(D_MODEL),
        compiler_params=_params("parallel", "parallel"),
        name="merge_mlp",
    )(x, attn, pool, hy, mods, mods, mods, mods, lw["g_out"], lw["w_out"], lw["g_mlp"],
      lw["w_mlp1"], lw["w_mlp2"], g_final)


_ROPE_SWAP = np.concatenate([np.arange(8, 16), np.arange(0, 8), np.arange(24, 32), np.arange(16, 24)])


def _rope_tables(n):
    idx = np.arange(n)
    r = (idx // GRID_W).astype(np.float32)
    c = (idx % GRID_W).astype(np.float32)
    n_freq = HEAD_ROPE // 4
    inv = (ROPE_BASE ** (-np.arange(n_freq, dtype=np.float32) / n_freq)).astype(np.float32)
    ar, ac = r[:, None] * inv, c[:, None] * inv
    cos = np.zeros((n, HEAD_PAD), np.float32)
    sin = np.zeros((n, HEAD_PAD), np.float32)
    cos[:, :HEAD_NOPE] = 1.0
    cos[:, HEAD_NOPE:HEAD_NOPE + HEAD_ROPE] = np.concatenate(
        [np.cos(ar), np.cos(ar), np.cos(ac), np.cos(ac)], axis=1)
    sin[:, HEAD_NOPE:HEAD_NOPE + HEAD_ROPE] = np.concatenate(
        [-np.sin(ar), np.sin(ar), -np.sin(ac), np.sin(ac)], axis=1)
    return jnp.asarray(cos), jnp.asarray(sin)


def _identity_tables(n):
    cos = np.zeros((n, HEAD_PAD), np.float32)
    cos[:, :HEAD_NOPE + HEAD_ROPE] = 1.0
    return jnp.asarray(cos), jnp.zeros((n, HEAD_PAD), F32)


def _layer_weights(l, w):
    w_in = w["w_in"][l]
    kr = w_in[:, COL_KR:COL_Q]
    slot = lambda m: jnp.zeros((D_MODEL, HEAD_PAD), F32).at[:, HEAD_NOPE:HEAD_NOPE + HEAD_ROPE].set(m)
    w_in_p = jnp.concatenate([w_in[:, COL_KV:COL_KR], w_in[:, COL_Q:COL_POOL],
                              w_in[:, COL_POOL:COL_HY], w_in[:, COL_HY:],
                              slot(kr), slot(kr[:, _ROPE_SWAP])], axis=1).astype(BF16)
    wkv = w["w_kv_up"][l].reshape(KV_RANK, N_HEADS, HEAD_NOPE + HEAD_V)
    wka = wkv.at[:, :, HEAD_NOPE:].set(0.0).reshape(KV_RANK, N_HEADS * HEAD_PAD).astype(BF16)
    wvt = wkv[:, :, HEAD_NOPE:].reshape(KV_RANK, MLA_WIDTH).T.astype(BF16)
    wq = w["w_q_up"][l].reshape(Q_RANK, N_HEADS, HEAD_NOPE + HEAD_ROPE)
    wa = jnp.zeros((Q_RANK, N_HEADS, HEAD_PAD), F32).at[:, :, :HEAD_NOPE + HEAD_ROPE].set(wq)
    wb = jnp.zeros((Q_RANK, N_HEADS, HEAD_PAD), F32).at[:, :, HEAD_NOPE:HEAD_NOPE + HEAD_ROPE].set(
        wq[:, :, HEAD_NOPE:][:, :, _ROPE_SWAP])
    wab = jnp.concatenate([wa.reshape(Q_RANK, -1), wb.reshape(Q_RANK, -1)], axis=1).astype(BF16)
    pool_bd = jnp.zeros((POOL_WIDTH, POOL_WIDTH), F32)
    for g in range(len(POOL_WINDOWS)):
        sl = slice(g * POOL_GROUP, (g + 1) * POOL_GROUP)
        pool_bd = pool_bd.at[sl, sl].set(w["pool_w"][l, g])
    pad_rows = lambda m, rows: jnp.zeros((rows, m.shape[1]), F32).at[:m.shape[0]].set(m)
    pad_cols = lambda m, cols: jnp.zeros((m.shape[0], cols), F32).at[:, :m.shape[1]].set(m)
    row = lambda v: v.reshape(1, -1)
    return {
        "g_mix": row(w["g_mix"][l]), "w_in": w_in_p, "g_kv": row(w["g_kv"][l]), "wka": wka,
        "wvt": wvt, "g_q": row(w["g_q"][l]), "wab": wab,
        "pool_w": pool_bd.astype(BF16), "pool_scale": row(w["pool_scale"][l]),
        "hy_conv_w": w["hy_conv_w"][l], "hy_conv_b": row(w["hy_conv_b"][l]),
        "hy_w1": pad_cols(pad_rows(w["hy_f_w1"][l], LANE), LANE),
        "hy_b1": pad_cols(row(w["hy_f_b1"][l]), LANE),
        "hy_f1": pad_cols(row(w["hy_f_freq1"][l]), LANE),
        "hy_w2": pad_cols(pad_rows(w["hy_f_w2"][l], LANE), LANE),
        "hy_b2": pad_cols(row(w["hy_f_b2"][l]), LANE),
        "hy_f2": pad_cols(row(w["hy_f_freq2"][l]), LANE),
        "hy_w3": pad_rows(w["hy_f_w3"][l], LANE),
        "hy_bias": w["hy_bias"][l].reshape(2, 1, HY_WIDTH),
        "g_out": row(w["g_out"][l]), "w_out": w["w_out"][l].astype(BF16),
        "g_mlp": row(w["g_mlp"][l]), "w_mlp1": w["w_mlp1"][l].astype(BF16),
        "w_mlp2": w["w_mlp2"][l].astype(BF16),
    }


def kernel(x, c, ctx, c_ctx, w_mod, b_mod, g_mix, g_mlp, w_in, g_q, w_q_up, g_kv, w_kv_up,
           pool_w, pool_scale, hy_conv_w, hy_conv_b, hy_f_w1, hy_f_b1, hy_f_freq1, hy_f_w2,
           hy_f_b2, hy_f_freq2, hy_f_w3, hy_bias, g_out, w_out, w_mlp1, w_mlp2, g_final):
    w = dict(g_mix=g_mix, g_mlp=g_mlp, w_in=w_in, g_q=g_q, w_q_up=w_q_up, g_kv=g_kv,
             w_kv_up=w_kv_up, pool_w=pool_w, pool_scale=pool_scale, hy_conv_w=hy_conv_w,
             hy_conv_b=hy_conv_b, hy_f_w1=hy_f_w1, hy_f_b1=hy_f_b1, hy_f_freq1=hy_f_freq1,
             hy_f_w2=hy_f_w2, hy_f_b2=hy_f_b2, hy_f_freq2=hy_f_freq2, hy_f_w3=hy_f_w3,
             hy_bias=hy_bias, g_out=g_out, w_out=w_out, w_mlp1=w_mlp1, w_mlp2=w_mlp2)
    depth = w_mod.shape[0]
    bsz, n, _ = x.shape
    n_ctx = ctx.shape[1]
    tile = min(TOKEN_TILE, n)
    tile_c = min(TOKEN_TILE, n_ctx)

    rows = -(-(bsz + 1) // 8) * 8
    cond = jnp.zeros((rows, D_MODEL), F32).at[:bsz].set(c).at[bsz].set(c_ctx)
    mods_all = _modulation(cond, w_mod, b_mod)
    cos_l, sin_l = _rope_tables(n)
    cos_c, sin_c = _identity_tables(n_ctx)
    g_fin = g_final.reshape(1, D_MODEL)

    xc = ctx
    for l in range(depth):
        last = l == depth - 1
        lw = _layer_weights(l, w)
        mods = mods_all[l, :bsz].reshape(bsz, 1, 6 * D_MODEL)
        mods_c = jnp.broadcast_to(mods_all[l, bsz].reshape(1, 1, 6 * D_MODEL),
                                  (bsz, 1, 6 * D_MODEL))
        q, k, vt, pool_u, hy_u = _inproj(x, (mods, 0), (mods, 1), lw, cos_l, sin_l, tile)
        qc, kc, vtc, pool_uc, hy_uc = _inproj(xc, (mods_c, 0), (mods_c, 1), lw, cos_c, sin_c,
                                              tile_c)
        attn = _attention(q, [(kc, vtc), (k, vt)])
        pool = _pool(pool_u, lw["pool_w"], lw["pool_scale"])
        hy = _hyena(hy_u, lw)
        x = _merge(x, attn, pool, hy, mods, lw, g_fin, last, tile)
        if not last:
            attn_c = _attention(qc, [(kc, vtc)])
            pool_c = _pool(pool_uc, lw["pool_w"], lw["pool_scale"])
            hy_c = _hyena(hy_uc, lw)
            xc = _merge(xc, attn_c, pool_c, hy_c, mods_c, lw, g_fin, False, tile_c)
    return x
```

```python
import functools
import math

import jax
import jax.numpy as jnp
import numpy as np
from jax import lax
from jax.experimental import pallas as pl
from jax.experimental.pallas import tpu as pltpu

F32 = jnp.float32
BF16 = jnp.bfloat16

D_MODEL = 1024
GRID_W = 64
EPS = 1e-6
N_HEADS = 8
HEAD_V = 64
HEAD_NOPE = 64
HEAD_ROPE = 32
HEAD_PAD = 128
Q_RANK = 256
KV_RANK = 128
MLA_WIDTH = N_HEADS * HEAD_V
V_SLOT = 80
VT_ROWS = N_HEADS * V_SLOT
POOL_WIDTH = 256
HY_WIDTH = 256
MLA_SCALE = (HEAD_NOPE + HEAD_ROPE) ** -0.5
Q_SCALE = MLA_SCALE * math.log2(math.e)
ROPE_BASE = 10000.0
POOL_WINDOWS = (2, 4, 8, 16)
POOL_GROUP = POOL_WIDTH // len(POOL_WINDOWS)
POOL_PAD = 8
HY_EMB = 33
HY_FFN = 64
HY_DECAY_TARGET = 1e-2
HY_DECAY_SHORT_PCT = 0.3
HY_DECAY_LONG_PCT = 1.5
D_FF = 4 * D_MODEL
COL_KV, COL_KR, COL_Q, COL_POOL, COL_HY, N_IN = 0, 128, 160, 416, 672, 1440
P_KV, P_Q, P_POOL, P_HY, P_KRA, P_KRB, N_IN_P = 0, 128, 384, 640, 1408, 1536, 1664

LANE = 128
VMEM_LIMIT = 56 * 1024 * 1024
TOKEN_TILE = 512
Q_TILE = 512
CHUNK_UNROLL = 4
CONV_BLOCK = 512
FREQ_BLOCK = 128
FREQ_ROWS = 16
MLP_CHUNK = 1024
HIGHEST = lax.Precision.HIGHEST


def _params(*sem):
    return pltpu.CompilerParams(dimension_semantics=sem, vmem_limit_bytes=VMEM_LIMIT)


def _const_spec(shape):
    zeros = (0,) * len(shape)
    return pl.BlockSpec(shape, lambda *_: zeros, pipeline_mode=pl.Buffered(1))


def _rms(x, g):
    return x * lax.rsqrt(jnp.mean(x * x, axis=-1, keepdims=True) + EPS) * g


def _bdot(a, b):
    return jnp.dot(a, b, preferred_element_type=F32)


def _modulation_kernel(c_ref, w_ref, b_ref, o_ref):
    c = c_ref[...]
    s = c / (1.0 + jnp.exp(-c))
    o_ref[0] = jnp.dot(s, w_ref[0], preferred_element_type=F32, precision=HIGHEST) + b_ref[0]


def _modulation(cond, w_mod, b_mod):
    depth = w_mod.shape[0]
    rows = cond.shape[0]
    return pl.pallas_call(
        _modulation_kernel,
        out_shape=jax.ShapeDtypeStruct((depth, rows, 6 * D_MODEL), F32),
        grid=(depth, 6),
        in_specs=[
            pl.BlockSpec((rows, D_MODEL), lambda l, j: (0, 0)),
            pl.BlockSpec((1, D_MODEL, D_MODEL), lambda l, j: (l, 0, j)),
            pl.BlockSpec((1, 1, D_MODEL), lambda l, j: (l, 0, j)),
        ],
        out_specs=pl.BlockSpec((1, rows, D_MODEL), lambda l, j: (l, 0, j)),
        compiler_params=_params("arbitrary", "arbitrary"),
        name="modulation",
    )(cond, w_mod, b_mod.reshape(depth, 1, 6 * D_MODEL))


def _inproj_kernel(x_ref, sh_ref, sc_ref, gmix_ref, win_ref, gkv_ref, wka_ref, wvt_ref,
                   gq_ref, wab_ref, cos_ref, sin_ref,
                   q_ref, k_ref, vt_ref, pool_ref, hy_ref):
    x = x_ref[0]
    h = (_rms(x, gmix_ref[...]) * (1.0 + sc_ref[0]) + sh_ref[0]).astype(BF16)
    proj = _bdot(h, win_ref[...])
    pool_ref[0] = proj[:, P_POOL:P_HY]
    hy_ref[0] = proj[:, P_HY:P_KRA].astype(hy_ref.dtype)
    kvn = _rms(proj[:, P_KV:P_Q], gkv_ref[...]).astype(BF16)
    kpad = _bdot(kvn, wka_ref[...])
    vt = lax.dot_general(wvt_ref[...], kvn, (((1,), (1,)), ((), ())),
                         preferred_element_type=F32)
    row = lax.broadcasted_iota(jnp.int32, vt.shape, 0)
    vt_ref[0, 0] = jnp.where(row % V_SLOT >= HEAD_V, 1.0, vt).astype(vt_ref.dtype)
    qn = _rms(proj[:, P_Q:P_POOL], gq_ref[...]).astype(BF16)
    ab = _bdot(qn, wab_ref[...])
    cos = cos_ref[...]
    sin = sin_ref[...]
    kr = proj[:, P_KRA:P_KRB] * cos + proj[:, P_KRB:N_IN_P] * sin
    width = N_HEADS * HEAD_PAD
    for hd in range(N_HEADS):
        sl = slice(hd * HEAD_PAD, (hd + 1) * HEAD_PAD)
        sl_b = slice(width + hd * HEAD_PAD, width + (hd + 1) * HEAD_PAD)
        k_ref[0, :, sl] = (kpad[:, sl] + kr).astype(k_ref.dtype)
        q_ref[0, :, sl] = ((ab[:, sl] * cos + ab[:, sl_b] * sin) * Q_SCALE).astype(q_ref.dtype)


def _inproj(x, shift, scale, lw, cos_t, sin_t, tile):
    bsz, n, _ = x.shape
    nt = n // tile
    width = N_HEADS * HEAD_PAD
    mod_spec = lambda k: pl.BlockSpec((1, 1, D_MODEL), lambda b, i, k=k: (b, 0, k))
    return pl.pallas_call(
        _inproj_kernel,
        out_shape=(
            jax.ShapeDtypeStruct((bsz, n, width), BF16),
            jax.ShapeDtypeStruct((bsz, n, width), BF16),
            jax.ShapeDtypeStruct((bsz, nt, VT_ROWS, tile), BF16),
            jax.ShapeDtypeStruct((bsz, n, POOL_WIDTH), F32),
            jax.ShapeDtypeStruct((bsz, n, 3 * HY_WIDTH), BF16),
        ),
        grid=(bsz, nt),
        in_specs=[
            pl.BlockSpec((1, tile, D_MODEL), lambda b, i: (b, i, 0)),
            mod_spec(shift[1]), mod_spec(scale[1]),
            _const_spec((1, D_MODEL)),
            _const_spec((D_MODEL, N_IN_P)),
            _const_spec((1, KV_RANK)),
            _const_spec((KV_RANK, width)),
            _const_spec((VT_ROWS, KV_RANK)),
            _const_spec((1, Q_RANK)),
            _const_spec((Q_RANK, 2 * width)),
            pl.BlockSpec((tile, HEAD_PAD), lambda b, i: (i, 0)),
            pl.BlockSpec((tile, HEAD_PAD), lambda b, i: (i, 0)),
        ],
        out_specs=(
            pl.BlockSpec((1, tile, width), lambda b, i: (b, i, 0)),
            pl.BlockSpec((1, tile, width), lambda b, i: (b, i, 0)),
            pl.BlockSpec((1, 1, VT_ROWS, tile), lambda b, i: (b, i, 0, 0)),
            pl.BlockSpec((1, tile, POOL_WIDTH), lambda b, i: (b, i, 0)),
            pl.BlockSpec((1, tile, 3 * HY_WIDTH), lambda b, i: (b, i, 0)),
        ),
        compiler_params=_params("parallel", "parallel"),
        name="inproj",
    )(x, shift[0], scale[0], lw["g_mix"], lw["w_in"], lw["g_kv"], lw["wka"], lw["wvt"],
      lw["g_q"], lw["wab"], cos_t, sin_t)


def _attention_kernel(*refs, seg_chunks):
    q_ref = refs[0]
    n_seg = len(seg_chunks)
    seg_refs = refs[1:1 + 2 * n_seg]
    o_ref = refs[1 + 2 * n_seg]
    out_ref, s_bufs = refs[2 + 2 * n_seg], refs[3 + 2 * n_seg:]
    tq = q_ref.shape[1]
    seg_base = [sum(n * ch for n, ch in seg_chunks[:i]) for i in range(n_seg)]

    def scores(hd, k_ref, off, chunk):
        hs = slice(hd * HEAD_PAD, (hd + 1) * HEAD_PAD)
        return lax.dot_general(k_ref[0, pl.ds(off, chunk), hs], q_ref[0, :, hs],
                               (((1,), (1,)), ((), ())), preferred_element_type=F32)

    def stage(h1, h2, m2):
        buf1 = s_bufs[h1 % 2] if h1 is not None else None
        buf2 = s_bufs[h2 % 2] if h2 is not None else None
        vs2 = slice(h2 * V_SLOT, (h2 + 1) * V_SLOT) if h2 is not None else None

        def body(k_ref, vt_ref, chunk, base, c, carry):
            m1, acc2 = carry
            if isinstance(c, int):
                off, row0 = c * chunk, base + c * chunk
            else:
                off = pl.multiple_of(c * chunk, chunk)
                row0 = pl.multiple_of(base + c * chunk, math.gcd(base, chunk))
            rows = pl.ds(row0, chunk)
            if h1 is not None:
                s = scores(h1, k_ref, off, chunk)
                buf1[rows, :] = s
                m1 = jnp.maximum(m1, jnp.max(s, axis=0, keepdims=True))
            if h2 is not None:
                p = jnp.exp2((buf2[rows, :] - m2).astype(BF16))
                acc2 = acc2 + _bdot(vt_ref[0, c, vs2, :], p)
            return m1, acc2

        carry = (jnp.full((1, tq), -jnp.inf, F32), jnp.zeros((V_SLOT, tq), F32))
        for si, (n_chunks, chunk) in enumerate(seg_chunks):
            fn = functools.partial(body, seg_refs[2 * si], seg_refs[2 * si + 1], chunk,
                                   seg_base[si])
            if n_chunks == 1:
                carry = fn(0, carry)
            else:
                carry = lax.fori_loop(0, n_chunks, fn, carry, unroll=CHUNK_UNROLL)
        m1, acc2 = carry
        if h2 is not None:
            out_ref[h2 * HEAD_V:(h2 + 1) * HEAD_V, :] = (
                acc2[:HEAD_V, :] / acc2[HEAD_V:HEAD_V + 1, :])
        return m1

    m_next = stage(0, None, None)
    for hd in range(N_HEADS):
        m_next = stage(hd + 1 if hd + 1 < N_HEADS else None, hd, m_next)
    o_ref[0] = out_ref[...].T


def _attention(q, segments):
    bsz, n, width = q.shape
    tq = min(Q_TILE, n)
    in_specs = [pl.BlockSpec((1, tq, width), lambda b, i: (b, i, 0))]
    args = [q]
    seg_chunks = []
    for k, vt in segments:
        nk = k.shape[1]
        n_chunks, chunk = vt.shape[1], vt.shape[3]
        assert n_chunks * chunk == nk
        seg_chunks.append((n_chunks, chunk))
        in_specs.append(pl.BlockSpec((1, nk, width), lambda b, i: (b, 0, 0)))
        in_specs.append(pl.BlockSpec((1, n_chunks, VT_ROWS, chunk), lambda b, i: (b, 0, 0, 0)))
        args += [k, vt]
    return pl.pallas_call(
        functools.partial(_attention_kernel, seg_chunks=tuple(seg_chunks)),
        out_shape=jax.ShapeDtypeStruct((bsz, n, MLA_WIDTH), F32),
        grid=(bsz, n // tq),
        in_specs=in_specs,
        out_specs=pl.BlockSpec((1, tq, MLA_WIDTH), lambda b, i: (b, i, 0)),
        scratch_shapes=[pltpu.VMEM((MLA_WIDTH, tq), F32),
                        pltpu.VMEM((sum(kk.shape[1] for kk, _ in segments), tq), F32),
                        pltpu.VMEM((sum(kk.shape[1] for kk, _ in segments), tq), F32)],
        compiler_params=_params("parallel", "arbitrary"),
        name="attention",
    )(*args)


def _shift_rows(a, k):
    n = a.shape[0]
    return pltpu.roll(a, k % n, 0)


def _pool_kernel(u_ref, w_ref, scale_ref, o_ref, ext_ref):
    n = u_ref.shape[1]
    u = u_ref[0]
    zeros = jnp.zeros((POOL_PAD, POOL_WIDTH), F32)
    ext_ref[0:POOL_PAD, :] = zeros
    ext_ref[POOL_PAD + n:, :] = zeros
    ext_ref[POOL_PAD:POOL_PAD + n, :] = u
    ext = ext_ref[...]
    s2 = ext + _shift_rows(ext, 1)
    s4 = _shift_rows(s2, 1) + _shift_rows(s2, -1)
    s8 = _shift_rows(s4, 2) + _shift_rows(s4, -2)
    s16 = _shift_rows(s8, 4) + _shift_rows(s8, -4)
    t = lax.broadcasted_iota(jnp.int32, (n, POOL_WIDTH), 0)
    lane = lax.broadcasted_iota(jnp.int32, (n, POOL_WIDTH), 1)
    mean = None
    for g, (w, s) in enumerate(zip(POOL_WINDOWS, (s2, s4, s8, s16))):
        lo = jnp.clip(t - w // 2, 0, n)
        hi = jnp.clip(t + w // 2, 0, n)
        mg = s[POOL_PAD:POOL_PAD + n, :] / (hi - lo).astype(F32)
        mean = mg if mean is None else jnp.where(lane >= g * POOL_GROUP, mg, mean)
    diff = (mean - u).astype(BF16)
    o_ref[0] = _bdot(diff, w_ref[...]) * scale_ref[...]


def _pool(u, w_bd, scale):
    bsz, n, _ = u.shape
    return pl.pallas_call(
        _pool_kernel,
        out_shape=jax.ShapeDtypeStruct((bsz, n, POOL_WIDTH), F32),
        grid=(bsz,),
        in_specs=[
            pl.BlockSpec((1, n, POOL_WIDTH), lambda b: (b, 0, 0)),
            _const_spec((POOL_WIDTH, POOL_WIDTH)),
            _const_spec((1, POOL_WIDTH)),
        ],
        out_specs=pl.BlockSpec((1, n, POOL_WIDTH), lambda b: (b, 0, 0)),
        scratch_shapes=[pltpu.VMEM((n + 2 * POOL_PAD, POOL_WIDTH), F32)],
        compiler_params=_params("parallel"),
        name="pool_mixer",
    )(u, w_bd, scale)


def _filter_kernel(z_ref, w1_ref, b1_ref, f1_ref, w2_ref, b2_ref, f2_ref, w3_ref, dl_ref,
                   g_ref, asum_ref, *, n):
    i = pl.program_id(0)
    rows_per = z_ref.shape[0]
    z = z_ref[...]
    h = jnp.sin(f1_ref[...] * (jnp.dot(z, w1_ref[...], preferred_element_type=F32,
                                       precision=HIGHEST) + b1_ref[...]))
    h = jnp.sin(f2_ref[...] * (jnp.dot(h, w2_ref[...], preferred_element_type=F32,
                                       precision=HIGHEST) + b2_ref[...]))
    h = jnp.dot(h, w3_ref[...], preferred_element_type=F32, precision=HIGHEST)
    r = i * rows_per + lax.broadcasted_iota(jnp.int32, (rows_per, HY_WIDTH), 0)
    backward = r < n
    pos = jnp.where(backward, n - r, r - n)
    t = pos.astype(F32) * (1.0 / (n - 1))
    decay = jnp.exp(-t * dl_ref[...])
    valid = r > 0

    @pl.when(i == 0)
    def _():
        asum_ref[...] = jnp.zeros_like(asum_ref)

    for o in range(2):
        fwd = h[:, (2 * o) * HY_WIDTH:(2 * o + 1) * HY_WIDTH]
        bwd = h[:, (2 * o + 1) * HY_WIDTH:(2 * o + 2) * HY_WIDTH]
        g = jnp.where(valid, jnp.where(backward, bwd, fwd) * decay, 0.0)
        g_ref[o] = g
        asum_ref[o] += jnp.sum(jnp.abs(g), axis=0, keepdims=True)


def _hyena_filter(n, lw):
    rows = 2 * n
    tile = min(rows, 1024)
    r = np.arange(rows)
    pos = np.where(r < n, n - r, r - n).astype(np.float64)
    t = pos / (n - 1)
    bands = (HY_EMB - 1) // 2
    freqs = np.linspace(1e-4, bands - 1, bands)[None, :]
    wpos = 2.0 * math.pi * pos[:, None] / n
    z = np.zeros((rows, LANE), np.float32)
    z[:, 0] = t
    z[:, 1:1 + bands] = np.cos(freqs * wpos)
    z[:, 1 + bands:HY_EMB] = -np.sin(freqs * wpos)
    deltas = np.abs(np.linspace(math.log(HY_DECAY_TARGET) / HY_DECAY_LONG_PCT,
                                math.log(HY_DECAY_TARGET) / HY_DECAY_SHORT_PCT, HY_WIDTH))
    deltas = jnp.asarray(deltas[None, :], F32)
    return pl.pallas_call(
        functools.partial(_filter_kernel, n=n),
        out_shape=(jax.ShapeDtypeStruct((2, rows, HY_WIDTH), F32),
                   jax.ShapeDtypeStruct((2, 1, HY_WIDTH), F32)),
        grid=(rows // tile,),
        in_specs=[
            pl.BlockSpec((tile, LANE), lambda i: (i, 0)),
            _const_spec((LANE, LANE)), _const_spec((1, LANE)), _const_spec((1, LANE)),
            _const_spec((LANE, LANE)), _const_spec((1, LANE)), _const_spec((1, LANE)),
            _const_spec((LANE, 4 * HY_WIDTH)), _const_spec((1, HY_WIDTH)),
        ],
        out_specs=(pl.BlockSpec((2, tile, HY_WIDTH), lambda i: (0, i, 0)),
                   pl.BlockSpec((2, 1, HY_WIDTH), lambda i: (0, 0, 0))),
        compiler_params=_params("arbitrary"),
        name="hyena_filter",
    )(jnp.asarray(z), lw["hy_w1"], lw["hy_b1"], lw["hy_f1"], lw["hy_w2"], lw["hy_b2"],
      lw["hy_f2"], lw["hy_w3"], deltas)


def _dft_tables(blk):
    f = np.arange(blk, dtype=np.float64)[:, None]
    m = np.arange(blk, dtype=np.float64)[None, :]
    theta = math.pi * (2.0 * f + 1.0) * m / (2.0 * blk)
    return np.cos(theta), np.sin(theta)


def _spectra_kernel(g_ref, asum_ref, fwd_ref, k_ref, prev_ref):
    e = pl.program_id(1)
    blk = g_ref.shape[1]
    s = jnp.dot(fwd_ref[...], g_ref[0], preferred_element_type=F32, precision=HIGHEST)
    s = s / asum_ref[0]

    @pl.when(e > 0)
    def _():
        prev = prev_ref[...]
        f = lax.broadcasted_iota(jnp.int32, (blk, HY_WIDTH), 0)
        sgn = jnp.where(f % 2 == 0, 1.0, -1.0).astype(F32)
        k_ref[0, 0, 0] = s[:blk] - sgn * prev[blk:]
        k_ref[0, 0, 1] = s[blk:] + sgn * prev[:blk]

    prev_ref[...] = s


def _hyena_spectra(g, asum, blk):
    rows = g.shape[1]
    nseg = rows // blk
    cos, sin = _dft_tables(blk)
    fwd = jnp.asarray(np.concatenate([cos, -sin], axis=0), F32)
    return pl.pallas_call(
        _spectra_kernel,
        out_shape=jax.ShapeDtypeStruct((2, nseg - 1, 2, blk, HY_WIDTH), F32),
        grid=(2, nseg),
        in_specs=[
            pl.BlockSpec((1, blk, HY_WIDTH), lambda o, e: (o, e, 0)),
            pl.BlockSpec((1, 1, HY_WIDTH), lambda o, e: (o, 0, 0)),
            _const_spec((2 * blk, blk)),
        ],
        out_specs=pl.BlockSpec((1, 1, 2, blk, HY_WIDTH),
                               lambda o, e: (o, jnp.maximum(e - 1, 0), 0, 0, 0)),
        scratch_shapes=[pltpu.VMEM((2 * blk, HY_WIDTH), F32)],
        compiler_params=_params("arbitrary", "arbitrary"),
        name="hyena_spectra",
    )(g, asum, fwd)


def _short_conv_rows(st_ref, rows, w_ref, b_ref):
    r0, r1 = rows
    prev = st_ref[POOL_PAD + r0 - 1:POOL_PAD + r1 - 1, :]
    cur = st_ref[POOL_PAD + r0:POOL_PAD + r1, :]
    nxt = st_ref[POOL_PAD + r0 + 1:POOL_PAD + r1 + 1, :]
    return prev * w_ref[0:1, :] + cur * w_ref[1:2, :] + nxt * w_ref[2:3, :] + b_ref[...]


def _stage(st_ref, src_ref, n, blk):
    zeros = jnp.zeros((POOL_PAD, HY_WIDTH), F32)
    st_ref[0:POOL_PAD, :] = zeros
    st_ref[POOL_PAD + n:, :] = zeros
    for j in range(n // blk):
        st_ref[POOL_PAD + j * blk:POOL_PAD + (j + 1) * blk, :] = (
            src_ref[0, j * blk:(j + 1) * blk, :].astype(F32))


def _conv_kernel(u_ref, gate_ref, uw_ref, ub_ref, gw_ref, gb_ref, bias_ref, k_ref, fwd_ref,
                 inv_ref, o_ref, st_ref, u32_ref, u16_ref, acc_ref, uf_ref, yf_ref, *,
                 conv_input):
    f = pl.program_id(1)
    n = u_ref.shape[1]
    blk = fwd_ref.shape[2]
    nb = n // blk
    fb = fwd_ref.shape[1] // 2

    @pl.when(f == 0)
    def _():
        if conv_input:
            _stage(st_ref, u_ref, n, blk)
        for j in range(nb):
            rows = (j * blk, (j + 1) * blk)
            if conv_input:
                u = _short_conv_rows(st_ref, rows, uw_ref, ub_ref)
            else:
                u = u_ref[0, rows[0]:rows[1], :].astype(F32)
            u32_ref[rows[0]:rows[1], :] = u
            u16_ref[rows[0]:rows[1], :] = u.astype(BF16)
        acc_ref[...] = jnp.zeros_like(acc_ref)

    fwd = fwd_ref[0]
    for j in range(nb):
        uf_ref[j] = _bdot(fwd, u16_ref[j * blk:(j + 1) * blk, :])

    def rows_step(c, carry):
        r0 = pl.multiple_of(c * FREQ_ROWS, FREQ_ROWS)
        rs = pl.ds(r0, FREQ_ROWS)
        rs_im = pl.ds(r0 + fb, FREQ_ROWS)
        for i in range(nb):
            yr = jnp.zeros((FREQ_ROWS, HY_WIDTH), F32)
            yi = jnp.zeros((FREQ_ROWS, HY_WIDTH), F32)
            for j in range(nb):
                d = i - j + nb - 1
                kr = k_ref[0, d, 0, rs, :]
                ki = k_ref[0, d, 1, rs, :]
                ur = uf_ref[j, rs, :]
                ui = uf_ref[j, rs_im, :]
                yr = yr + (kr * ur - ki * ui)
                yi = yi + (kr * ui + ki * ur)
            yf_ref[i, rs, :] = yr.astype(BF16)
            yf_ref[i, rs_im, :] = yi.astype(BF16)
        return carry

    lax.fori_loop(0, fb // FREQ_ROWS, rows_step, 0)

    inv = inv_ref[0]
    for i in range(nb):
        acc_ref[i * blk:(i + 1) * blk, :] += _bdot(inv, yf_ref[i])

    @pl.when(f == pl.num_programs(1) - 1)
    def _():
        _stage(st_ref, gate_ref, n, blk)
        bias = bias_ref[0]
        for j in range(nb):
            rows = (j * blk, (j + 1) * blk)
            rs = slice(rows[0], rows[1])
            gate = _short_conv_rows(st_ref, rows, gw_ref, gb_ref)
            conv = acc_ref[rs, :] + u32_ref[rs, :] * bias
            o_ref[0, rs, :] = (gate * conv).astype(o_ref.dtype)


def _hyena_conv(u_arr, u_col, gate_arr, gate_col, lw, khat, order, blk, conv_input, out_dtype):
    bsz, n, _ = gate_arr.shape
    nb = n // blk
    fb = min(FREQ_BLOCK, blk)
    nf = blk // fb
    cos, sin = _dft_tables(blk)
    fwd = np.concatenate([cos.reshape(nf, fb, blk), -sin.reshape(nf, fb, blk)], axis=1)
    inv = np.concatenate([cos.T.reshape(blk, nf, fb), -sin.T.reshape(blk, nf, fb)], axis=2)
    inv = np.transpose(inv, (1, 0, 2)) / blk
    cw, cb = lw["hy_conv_w"], lw["hy_conv_b"]
    col = lambda c: (lambda b, f, c=c: (0, c))
    return pl.pallas_call(
        functools.partial(_conv_kernel, conv_input=conv_input),
        out_shape=jax.ShapeDtypeStruct((bsz, n, HY_WIDTH), out_dtype),
        grid=(bsz, nf),
        in_specs=[
            pl.BlockSpec((1, n, HY_WIDTH), lambda b, f, c=u_col: (b, 0, c)),
            pl.BlockSpec((1, n, HY_WIDTH), lambda b, f, c=gate_col: (b, 0, c)),
            pl.BlockSpec((3, HY_WIDTH), col(u_col if conv_input else 0)),
            pl.BlockSpec((1, HY_WIDTH), col(u_col if conv_input else 0)),
            pl.BlockSpec((3, HY_WIDTH), col(gate_col)),
            pl.BlockSpec((1, HY_WIDTH), col(gate_col)),
            pl.BlockSpec((1, 1, HY_WIDTH), lambda b, f, o=order: (o, 0, 0)),
            pl.BlockSpec((1, 2 * nb - 1, 2, fb, HY_WIDTH), lambda b, f, o=order: (o, 0, 0, f, 0)),
            pl.BlockSpec((1, 2 * fb, blk), lambda b, f: (f, 0, 0)),
            pl.BlockSpec((1, blk, 2 * fb), lambda b, f: (f, 0, 0)),
        ],
        out_specs=pl.BlockSpec((1, n, HY_WIDTH), lambda b, f: (b, 0, 0)),
        scratch_shapes=[
            pltpu.VMEM((n + 2 * POOL_PAD, HY_WIDTH), F32),
            pltpu.VMEM((n, HY_WIDTH), F32),
            pltpu.VMEM((n, HY_WIDTH), BF16),
            pltpu.VMEM((n, HY_WIDTH), F32),
            pltpu.VMEM((nb, 2 * fb, HY_WIDTH), F32),
            pltpu.VMEM((nb, 2 * fb, HY_WIDTH), BF16),
        ],
        compiler_params=_params("parallel", "arbitrary"),
        name="hyena_conv%d" % order,
    )(u_arr, gate_arr, cw, cb, cw, cb, lw["hy_bias"], khat,
      jnp.asarray(fwd, BF16), jnp.asarray(inv, BF16))


def _hyena(hy_u, lw):
    n = hy_u.shape[1]
    blk = min(CONV_BLOCK, n)
    g, asum = _hyena_filter(n, lw)
    khat = _hyena_spectra(g, asum, blk)
    z = _hyena_conv(hy_u, 0, hy_u, 1, lw, khat, 0, blk, True, BF16)
    return _hyena_conv(z, 0, hy_u, 2, lw, khat, 1, blk, False, F32)


def _merge_kernel(x_ref, attn_ref, pool_ref, hy_ref, g1_ref, sh2_ref, sc2_ref, g2_ref,
                  gout_ref, wout_ref, gmlp_ref, w1_ref, w2_ref, gfin_ref, o_ref, *, final_norm):
    gout = gout_ref[...]
    a = _rms(attn_ref[0], gout[:, :MLA_WIDTH]).astype(BF16)
    p = _rms(pool_ref[0], gout[:, MLA_WIDTH:MLA_WIDTH + POOL_WIDTH]).astype(BF16)
    hh = _rms(hy_ref[0], gout[:, MLA_WIDTH + POOL_WIDTH:]).astype(BF16)
    y = (_bdot(a, wout_ref[0:MLA_WIDTH, :])
         + _bdot(p, wout_ref[MLA_WIDTH:MLA_WIDTH + POOL_WIDTH, :])
         + _bdot(hh, wout_ref[MLA_WIDTH + POOL_WIDTH:, :]))
    x1 = x_ref[0] + g1_ref[0] * y
    h2 = (_rms(x1, gmlp_ref[...]) * (1.0 + sc2_ref[0]) + sh2_ref[0]).astype(BF16)
    y2 = jnp.zeros_like(x1)
    for c in range(D_FF // MLP_CHUNK):
        cs = slice(c * MLP_CHUNK, (c + 1) * MLP_CHUNK)
        hid = jnp.maximum(_bdot(h2, w1_ref[:, cs]), 0.0)
        y2 = y2 + _bdot((hid * hid).astype(BF16), w2_ref[cs, :])
    x2 = x1 + g2_ref[0] * y2
    if final_norm:
        x2 = _rms(x2, gfin_ref[...])
    o_ref[0] = x2


def _merge(x, attn, pool, hy, mods, lw, g_final, final_norm, tile):
    bsz, n, _ = x.shape
    nt = n // tile
    tok = lambda w: pl.BlockSpec((1, tile, w), lambda b, i: (b, i, 0))
    mod_spec = lambda k: pl.BlockSpec((1, 1, D_MODEL), lambda b, i, k=k: (b, 0, k))
    return pl.pallas_call(
        functools.partial(_merge_kernel, final_norm=final_norm),
        out_shape=jax.ShapeDtypeStruct((bsz, n, D_MODEL), F32),
        grid=(bsz, nt),
        in_specs=[
            tok(D_MODEL), tok(MLA_WIDTH), tok(POOL_WIDTH), tok(HY_WIDTH),
            mod_spec(2), mod_spec(3), mod_spec(4), mod_spec(5),
            _const_spec((1, D_MODEL)), _const_spec((D_MODEL, D_MODEL)),
            _const_spec((1, D_MODEL)), _const_spec((D_MODEL, D_FF)),
            _const_spec((D_FF, D_MODEL)), _const_spec((1, D_MODEL)),
        ],
        out_specs=tok(D_MODEL),
        compiler_params=_params("parallel", "parallel"),
        name="merge_mlp",
    )(x, attn, pool, hy, mods, mods, mods, mods, lw["g_out"], lw["w_out"], lw["g_mlp"],
      lw["w_mlp1"], lw["w_mlp2"], g_final)


_ROPE_SWAP = np.concatenate([np.arange(8, 16), np.arange(0, 8), np.arange(24, 32), np.arange(16, 24)])


def _rope_tables(n):
    idx = np.arange(n)
    r = (idx // GRID_W).astype(np.float32)
    c = (idx % GRID_W).astype(np.float32)
    n_freq = HEAD_ROPE // 4
    inv = (ROPE_BASE ** (-np.arange(n_freq, dtype=np.float32) / n_freq)).astype(np.float32)
    ar, ac = r[:, None] * inv, c[:, None] * inv
    cos = np.zeros((n, HEAD_PAD), np.float32)
    sin = np.zeros((n, HEAD_PAD), np.float32)
    cos[:, :HEAD_NOPE] = 1.0
    cos[:, HEAD_NOPE:HEAD_NOPE + HEAD_ROPE] = np.concatenate(
        [np.cos(ar), np.cos(ar), np.cos(ac), np.cos(ac)], axis=1)
    sin[:, HEAD_NOPE:HEAD_NOPE + HEAD_ROPE] = np.concatenate(
        [-np.sin(ar), np.sin(ar), -np.sin(ac), np.sin(ac)], axis=1)
    return jnp.asarray(cos), jnp.asarray(sin)


def _identity_tables(n):
    cos = np.zeros((n, HEAD_PAD), np.float32)
    cos[:, :HEAD_NOPE + HEAD_ROPE] = 1.0
    return jnp.asarray(cos), jnp.zeros((n, HEAD_PAD), F32)


def _layer_weights(l, w):
    w_in = w["w_in"][l]
    kr = w_in[:, COL_KR:COL_Q]
    slot = lambda m: jnp.zeros((D_MODEL, HEAD_PAD), F32).at[:, HEAD_NOPE:HEAD_NOPE + HEAD_ROPE].set(m)
    w_in_p = jnp.concatenate([w_in[:, COL_KV:COL_KR], w_in[:, COL_Q:COL_POOL],
                              w_in[:, COL_POOL:COL_HY], w_in[:, COL_HY:],
                              slot(kr), slot(kr[:, _ROPE_SWAP])], axis=1).astype(BF16)
    wkv = w["w_kv_up"][l].reshape(KV_RANK, N_HEADS, HEAD_NOPE + HEAD_V)
    wka = wkv.at[:, :, HEAD_NOPE:].set(0.0).reshape(KV_RANK, N_HEADS * HEAD_PAD).astype(BF16)
    wvt = jnp.zeros((KV_RANK, N_HEADS, V_SLOT), F32).at[:, :, :HEAD_V].set(wkv[:, :, HEAD_NOPE:])
    wvt = wvt.reshape(KV_RANK, VT_ROWS).T.astype(BF16)
    wq = w["w_q_up"][l].reshape(Q_RANK, N_HEADS, HEAD_NOPE + HEAD_ROPE)
    wa = jnp.zeros((Q_RANK, N_HEADS, HEAD_PAD), F32).at[:, :, :HEAD_NOPE + HEAD_ROPE].set(wq)
    wb = jnp.zeros((Q_RANK, N_HEADS, HEAD_PAD), F32).at[:, :, HEAD_NOPE:HEAD_NOPE + HEAD_ROPE].set(
        wq[:, :, HEAD_NOPE:][:, :, _ROPE_SWAP])
    wab = jnp.concatenate([wa.reshape(Q_RANK, -1), wb.reshape(Q_RANK, -1)], axis=1).astype(BF16)
    pool_bd = jnp.zeros((POOL_WIDTH, POOL_WIDTH), F32)
    for g in range(len(POOL_WINDOWS)):
        sl = slice(g * POOL_GROUP, (g + 1) * POOL_GROUP)
        pool_bd = pool_bd.at[sl, sl].set(w["pool_w"][l, g])
    pad_rows = lambda m, rows: jnp.zeros((rows, m.shape[1]), F32).at[:m.shape[0]].set(m)
    pad_cols = lambda m, cols: jnp.zeros((m.shape[0], cols), F32).at[:, :m.shape[1]].set(m)
    row = lambda v: v.reshape(1, -1)
    return {
        "g_mix": row(w["g_mix"][l]), "w_in": w_in_p, "g_kv": row(w["g_kv"][l]), "wka": wka,
        "wvt": wvt, "g_q": row(w["g_q"][l]), "wab": wab,
        "pool_w": pool_bd.astype(BF16), "pool_scale": row(w["pool_scale"][l]),
        "hy_conv_w": w["hy_conv_w"][l], "hy_conv_b": row(w["hy_conv_b"][l]),
        "hy_w1": pad_cols(pad_rows(w["hy_f_w1"][l], LANE), LANE),
        "hy_b1": pad_cols(row(w["hy_f_b1"][l]), LANE),
        "hy_f1": pad_cols(row(w["hy_f_freq1"][l]), LANE),
        "hy_w2": pad_cols(pad_rows(w["hy_f_w2"][l], LANE), LANE),
        "hy_b2": pad_cols(row(w["hy_f_b2"][l]), LANE),
        "hy_f2": pad_cols(row(w["hy_f_freq2"][l]), LANE),
        "hy_w3": pad_rows(w["hy_f_w3"][l], LANE),
        "hy_bias": w["hy_bias"][l].reshape(2, 1, HY_WIDTH),
        "g_out": row(w["g_out"][l]), "w_out": w["w_out"][l].astype(BF16),
        "g_mlp": row(w["g_mlp"][l]), "w_mlp1": w["w_mlp1"][l].astype(BF16),
        "w_mlp2": w["w_mlp2"][l].astype(BF16),
    }


def kernel(x, c, ctx, c_ctx, w_mod, b_mod, g_mix, g_mlp, w_in, g_q, w_q_up, g_kv, w_kv_up,
           pool_w, pool_scale, hy_conv_w, hy_conv_b, hy_f_w1, hy_f_b1, hy_f_freq1, hy_f_w2,
           hy_f_b2, hy_f_freq2, hy_f_w3, hy_bias, g_out, w_out, w_mlp1, w_mlp2, g_final):
    w = dict(g_mix=g_mix, g_mlp=g_mlp, w_in=w_in, g_q=g_q, w_q_up=w_q_up, g_kv=g_kv,
             w_kv_up=w_kv_up, pool_w=pool_w, pool_scale=pool_scale, hy_conv_w=hy_conv_w,
             hy_conv_b=hy_conv_b, hy_f_w1=hy_f_w1, hy_f_b1=hy_f_b1, hy_f_freq1=hy_f_freq1,
             hy_f_w2=hy_f_w2, hy_f_b2=hy_f_b2, hy_f_freq2=hy_f_freq2, hy_f_w3=hy_f_w3,
             hy_bias=hy_bias, g_out=g_out, w_out=w_out, w_mlp1=w_mlp1, w_mlp2=w_mlp2)
    depth = w_mod.shape[0]
    bsz, n, _ = x.shape
    n_ctx = ctx.shape[1]
    tile = min(TOKEN_TILE, n)
    tile_c = min(TOKEN_TILE, n_ctx)

    rows = -(-(bsz + 1) // 8) * 8
    cond = jnp.zeros((rows, D_MODEL), F32).at[:bsz].set(c).at[bsz].set(c_ctx)
    mods_all = _modulation(cond, w_mod, b_mod)
    cos_l, sin_l = _rope_tables(n)
    cos_c, sin_c = _identity_tables(n_ctx)
    g_fin = g_final.reshape(1, D_MODEL)

    xc = ctx
    for l in range(depth):
        last = l == depth - 1
        lw = _layer_weights(l, w)
        mods = mods_all[l, :bsz].reshape(bsz, 1, 6 * D_MODEL)
        mods_c = jnp.broadcast_to(mods_all[l, bsz].reshape(1, 1, 6 * D_MODEL),
                                  (bsz, 1, 6 * D_MODEL))
        q, k, vt, pool_u, hy_u = _inproj(x, (mods, 0), (mods, 1), lw, cos_l, sin_l, tile)
        qc, kc, vtc, pool_uc, hy_uc = _inproj(xc, (mods_c, 0), (mods_c, 1), lw, cos_c, sin_c,
                                              tile_c)
        attn = _attention(q, [(kc, vtc), (k, vt)])
        pool = _pool(pool_u, lw["pool_w"], lw["pool_scale"])
        hy = _hyena(hy_u, lw)
        x = _merge(x, attn, pool, hy, mods, lw, g_fin, last, tile)
        if not last:
            attn_c = _attention(qc, [(kc, vtc)])
            pool_c = _pool(pool_uc, lw["pool_w"], lw["pool_scale"])
            hy_c = _hyena(hy_uc, lw)
            xc = _merge(xc, attn_c, pool_c, hy_c, mods_c, lw, g_fin, False, tile_c)
    return x
```

```python
import functools
import math

import jax
import jax.numpy as jnp
import numpy as np
from jax import lax
from jax.experimental import pallas as pl
from jax.experimental.pallas import tpu as pltpu

F32 = jnp.float32
BF16 = jnp.bfloat16

D_MODEL = 1024
GRID_W = 64
EPS = 1e-6
N_HEADS = 8
HEAD_V = 64
HEAD_NOPE = 64
HEAD_ROPE = 32
HEAD_PAD = 128
Q_RANK = 256
KV_RANK = 128
MLA_WIDTH = N_HEADS * HEAD_V
V_SLOT = 80
VT_ROWS = N_HEADS * V_SLOT
POOL_WIDTH = 256
HY_WIDTH = 256
MLA_SCALE = (HEAD_NOPE + HEAD_ROPE) ** -0.5
Q_SCALE = MLA_SCALE * math.log2(math.e)
ROPE_BASE = 10000.0
POOL_WINDOWS = (2, 4, 8, 16)
POOL_GROUP = POOL_WIDTH // len(POOL_WINDOWS)
POOL_PAD = 8
EDGE_ROWS = 16
HY_EMB = 33
HY_FFN = 64
HY_DECAY_TARGET = 1e-2
HY_DECAY_SHORT_PCT = 0.3
HY_DECAY_LONG_PCT = 1.5
D_FF = 4 * D_MODEL
COL_KV, COL_KR, COL_Q, COL_POOL, COL_HY, N_IN = 0, 128, 160, 416, 672, 1440
P_KV, P_Q, P_POOL, P_HY, P_KRA, P_KRB, N_IN_P = 0, 128, 384, 640, 1408, 1536, 1664

LANE = 128
VMEM_LIMIT = 56 * 1024 * 1024
TOKEN_TILE = 512
Q_TILE = 512
CHUNK_UNROLL = 8
CONV_BLOCK = 512
FREQ_BLOCK = 128
FREQ_ROWS = 16
MLP_CHUNK = 1024
HIGHEST = lax.Precision.HIGHEST
HIGH = lax.Precision.HIGH


def _params(*sem):
    return pltpu.CompilerParams(dimension_semantics=sem, vmem_limit_bytes=VMEM_LIMIT)


def _const_spec(shape):
    zeros = (0,) * len(shape)
    return pl.BlockSpec(shape, lambda *_: zeros, pipeline_mode=pl.Buffered(1))


def _rms(x, g):
    return x * lax.rsqrt(jnp.mean(x * x, axis=-1, keepdims=True) + EPS) * g


def _bdot(a, b):
    return jnp.dot(a, b, preferred_element_type=F32)


def _split(x):
    hi = x.astype(BF16)
    return hi, (x - hi.astype(F32)).astype(BF16)


def _dot3(a, b):
    return _bdot(a[0], b[0]) + (_bdot(a[0], b[1]) + _bdot(a[1], b[0]))


def _modulation_kernel(c_ref, w_ref, b_ref, o_ref):
    c = c_ref[...]
    s = c / (1.0 + jnp.exp(-c))
    o_ref[0] = jnp.dot(s, w_ref[0], preferred_element_type=F32, precision=HIGHEST) + b_ref[0]


def _modulation(cond, w_mod, b_mod):
    depth = w_mod.shape[0]
    rows = cond.shape[0]
    return pl.pallas_call(
        _modulation_kernel,
        out_shape=jax.ShapeDtypeStruct((depth, rows, 6 * D_MODEL), F32),
        grid=(depth, 6),
        in_specs=[
            pl.BlockSpec((rows, D_MODEL), lambda l, j: (0, 0)),
            pl.BlockSpec((1, D_MODEL, D_MODEL), lambda l, j: (l, 0, j)),
            pl.BlockSpec((1, 1, D_MODEL), lambda l, j: (l, 0, j)),
        ],
        out_specs=pl.BlockSpec((1, rows, D_MODEL), lambda l, j: (l, 0, j)),
        compiler_params=_params("arbitrary", "arbitrary"),
        name="modulation",
    )(cond, w_mod, b_mod.reshape(depth, 1, 6 * D_MODEL))


def _inproj_kernel(x_ref, sh_ref, sc_ref, gmix_ref, win_ref, gkv_ref, wka_ref, wvt_ref,
                   gq_ref, wab_ref, cos_ref, sin_ref,
                   q_ref, k_ref, vt_ref, pool_ref, hy_ref):
    x = x_ref[0]
    h = (_rms(x, gmix_ref[...]) * (1.0 + sc_ref[0]) + sh_ref[0]).astype(BF16)
    proj = _bdot(h, win_ref[...])
    pool_ref[0] = proj[:, P_POOL:P_HY]
    hy_ref[0] = proj[:, P_HY:P_KRA].astype(hy_ref.dtype)
    kvn = _rms(proj[:, P_KV:P_Q], gkv_ref[...]).astype(BF16)
    kpad = _bdot(kvn, wka_ref[...])
    vt = lax.dot_general(wvt_ref[...], kvn, (((1,), (1,)), ((), ())),
                         preferred_element_type=F32)
    row = lax.broadcasted_iota(jnp.int32, vt.shape, 0)
    vt = jnp.where(row % V_SLOT >= HEAD_V, 1.0, vt).astype(vt_ref.dtype)
    vt_ref[0, 0] = vt.reshape(N_HEADS, V_SLOT, vt.shape[1])
    qn = _rms(proj[:, P_Q:P_POOL], gq_ref[...]).astype(BF16)
    ab = _bdot(qn, wab_ref[...])
    cos = cos_ref[...]
    sin = sin_ref[...]
    kr = proj[:, P_KRA:P_KRB] * cos + proj[:, P_KRB:N_IN_P] * sin
    width = N_HEADS * HEAD_PAD
    for hd in range(N_HEADS):
        sl = slice(hd * HEAD_PAD, (hd + 1) * HEAD_PAD)
        sl_b = slice(width + hd * HEAD_PAD, width + (hd + 1) * HEAD_PAD)
        k_ref[0, hd] = (kpad[:, sl] + kr).astype(k_ref.dtype)
        q_ref[0, hd] = ((ab[:, sl] * cos + ab[:, sl_b] * sin) * Q_SCALE).astype(q_ref.dtype)


def _inproj(x, shift, scale, lw, cos_t, sin_t, tile):
    bsz, n, _ = x.shape
    nt = n // tile
    width = N_HEADS * HEAD_PAD
    mod_spec = lambda k: pl.BlockSpec((1, 1, D_MODEL), lambda b, i, k=k: (b, 0, k))
    return pl.pallas_call(
        _inproj_kernel,
        out_shape=(
            jax.ShapeDtypeStruct((bsz, N_HEADS, n, HEAD_PAD), BF16),
            jax.ShapeDtypeStruct((bsz, N_HEADS, n, HEAD_PAD), BF16),
            jax.ShapeDtypeStruct((bsz, nt, N_HEADS, V_SLOT, tile), BF16),
            jax.ShapeDtypeStruct((bsz, n, POOL_WIDTH), F32),
            jax.ShapeDtypeStruct((bsz, n, 3 * HY_WIDTH), BF16),
        ),
        grid=(bsz, nt),
        in_specs=[
            pl.BlockSpec((1, tile, D_MODEL), lambda b, i: (b, i, 0)),
            mod_spec(shift[1]), mod_spec(scale[1]),
            _const_spec((1, D_MODEL)),
            _const_spec((D_MODEL, N_IN_P)),
            _const_spec((1, KV_RANK)),
            _const_spec((KV_RANK, width)),
            _const_spec((VT_ROWS, KV_RANK)),
            _const_spec((1, Q_RANK)),
            _const_spec((Q_RANK, 2 * width)),
            pl.BlockSpec((tile, HEAD_PAD), lambda b, i: (i, 0)),
            pl.BlockSpec((tile, HEAD_PAD), lambda b, i: (i, 0)),
        ],
        out_specs=(
            pl.BlockSpec((1, N_HEADS, tile, HEAD_PAD), lambda b, i: (b, 0, i, 0)),
            pl.BlockSpec((1, N_HEADS, tile, HEAD_PAD), lambda b, i: (b, 0, i, 0)),
            pl.BlockSpec((1, 1, N_HEADS, V_SLOT, tile), lambda b, i: (b, i, 0, 0, 0)),
            pl.BlockSpec((1, tile, POOL_WIDTH), lambda b, i: (b, i, 0)),
            pl.BlockSpec((1, tile, 3 * HY_WIDTH), lambda b, i: (b, i, 0)),
        ),
        compiler_params=_params("parallel", "parallel"),
        name="inproj",
    )(x, shift[0], scale[0], lw["g_mix"], lw["w_in"], lw["g_kv"], lw["wka"], lw["wvt"],
      lw["g_q"], lw["wab"], cos_t, sin_t)


def _attention_kernel(*refs, seg_chunks):
    q_ref = refs[0]
    n_seg = len(seg_chunks)
    seg_refs = refs[1:1 + 2 * n_seg]
    o_ref = refs[1 + 2 * n_seg]
    out_ref, m_ref = refs[2 + 2 * n_seg], refs[3 + 2 * n_seg]
    s_bufs = refs[4 + 2 * n_seg:]
    tq = q_ref.shape[2]
    step_id = pl.program_id(0) + pl.program_id(1) + N_HEADS

    def stage(h1, h2, parity1):
        buf1, buf2 = s_bufs[parity1], s_bufs[1 - parity1]
        m2 = m_ref[...] if h2 is not None else None
        m1 = jnp.full((1, tq), -jnp.inf, F32)
        acc2 = jnp.zeros((V_SLOT, tq), F32)
        base = 0
        for si, (n_chunks, chunk) in enumerate(seg_chunks):
            k_ref, vt_ref = seg_refs[2 * si], seg_refs[2 * si + 1]
            for c in range(n_chunks):
                rows = slice(base + c * chunk, base + (c + 1) * chunk)
                if h1 is not None:
                    s = lax.dot_general(k_ref[0, h1, c * chunk:(c + 1) * chunk, :], q_ref[0, h1],
                                        (((1,), (1,)), ((), ())), preferred_element_type=F32)
                    buf1[rows, :] = s
                    m1 = jnp.maximum(m1, jnp.max(s, axis=0, keepdims=True))
                if h2 is not None:
                    p = jnp.exp2((buf2[rows, :] - m2).astype(BF16))
                    acc2 = acc2 + _bdot(vt_ref[0, c, h2], p)
            base += n_chunks * chunk
        if h2 is not None:
            out_ref[h2] = acc2[:HEAD_V, :] / acc2[HEAD_V:HEAD_V + 1, :]
        if h1 is not None:
            m_ref[...] = m1

    for hd in range(-1, N_HEADS):
        h1 = hd + 1 if hd + 1 < N_HEADS else None
        h2 = hd if hd >= 0 else None
        pl.when(step_id >= hd)(functools.partial(stage, h1, h2, (hd + 1) % 2))
    o_ref[0] = out_ref[...].reshape(MLA_WIDTH, tq).T


def _attention(q, segments):
    bsz, _, n, _ = q.shape
    tq = min(Q_TILE, n)
    in_specs = [pl.BlockSpec((1, N_HEADS, tq, HEAD_PAD), lambda b, i: (b, 0, i, 0))]
    args = [q]
    seg_chunks = []
    for k, vt in segments:
        nk = k.shape[2]
        n_chunks, chunk = vt.shape[1], vt.shape[4]
        assert n_chunks * chunk == nk
        seg_chunks.append((n_chunks, chunk))
        in_specs.append(pl.BlockSpec((1, N_HEADS, nk, HEAD_PAD), lambda b, i: (b, 0, 0, 0)))
        in_specs.append(pl.BlockSpec((1, n_chunks, N_HEADS, V_SLOT, chunk),
                                     lambda b, i: (b, 0, 0, 0, 0)))
        args += [k, vt]
    n_keys = sum(nc * ch for nc, ch in seg_chunks)
    return pl.pallas_call(
        functools.partial(_attention_kernel, seg_chunks=tuple(seg_chunks)),
        out_shape=jax.ShapeDtypeStruct((bsz, n, MLA_WIDTH), F32),
        grid=(bsz, n // tq),
        in_specs=in_specs,
        out_specs=pl.BlockSpec((1, tq, MLA_WIDTH), lambda b, i: (b, i, 0)),
        scratch_shapes=[pltpu.VMEM((N_HEADS, HEAD_V, tq), F32), pltpu.VMEM((1, tq), F32),
                        pltpu.VMEM((n_keys, tq), F32), pltpu.VMEM((n_keys, tq), F32)],
        compiler_params=_params("parallel", "arbitrary"),
        name="attention",
    )(*args)


def _shift_rows(a, k):
    n = a.shape[0]
    return pltpu.roll(a, k % n, 0)


def _pool_kernel(u_ref, w_ref, scale_ref, o_ref, ext_ref):
    n = u_ref.shape[1]
    u = u_ref[0]
    zeros = jnp.zeros((POOL_PAD, POOL_WIDTH), F32)
    ext_ref[0:POOL_PAD, :] = zeros
    ext_ref[POOL_PAD + n:, :] = zeros
    ext_ref[POOL_PAD:POOL_PAD + n, :] = u
    ext = ext_ref[...]
    s2 = ext + _shift_rows(ext, 1)
    s4 = _shift_rows(s2, 1) + _shift_rows(s2, -1)
    s8 = _shift_rows(s4, 2) + _shift_rows(s4, -2)
    s16 = _shift_rows(s8, 4) + _shift_rows(s8, -4)
    t = lax.broadcasted_iota(jnp.int32, (n, POOL_WIDTH), 0)
    lane = lax.broadcasted_iota(jnp.int32, (n, POOL_WIDTH), 1)
    mean = None
    for g, (w, s) in enumerate(zip(POOL_WINDOWS, (s2, s4, s8, s16))):
        lo = jnp.clip(t - w // 2, 0, n)
        hi = jnp.clip(t + w // 2, 0, n)
        mg = s[POOL_PAD:POOL_PAD + n, :] / (hi - lo).astype(F32)
        mean = mg if mean is None else jnp.where(lane >= g * POOL_GROUP, mg, mean)
    diff = (mean - u).astype(BF16)
    o_ref[0] = _bdot(diff, w_ref[...]) * scale_ref[...]


def _pool(u, w_bd, scale):
    bsz, n, _ = u.shape
    return pl.pallas_call(
        _pool_kernel,
        out_shape=jax.ShapeDtypeStruct((bsz, n, POOL_WIDTH), F32),
        grid=(bsz,),
        in_specs=[
            pl.BlockSpec((1, n, POOL_WIDTH), lambda b: (b, 0, 0)),
            _const_spec((POOL_WIDTH, POOL_WIDTH)),
            _const_spec((1, POOL_WIDTH)),
        ],
        out_specs=pl.BlockSpec((1, n, POOL_WIDTH), lambda b: (b, 0, 0)),
        scratch_shapes=[pltpu.VMEM((n + 2 * POOL_PAD, POOL_WIDTH), F32)],
        compiler_params=_params("parallel"),
        name="pool_mixer",
    )(u, w_bd, scale)


def _filter_kernel(z_ref, w1_ref, b1_ref, f1_ref, w2_ref, b2_ref, f2_ref, w3h_ref, w3l_ref,
                   dl_ref, g_ref, asum_ref, *, n):
    i = pl.program_id(0)
    rows_per = z_ref.shape[0]
    z = z_ref[...]
    h = jnp.sin(f1_ref[...] * (jnp.dot(z, w1_ref[...], preferred_element_type=F32,
                                       precision=HIGHEST) + b1_ref[...]))
    h = jnp.sin(f2_ref[...] * (jnp.dot(h, w2_ref[...], preferred_element_type=F32,
                                       precision=HIGHEST) + b2_ref[...]))
    h = _dot3(_split(h), (w3h_ref[...], w3l_ref[...]))
    r = i * rows_per + lax.broadcasted_iota(jnp.int32, (rows_per, HY_WIDTH), 0)
    backward = r < n
    pos = jnp.where(backward, n - r, r - n)
    t = pos.astype(F32) * (1.0 / (n - 1))
    decay = jnp.exp(-t * dl_ref[...])
    valid = r > 0

    @pl.when(i == 0)
    def _():
        asum_ref[...] = jnp.zeros_like(asum_ref)

    for o in range(2):
        fwd = h[:, (2 * o) * HY_WIDTH:(2 * o + 1) * HY_WIDTH]
        bwd = h[:, (2 * o + 1) * HY_WIDTH:(2 * o + 2) * HY_WIDTH]
        g = jnp.where(valid, jnp.where(backward, bwd, fwd) * decay, 0.0)
        g_ref[o] = g
        asum_ref[o] += jnp.sum(jnp.abs(g), axis=0, keepdims=True)


def _hyena_filter(n, lw):
    rows = 2 * n
    tile = min(rows, 1024)
    r = np.arange(rows)
    pos = np.where(r < n, n - r, r - n).astype(np.float64)
    t = pos / (n - 1)
    bands = (HY_EMB - 1) // 2
    freqs = np.linspace(1e-4, bands - 1, bands)[None, :]
    wpos = 2.0 * math.pi * pos[:, None] / n
    z = np.zeros((rows, LANE), np.float32)
    z[:, 0] = t
    z[:, 1:1 + bands] = np.cos(freqs * wpos)
    z[:, 1 + bands:HY_EMB] = -np.sin(freqs * wpos)
    deltas = np.abs(np.linspace(math.log(HY_DECAY_TARGET) / HY_DECAY_LONG_PCT,
                                math.log(HY_DECAY_TARGET) / HY_DECAY_SHORT_PCT, HY_WIDTH))
    deltas = jnp.asarray(deltas[None, :], F32)
    return pl.pallas_call(
        functools.partial(_filter_kernel, n=n),
        out_shape=(jax.ShapeDtypeStruct((2, rows, HY_WIDTH), F32),
                   jax.ShapeDtypeStruct((2, 1, HY_WIDTH), F32)),
        grid=(rows // tile,),
        in_specs=[
            pl.BlockSpec((tile, LANE), lambda i: (i, 0)),
            _const_spec((LANE, LANE)), _const_spec((1, LANE)), _const_spec((1, LANE)),
            _const_spec((LANE, LANE)), _const_spec((1, LANE)), _const_spec((1, LANE)),
            _const_spec((LANE, 4 * HY_WIDTH)), _const_spec((LANE, 4 * HY_WIDTH)),
            _const_spec((1, HY_WIDTH)),
        ],
        out_specs=(pl.BlockSpec((2, tile, HY_WIDTH), lambda i: (0, i, 0)),
                   pl.BlockSpec((2, 1, HY_WIDTH), lambda i: (0, 0, 0))),
        compiler_params=_params("arbitrary"),
        name="hyena_filter",
    )(jnp.asarray(z), lw["hy_w1"], lw["hy_b1"], lw["hy_f1"], lw["hy_w2"], lw["hy_b2"],
      lw["hy_f2"], *_split(lw["hy_w3"]), deltas)


def _dft_tables(blk):
    f = np.arange(blk, dtype=np.float64)[:, None]
    m = np.arange(blk, dtype=np.float64)[None, :]
    theta = math.pi * (2.0 * f + 1.0) * m / (2.0 * blk)
    return np.cos(theta), np.sin(theta)


def _spectra_kernel(g_ref, asum_ref, fwdh_ref, fwdl_ref, k_ref, prev_ref):
    e = pl.program_id(1)
    blk = g_ref.shape[1]
    s = _dot3((fwdh_ref[...], fwdl_ref[...]), _split(g_ref[0]))
    s = s / asum_ref[0]

    @pl.when(e > 0)
    def _():
        prev = prev_ref[...]
        f = lax.broadcasted_iota(jnp.int32, (blk, HY_WIDTH), 0)
        sgn = jnp.where(f % 2 == 0, 1.0, -1.0).astype(F32)
        k_ref[0, 0, 0] = s[:blk] - sgn * prev[blk:]
        k_ref[0, 0, 1] = s[blk:] + sgn * prev[:blk]

    prev_ref[...] = s


def _hyena_spectra(g, asum, blk):
    rows = g.shape[1]
    nseg = rows // blk
    cos, sin = _dft_tables(blk)
    fwd = jnp.asarray(np.concatenate([cos, -sin], axis=0), F32)
    return pl.pallas_call(
        _spectra_kernel,
        out_shape=jax.ShapeDtypeStruct((2, nseg - 1, 2, blk, HY_WIDTH), F32),
        grid=(2, nseg),
        in_specs=[
            pl.BlockSpec((1, blk, HY_WIDTH), lambda o, e: (o, e, 0)),
            pl.BlockSpec((1, 1, HY_WIDTH), lambda o, e: (o, 0, 0)),
            _const_spec((2 * blk, blk)), _const_spec((2 * blk, blk)),
        ],
        out_specs=pl.BlockSpec((1, 1, 2, blk, HY_WIDTH),
                               lambda o, e: (o, jnp.maximum(e - 1, 0), 0, 0, 0)),
        scratch_shapes=[pltpu.VMEM((2 * blk, HY_WIDTH), F32)],
        compiler_params=_params("arbitrary", "arbitrary"),
        name="hyena_spectra",
    )(g, asum, *_split(fwd))


def _short_conv_rows(src_ref, rows, w_ref, b_ref):
    r0, r1 = rows
    n = src_ref.shape[1]
    cur = src_ref[0, r0:r1, :].astype(F32)
    row = lax.broadcasted_iota(jnp.int32, cur.shape, 0)
    edge = jnp.zeros((1, cur.shape[1]), F32)
    before = src_ref[0, r0 - EDGE_ROWS:r0, :].astype(F32)[EDGE_ROWS - 1:, :] if r0 > 0 else edge
    after = src_ref[0, r1:r1 + EDGE_ROWS, :].astype(F32)[:1, :] if r1 < n else edge
    prev = jnp.where(row == 0, before, _shift_rows(cur, 1))
    nxt = jnp.where(row == r1 - r0 - 1, after, _shift_rows(cur, -1))
    return prev * w_ref[0:1, :] + cur * w_ref[1:2, :] + nxt * w_ref[2:3, :] + b_ref[...]


def _conv_kernel(u_ref, gate_ref, uw_ref, ub_ref, gw_ref, gb_ref, bias_ref, k_ref, fwd_ref,
                 inv_ref, o_ref, u32_ref, u16_ref, acc_ref, uf_ref, yf_ref, *, conv_input):
    f = pl.program_id(1)
    n = u_ref.shape[1]
    blk = fwd_ref.shape[2]
    nb = n // blk
    fb = fwd_ref.shape[1] // 2

    @pl.when(f == 0)
    def _():
        for j in range(nb):
            rows = (j * blk, (j + 1) * blk)
            if conv_input:
                u = _short_conv_rows(u_ref, rows, uw_ref, ub_ref)
            else:
                u = u_ref[0, rows[0]:rows[1], :].astype(F32)
            u32_ref[rows[0]:rows[1], :] = u
            u16_ref[rows[0]:rows[1], :] = u.astype(BF16)
        acc_ref[...] = jnp.zeros_like(acc_ref)

    fwd = fwd_ref[0]
    for j in range(nb):
        uf_ref[j] = _bdot(fwd, u16_ref[j * blk:(j + 1) * blk, :])

    def rows_step(c, carry):
        r0 = pl.multiple_of(c * FREQ_ROWS, FREQ_ROWS)
        rs = pl.ds(r0, FREQ_ROWS)
        rs_im = pl.ds(r0 + fb, FREQ_ROWS)
        for i in range(nb):
            yr = jnp.zeros((FREQ_ROWS, HY_WIDTH), F32)
            yi = jnp.zeros((FREQ_ROWS, HY_WIDTH), F32)
            for j in range(nb):
                d = i - j + nb - 1
                kr = k_ref[0, d, 0, rs, :]
                ki = k_ref[0, d, 1, rs, :]
                ur = uf_ref[j, rs, :]
                ui = uf_ref[j, rs_im, :]
                yr = yr + (kr * ur - ki * ui)
                yi = yi + (kr * ui + ki * ur)
            yf_ref[i, rs, :] = yr.astype(BF16)
            yf_ref[i, rs_im, :] = yi.astype(BF16)
        return carry

    lax.fori_loop(0, fb // FREQ_ROWS, rows_step, 0)

    inv = inv_ref[0]
    for i in range(nb):
        acc_ref[i * blk:(i + 1) * blk, :] += _bdot(inv, yf_ref[i])

    @pl.when(f == pl.num_programs(1) - 1)
    def _():
        bias = bias_ref[0]
        for j in range(nb):
            rows = (j * blk, (j + 1) * blk)
            rs = slice(rows[0], rows[1])
            gate = _short_conv_rows(gate_ref, rows, gw_ref, gb_ref)
            conv = acc_ref[rs, :] + u32_ref[rs, :] * bias
            o_ref[0, rs, :] = (gate * conv).astype(o_ref.dtype)


def _hyena_conv(u_arr, u_col, gate_arr, gate_col, lw, khat, order, blk, conv_input, out_dtype):
    bsz, n, _ = gate_arr.shape
    nb = n // blk
    fb = min(FREQ_BLOCK, blk)
    nf = blk // fb
    cos, sin = _dft_tables(blk)
    fwd = np.concatenate([cos.reshape(nf, fb, blk), -sin.reshape(nf, fb, blk)], axis=1)
    inv = np.concatenate([cos.T.reshape(blk, nf, fb), -sin.T.reshape(blk, nf, fb)], axis=2)
    inv = np.transpose(inv, (1, 0, 2)) / blk
    cw, cb = lw["hy_conv_w"], lw["hy_conv_b"]
    col = lambda c: (lambda b, f, c=c: (0, c))
    return pl.pallas_call(
        functools.partial(_conv_kernel, conv_input=conv_input),
        out_shape=jax.ShapeDtypeStruct((bsz, n, HY_WIDTH), out_dtype),
        grid=(bsz, nf),
        in_specs=[
            pl.BlockSpec((1, n, HY_WIDTH), lambda b, f, c=u_col: (b, 0, c)),
            pl.BlockSpec((1, n, HY_WIDTH), lambda b, f, c=gate_col: (b, 0, c)),
            pl.BlockSpec((3, HY_WIDTH), col(u_col if conv_input else 0)),
            pl.BlockSpec((1, HY_WIDTH), col(u_col if conv_input else 0)),
            pl.BlockSpec((3, HY_WIDTH), col(gate_col)),
            pl.BlockSpec((1, HY_WIDTH), col(gate_col)),
            pl.BlockSpec((1, 1, HY_WIDTH), lambda b, f, o=order: (o, 0, 0)),
            pl.BlockSpec((1, 2 * nb - 1, 2, fb, HY_WIDTH), lambda b, f, o=order: (o, 0, 0, f, 0)),
            pl.BlockSpec((1, 2 * fb, blk), lambda b, f: (f, 0, 0)),
            pl.BlockSpec((1, blk, 2 * fb), lambda b, f: (f, 0, 0)),
        ],
        out_specs=pl.BlockSpec((1, n, HY_WIDTH), lambda b, f: (b, 0, 0)),
        scratch_shapes=[
            pltpu.VMEM((n, HY_WIDTH), F32),
            pltpu.VMEM((n, HY_WIDTH), BF16),
            pltpu.VMEM((n, HY_WIDTH), F32),
            pltpu.VMEM((nb, 2 * fb, HY_WIDTH), F32),
            pltpu.VMEM((nb, 2 * fb, HY_WIDTH), BF16),
        ],
        compiler_params=_params("parallel", "arbitrary"),
        name="hyena_conv%d" % order,
    )(u_arr, gate_arr, cw, cb, cw, cb, lw["hy_bias"], khat,
      jnp.asarray(fwd, BF16), jnp.asarray(inv, BF16))


def _hyena(hy_u, lw):
    n = hy_u.shape[1]
    blk = min(CONV_BLOCK, n)
    g, asum = _hyena_filter(n, lw)
    khat = _hyena_spectra(g, asum, blk)
    z = _hyena_conv(hy_u, 0, hy_u, 1, lw, khat, 0, blk, True, BF16)
    return _hyena_conv(z, 0, hy_u, 2, lw, khat, 1, blk, False, F32)


def _merge_kernel(x_ref, attn_ref, pool_ref, hy_ref, g1_ref, sh2_ref, sc2_ref, g2_ref,
                  gout_ref, wout_ref, gmlp_ref, w1_ref, w2_ref, gfin_ref, o_ref, *, final_norm):
    gout = gout_ref[...]
    a = _rms(attn_ref[0], gout[:, :MLA_WIDTH]).astype(BF16)
    p = _rms(pool_ref[0], gout[:, MLA_WIDTH:MLA_WIDTH + POOL_WIDTH]).astype(BF16)
    hh = _rms(hy_ref[0], gout[:, MLA_WIDTH + POOL_WIDTH:]).astype(BF16)
    y = (_bdot(a, wout_ref[0:MLA_WIDTH, :])
         + _bdot(p, wout_ref[MLA_WIDTH:MLA_WIDTH + POOL_WIDTH, :])
         + _bdot(hh, wout_ref[MLA_WIDTH + POOL_WIDTH:, :]))
    x1 = x_ref[0] + g1_ref[0] * y
    h2 = (_rms(x1, gmlp_ref[...]) * (1.0 + sc2_ref[0]) + sh2_ref[0]).astype(BF16)
    y2 = jnp.zeros_like(x1)
    for c in range(D_FF // MLP_CHUNK):
        cs = slice(c * MLP_CHUNK, (c + 1) * MLP_CHUNK)
        hid = jnp.maximum(_bdot(h2, w1_ref[:, cs]), 0.0)
        y2 = y2 + _bdot((hid * hid).astype(BF16), w2_ref[cs, :])
    x2 = x1 + g2_ref[0] * y2
    if final_norm:
        x2 = _rms(x2, gfin_ref[...])
    o_ref[0] = x2


def _merge(x, attn, pool, hy, mods, lw, g_final, final_norm, tile):
    bsz, n, _ = x.shape
    nt = n // tile
    tok = lambda w: pl.BlockSpec((1, tile, w), lambda b, i: (b, i, 0))
    mod_spec = lambda k: pl.BlockSpec((1, 1, D_MODEL), lambda b, i, k=k: (b, 0, k))
    return pl.pallas_call(
        functools.partial(_merge_kernel, final_norm=final_norm),
        out_shape=jax.ShapeDtypeStruct((bsz, n, D_MODEL), F32),
        grid=(bsz, nt),
        in_specs=[
            tok(D_MODEL), tok(MLA_WIDTH), tok(POOL_WIDTH), tok(HY_WIDTH),
            mod_spec(2), mod_spec(3), mod_spec(4), mod_spec(5),
            _const_spec((1, D_MODEL)), _const_spec((D_MODEL, D_MODEL)),
            _const_spec((1, D_MODEL)), _const_spec((D_MODEL, D_FF)),
            _const_spec((D_FF, D_MODEL)), _const_spec((1, D_MODEL)),
        ],
        out_specs=tok(D_MODEL),
        compiler_params=_params("parallel", "parallel"),
        name="merge_mlp",
    )(x, attn, pool, hy, mods, mods, mods, mods, lw["g_out"], lw["w_out"], lw["g_mlp"],
      lw["w_mlp1"], lw["w_mlp2"], g_final)


_ROPE_SWAP = np.concatenate([np.arange(8, 16), np.arange(0, 8), np.arange(24, 32), np.arange(16, 24)])


def _rope_tables(n):
    idx = np.arange(n)
    r = (idx // GRID_W).astype(np.float32)
    c = (idx % GRID_W).astype(np.float32)
    n_freq = HEAD_ROPE // 4
    inv = (ROPE_BASE ** (-np.arange(n_freq, dtype=np.float32) / n_freq)).astype(np.float32)
    ar, ac = r[:, None] * inv, c[:, None] * inv
    cos = np.zeros((n, HEAD_PAD), np.float32)
    sin = np.zeros((n, HEAD_PAD), np.float32)
    cos[:, :HEAD_NOPE] = 1.0
    cos[:, HEAD_NOPE:HEAD_NOPE + HEAD_ROPE] = np.concatenate(
        [np.cos(ar), np.cos(ar), np.cos(ac), np.cos(ac)], axis=1)
    sin[:, HEAD_NOPE:HEAD_NOPE + HEAD_ROPE] = np.concatenate(
        [-np.sin(ar), np.sin(ar), -np.sin(ac), np.sin(ac)], axis=1)
    return jnp.asarray(cos), jnp.asarray(sin)


def _identity_tables(n):
    cos = np.zeros((n, HEAD_PAD), np.float32)
    cos[:, :HEAD_NOPE + HEAD_ROPE] = 1.0
    return jnp.asarray(cos), jnp.zeros((n, HEAD_PAD), F32)


def _layer_weights(l, w):
    w_in = w["w_in"][l]
    kr = w_in[:, COL_KR:COL_Q]
    slot = lambda m: jnp.zeros((D_MODEL, HEAD_PAD), F32).at[:, HEAD_NOPE:HEAD_NOPE + HEAD_ROPE].set(m)
    w_in_p = jnp.concatenate([w_in[:, COL_KV:COL_KR], w_in[:, COL_Q:COL_POOL],
                              w_in[:, COL_POOL:COL_HY], w_in[:, COL_HY:],
                              slot(kr), slot(kr[:, _ROPE_SWAP])], axis=1).astype(BF16)
    wkv = w["w_kv_up"][l].reshape(KV_RANK, N_HEADS, HEAD_NOPE + HEAD_V)
    wka = wkv.at[:, :, HEAD_NOPE:].set(0.0).reshape(KV_RANK, N_HEADS * HEAD_PAD).astype(BF16)
    wvt = jnp.zeros((KV_RANK, N_HEADS, V_SLOT), F32).at[:, :, :HEAD_V].set(wkv[:, :, HEAD_NOPE:])
    wvt = wvt.reshape(KV_RANK, VT_ROWS).T.astype(BF16)
    wq = w["w_q_up"][l].reshape(Q_RANK, N_HEADS, HEAD_NOPE + HEAD_ROPE)
    wa = jnp.zeros((Q_RANK, N_HEADS, HEAD_PAD), F32).at[:, :, :HEAD_NOPE + HEAD_ROPE].set(wq)
    wb = jnp.zeros((Q_RANK, N_HEADS, HEAD_PAD), F32).at[:, :, HEAD_NOPE:HEAD_NOPE + HEAD_ROPE].set(
        wq[:, :, HEAD_NOPE:][:, :, _ROPE_SWAP])
    wab = jnp.concatenate([wa.reshape(Q_RANK, -1), wb.reshape(Q_RANK, -1)], axis=1).astype(BF16)
    pool_bd = jnp.zeros((POOL_WIDTH, POOL_WIDTH), F32)
    for g in range(len(POOL_WINDOWS)):
        sl = slice(g * POOL_GROUP, (g + 1) * POOL_GROUP)
        pool_bd = pool_bd.at[sl, sl].set(w["pool_w"][l, g])
    pad_rows = lambda m, rows: jnp.zeros((rows, m.shape[1]), F32).at[:m.shape[0]].set(m)
    pad_cols = lambda m, cols: jnp.zeros((m.shape[0], cols), F32).at[:, :m.shape[1]].set(m)
    row = lambda v: v.reshape(1, -1)
    return {
        "g_mix": row(w["g_mix"][l]), "w_in": w_in_p, "g_kv": row(w["g_kv"][l]), "wka": wka,
        "wvt": wvt, "g_q": row(w["g_q"][l]), "wab": wab,
        "pool_w": pool_bd.astype(BF16), "pool_scale": row(w["pool_scale"][l]),
        "hy_conv_w": w["hy_conv_w"][l], "hy_conv_b": row(w["hy_conv_b"][l]),
        "hy_w1": pad_cols(pad_rows(w["hy_f_w1"][l], LANE), LANE),
        "hy_b1": pad_cols(row(w["hy_f_b1"][l]), LANE),
        "hy_f1": pad_cols(row(w["hy_f_freq1"][l]), LANE),
        "hy_w2": pad_cols(pad_rows(w["hy_f_w2"][l], LANE), LANE),
        "hy_b2": pad_cols(row(w["hy_f_b2"][l]), LANE),
        "hy_f2": pad_cols(row(w["hy_f_freq2"][l]), LANE),
        "hy_w3": pad_rows(w["hy_f_w3"][l], LANE),
        "hy_bias": w["hy_bias"][l].reshape(2, 1, HY_WIDTH),
        "g_out": row(w["g_out"][l]), "w_out": w["w_out"][l].astype(BF16),
        "g_mlp": row(w["g_mlp"][l]), "w_mlp1": w["w_mlp1"][l].astype(BF16),
        "w_mlp2": w["w_mlp2"][l].astype(BF16),
    }


def kernel(x, c, ctx, c_ctx, w_mod, b_mod, g_mix, g_mlp, w_in, g_q, w_q_up, g_kv, w_kv_up,
           pool_w, pool_scale, hy_conv_w, hy_conv_b, hy_f_w1, hy_f_b1, hy_f_freq1, hy_f_w2,
           hy_f_b2, hy_f_freq2, hy_f_w3, hy_bias, g_out, w_out, w_mlp1, w_mlp2, g_final):
    w = dict(g_mix=g_mix, g_mlp=g_mlp, w_in=w_in, g_q=g_q, w_q_up=w_q_up, g_kv=g_kv,
             w_kv_up=w_kv_up, pool_w=pool_w, pool_scale=pool_scale, hy_conv_w=hy_conv_w,
             hy_conv_b=hy_conv_b, hy_f_w1=hy_f_w1, hy_f_b1=hy_f_b1, hy_f_freq1=hy_f_freq1,
             hy_f_w2=hy_f_w2, hy_f_b2=hy_f_b2, hy_f_freq2=hy_f_freq2, hy_f_w3=hy_f_w3,
             hy_bias=hy_bias, g_out=g_out, w_out=w_out, w_mlp1=w_mlp1, w_mlp2=w_mlp2)
    depth = w_mod.shape[0]
    bsz, n, _ = x.shape
    n_ctx = ctx.shape[1]
    tile = min(TOKEN_TILE, n)
    tile_c = min(TOKEN_TILE, n_ctx)

    rows = -(-(bsz + 1) // 8) * 8
    cond = jnp.zeros((rows, D_MODEL), F32).at[:bsz].set(c).at[bsz].set(c_ctx)
    mods_all = _modulation(cond, w_mod, b_mod)
    cos_l, sin_l = _rope_tables(n)
    cos_c, sin_c = _identity_tables(n_ctx)
    g_fin = g_final.reshape(1, D_MODEL)

    xc = ctx
    for l in range(depth):
        last = l == depth - 1
        lw = _layer_weights(l, w)
        mods = mods_all[l, :bsz].reshape(bsz, 1, 6 * D_MODEL)
        mods_c = jnp.broadcast_to(mods_all[l, bsz].reshape(1, 1, 6 * D_MODEL),
                                  (bsz, 1, 6 * D_MODEL))
        q, k, vt, pool_u, hy_u = _inproj(x, (mods, 0), (mods, 1), lw, cos_l, sin_l, tile)
        qc, kc, vtc, pool_uc, hy_uc = _inproj(xc, (mods_c, 0), (mods_c, 1), lw, cos_c, sin_c,
                                              tile_c)
        attn = _attention(q, [(kc, vtc), (k, vt)])
        pool = _pool(pool_u, lw["pool_w"], lw["pool_scale"])
        hy = _hyena(hy_u, lw)
        x = _merge(x, attn, pool, hy, mods, lw, g_fin, last, tile)
        if not last:
            attn_c = _attention(qc, [(kc, vtc)])
            pool_c = _pool(pool_uc, lw["pool_w"], lw["pool_scale"])
            hy_c = _hyena(hy_uc, lw)
            xc = _merge(xc, attn_c, pool_c, hy_c, mods_c, lw, g_fin, False, tile_c)
    return x
```

```python
import functools
import math

import jax
import jax.numpy as jnp
import numpy as np
from jax import lax
from jax.experimental import pallas as pl
from jax.experimental.pallas import tpu as pltpu

F32 = jnp.float32
BF16 = jnp.bfloat16

D_MODEL = 1024
GRID_W = 64
EPS = 1e-6
N_HEADS = 8
HEAD_V = 64
HEAD_NOPE = 64
HEAD_ROPE = 32
HEAD_PAD = 128
Q_RANK = 256
KV_RANK = 128
MLA_WIDTH = N_HEADS * HEAD_V
V_SLOT = 80
VT_ROWS = N_HEADS * V_SLOT
POOL_WIDTH = 256
HY_WIDTH = 256
MLA_SCALE = (HEAD_NOPE + HEAD_ROPE) ** -0.5
Q_SCALE = MLA_SCALE * math.log2(math.e)
ROPE_BASE = 10000.0
POOL_WINDOWS = (2, 4, 8, 16)
POOL_GROUP = POOL_WIDTH // len(POOL_WINDOWS)
POOL_PAD = 8
EDGE_ROWS = 16
HY_EMB = 33
HY_FFN = 64
HY_DECAY_TARGET = 1e-2
HY_DECAY_SHORT_PCT = 0.3
HY_DECAY_LONG_PCT = 1.5
D_FF = 4 * D_MODEL
COL_KV, COL_KR, COL_Q, COL_POOL, COL_HY, N_IN = 0, 128, 160, 416, 672, 1440
P_KV, P_Q, P_POOL, P_HY, P_KRA, P_KRB, N_IN_P = 0, 128, 384, 640, 1408, 1536, 1664

LANE = 128
VMEM_LIMIT = 56 * 1024 * 1024
TOKEN_TILE = 512
Q_TILE = 512
CHUNK_UNROLL = 8
CONV_BLOCK = 512
FREQ_BLOCK = 128
FREQ_ROWS = 16
MLP_CHUNK = 1024
HIGHEST = lax.Precision.HIGHEST
HIGH = lax.Precision.HIGH


def _params(*sem):
    return pltpu.CompilerParams(dimension_semantics=sem, vmem_limit_bytes=VMEM_LIMIT)


def _const_spec(shape):
    zeros = (0,) * len(shape)
    return pl.BlockSpec(shape, lambda *_: zeros, pipeline_mode=pl.Buffered(1))


def _rms(x, g):
    return x * lax.rsqrt(jnp.mean(x * x, axis=-1, keepdims=True) + EPS) * g


def _bdot(a, b):
    return jnp.dot(a, b, preferred_element_type=F32)


def _split(x):
    hi = x.astype(BF16)
    return hi, (x - hi.astype(F32)).astype(BF16)


def _dot3(a, b):
    return _bdot(a[0], b[0]) + (_bdot(a[0], b[1]) + _bdot(a[1], b[0]))


def _modulation_kernel(c_ref, w_ref, b_ref, o_ref):
    c = c_ref[...]
    s = c / (1.0 + jnp.exp(-c))
    o_ref[0] = jnp.dot(s, w_ref[0], preferred_element_type=F32, precision=HIGHEST) + b_ref[0]


def _modulation(cond, w_mod, b_mod):
    depth = w_mod.shape[0]
    rows = cond.shape[0]
    return pl.pallas_call(
        _modulation_kernel,
        out_shape=jax.ShapeDtypeStruct((depth, rows, 6 * D_MODEL), F32),
        grid=(depth, 6),
        in_specs=[
            pl.BlockSpec((rows, D_MODEL), lambda l, j: (0, 0)),
            pl.BlockSpec((1, D_MODEL, D_MODEL), lambda l, j: (l, 0, j)),
            pl.BlockSpec((1, 1, D_MODEL), lambda l, j: (l, 0, j)),
        ],
        out_specs=pl.BlockSpec((1, rows, D_MODEL), lambda l, j: (l, 0, j)),
        compiler_params=_params("arbitrary", "arbitrary"),
        name="modulation",
    )(cond, w_mod, b_mod.reshape(depth, 1, 6 * D_MODEL))


def _inproj_kernel(x_ref, sh_ref, sc_ref, gmix_ref, win_ref, gkv_ref, wka_ref, wvt_ref,
                   gq_ref, wab_ref, cos_ref, sin_ref,
                   q_ref, k_ref, vt_ref, pool_ref, hy_ref):
    x = x_ref[0]
    h = (_rms(x, gmix_ref[...]) * (1.0 + sc_ref[0]) + sh_ref[0]).astype(BF16)
    proj = _bdot(h, win_ref[...])
    pool_ref[0] = proj[:, P_POOL:P_HY]
    hy_ref[0] = proj[:, P_HY:P_KRA].astype(hy_ref.dtype)
    kvn = _rms(proj[:, P_KV:P_Q], gkv_ref[...]).astype(BF16)
    kpad = _bdot(kvn, wka_ref[...])
    vt = lax.dot_general(wvt_ref[...], kvn, (((1,), (1,)), ((), ())),
                         preferred_element_type=F32)
    row = lax.broadcasted_iota(jnp.int32, vt.shape, 0)
    vt = jnp.where(row % V_SLOT >= HEAD_V, 1.0, vt).astype(vt_ref.dtype)
    vt_ref[0, 0] = vt.reshape(N_HEADS, V_SLOT, vt.shape[1])
    qn = _rms(proj[:, P_Q:P_POOL], gq_ref[...]).astype(BF16)
    ab = _bdot(qn, wab_ref[...])
    cos = cos_ref[...]
    sin = sin_ref[...]
    kr = proj[:, P_KRA:P_KRB] * cos + proj[:, P_KRB:N_IN_P] * sin
    width = N_HEADS * HEAD_PAD
    for hd in range(N_HEADS):
        sl = slice(hd * HEAD_PAD, (hd + 1) * HEAD_PAD)
        sl_b = slice(width + hd * HEAD_PAD, width + (hd + 1) * HEAD_PAD)
        k_ref[0, hd] = (kpad[:, sl] + kr).astype(k_ref.dtype)
        q_ref[0, hd] = ((ab[:, sl] * cos + ab[:, sl_b] * sin) * Q_SCALE).astype(q_ref.dtype)


def _inproj(x, shift, scale, lw, cos_t, sin_t, tile):
    bsz, n, _ = x.shape
    nt = n // tile
    width = N_HEADS * HEAD_PAD
    mod_spec = lambda k: pl.BlockSpec((1, 1, D_MODEL), lambda b, i, k=k: (b, 0, k))
    return pl.pallas_call(
        _inproj_kernel,
        out_shape=(
            jax.ShapeDtypeStruct((bsz, N_HEADS, n, HEAD_PAD), BF16),
            jax.ShapeDtypeStruct((bsz, N_HEADS, n, HEAD_PAD), BF16),
            jax.ShapeDtypeStruct((bsz, nt, N_HEADS, V_SLOT, tile), BF16),
            jax.ShapeDtypeStruct((bsz, n, POOL_WIDTH), F32),
            jax.ShapeDtypeStruct((bsz, n, 3 * HY_WIDTH), BF16),
        ),
        grid=(bsz, nt),
        in_specs=[
            pl.BlockSpec((1, tile, D_MODEL), lambda b, i: (b, i, 0)),
            mod_spec(shift[1]), mod_spec(scale[1]),
            _const_spec((1, D_MODEL)),
            _const_spec((D_MODEL, N_IN_P)),
            _const_spec((1, KV_RANK)),
            _const_spec((KV_RANK, width)),
            _const_spec((VT_ROWS, KV_RANK)),
            _const_spec((1, Q_RANK)),
            _const_spec((Q_RANK, 2 * width)),
            pl.BlockSpec((tile, HEAD_PAD), lambda b, i: (i, 0)),
            pl.BlockSpec((tile, HEAD_PAD), lambda b, i: (i, 0)),
        ],
        out_specs=(
            pl.BlockSpec((1, N_HEADS, tile, HEAD_PAD), lambda b, i: (b, 0, i, 0)),
            pl.BlockSpec((1, N_HEADS, tile, HEAD_PAD), lambda b, i: (b, 0, i, 0)),
            pl.BlockSpec((1, 1, N_HEADS, V_SLOT, tile), lambda b, i: (b, i, 0, 0, 0)),
            pl.BlockSpec((1, tile, POOL_WIDTH), lambda b, i: (b, i, 0)),
            pl.BlockSpec((1, tile, 3 * HY_WIDTH), lambda b, i: (b, i, 0)),
        ),
        compiler_params=_params("parallel", "parallel"),
        name="inproj",
    )(x, shift[0], scale[0], lw["g_mix"], lw["w_in"], lw["g_kv"], lw["wka"], lw["wvt"],
      lw["g_q"], lw["wab"], cos_t, sin_t)


def _attention_kernel(*refs, seg_chunks):
    q_ref = refs[0]
    n_seg = len(seg_chunks)
    seg_refs = refs[1:1 + 2 * n_seg]
    o_ref = refs[1 + 2 * n_seg]
    out_ref, m_ref = refs[2 + 2 * n_seg], refs[3 + 2 * n_seg]
    s_bufs = refs[4 + 2 * n_seg:]
    tq = q_ref.shape[2]
    step_id = pl.program_id(0) + pl.program_id(1) + N_HEADS

    def stage(h1, h2, parity1):
        buf1, buf2 = s_bufs[parity1], s_bufs[1 - parity1]
        m2 = m_ref[...] if h2 is not None else None
        m1 = jnp.full((1, tq), -jnp.inf, F32)
        acc2 = jnp.zeros((V_SLOT, tq), F32)
        base = 0
        for si, (n_chunks, chunk) in enumerate(seg_chunks):
            k_ref, vt_ref = seg_refs[2 * si], seg_refs[2 * si + 1]
            for c in range(n_chunks):
                rows = slice(base + c * chunk, base + (c + 1) * chunk)
                if h1 is not None:
                    s = lax.dot_general(k_ref[0, h1, c * chunk:(c + 1) * chunk, :], q_ref[0, h1],
                                        (((1,), (1,)), ((), ())), preferred_element_type=F32)
                    buf1[rows, :] = s
                    m1 = jnp.maximum(m1, jnp.max(s, axis=0, keepdims=True))
                if h2 is not None:
                    p = jnp.exp2(buf2[rows, :] - m2).astype(BF16)
                    acc2 = acc2 + _bdot(vt_ref[0, c, h2], p)
            base += n_chunks * chunk
        if h2 is not None:
            out_ref[h2] = acc2[:HEAD_V, :] / acc2[HEAD_V:HEAD_V + 1, :]
        if h1 is not None:
            m_ref[...] = m1

    for hd in range(-1, N_HEADS):
        h1 = hd + 1 if hd + 1 < N_HEADS else None
        h2 = hd if hd >= 0 else None
        pl.when(step_id >= hd)(functools.partial(stage, h1, h2, (hd + 1) % 2))
    o_ref[0] = out_ref[...].reshape(MLA_WIDTH, tq).T


def _attention(q, segments):
    bsz, _, n, _ = q.shape
    tq = min(Q_TILE, n)
    in_specs = [pl.BlockSpec((1, N_HEADS, tq, HEAD_PAD), lambda b, i: (b, 0, i, 0))]
    args = [q]
    seg_chunks = []
    for k, vt in segments:
        nk = k.shape[2]
        n_chunks, chunk = vt.shape[1], vt.shape[4]
        assert n_chunks * chunk == nk
        seg_chunks.append((n_chunks, chunk))
        in_specs.append(pl.BlockSpec((1, N_HEADS, nk, HEAD_PAD), lambda b, i: (b, 0, 0, 0)))
        in_specs.append(pl.BlockSpec((1, n_chunks, N_HEADS, V_SLOT, chunk),
                                     lambda b, i: (b, 0, 0, 0, 0)))
        args += [k, vt]
    n_keys = sum(nc * ch for nc, ch in seg_chunks)
    return pl.pallas_call(
        functools.partial(_attention_kernel, seg_chunks=tuple(seg_chunks)),
        out_shape=jax.ShapeDtypeStruct((bsz, n, MLA_WIDTH), F32),
        grid=(bsz, n // tq),
        in_specs=in_specs,
        out_specs=pl.BlockSpec((1, tq, MLA_WIDTH), lambda b, i: (b, i, 0)),
        scratch_shapes=[pltpu.VMEM((N_HEADS, HEAD_V, tq), F32), pltpu.VMEM((1, tq), F32),
                        pltpu.VMEM((n_keys, tq), F32), pltpu.VMEM((n_keys, tq), F32)],
        compiler_params=_params("parallel", "arbitrary"),
        name="attention",
    )(*args)


def _shift_rows(a, k):
    n = a.shape[0]
    return pltpu.roll(a, k % n, 0)


def _pool_kernel(u_ref, w_ref, scale_ref, o_ref, ext_ref):
    n = u_ref.shape[1]
    u = u_ref[0]
    zeros = jnp.zeros((POOL_PAD, POOL_WIDTH), F32)
    ext_ref[0:POOL_PAD, :] = zeros
    ext_ref[POOL_PAD + n:, :] = zeros
    ext_ref[POOL_PAD:POOL_PAD + n, :] = u
    ext = ext_ref[...]
    s2 = ext + _shift_rows(ext, 1)
    s4 = _shift_rows(s2, 1) + _shift_rows(s2, -1)
    s8 = _shift_rows(s4, 2) + _shift_rows(s4, -2)
    s16 = _shift_rows(s8, 4) + _shift_rows(s8, -4)
    t = lax.broadcasted_iota(jnp.int32, (n, POOL_WIDTH), 0)
    lane = lax.broadcasted_iota(jnp.int32, (n, POOL_WIDTH), 1)
    mean = None
    for g, (w, s) in enumerate(zip(POOL_WINDOWS, (s2, s4, s8, s16))):
        lo = jnp.clip(t - w // 2, 0, n)
        hi = jnp.clip(t + w // 2, 0, n)
        mg = s[POOL_PAD:POOL_PAD + n, :] / (hi - lo).astype(F32)
        mean = mg if mean is None else jnp.where(lane >= g * POOL_GROUP, mg, mean)
    diff = (mean - u).astype(BF16)
    o_ref[0] = _bdot(diff, w_ref[...]) * scale_ref[...]


def _pool(u, w_bd, scale):
    bsz, n, _ = u.shape
    return pl.pallas_call(
        _pool_kernel,
        out_shape=jax.ShapeDtypeStruct((bsz, n, POOL_WIDTH), F32),
        grid=(bsz,),
        in_specs=[
            pl.BlockSpec((1, n, POOL_WIDTH), lambda b: (b, 0, 0)),
            _const_spec((POOL_WIDTH, POOL_WIDTH)),
            _const_spec((1, POOL_WIDTH)),
        ],
        out_specs=pl.BlockSpec((1, n, POOL_WIDTH), lambda b: (b, 0, 0)),
        scratch_shapes=[pltpu.VMEM((n + 2 * POOL_PAD, POOL_WIDTH), F32)],
        compiler_params=_params("parallel"),
        name="pool_mixer",
    )(u, w_bd, scale)


def _filter_kernel(z_ref, w1_ref, b1_ref, f1_ref, w2_ref, b2_ref, f2_ref, w3h_ref, w3l_ref,
                   dl_ref, g_ref, asum_ref, *, n):
    i = pl.program_id(0)
    rows_per = z_ref.shape[0]
    z = z_ref[...]
    h = jnp.sin(f1_ref[...] * (jnp.dot(z, w1_ref[...], preferred_element_type=F32,
                                       precision=HIGHEST) + b1_ref[...]))
    h = jnp.sin(f2_ref[...] * (jnp.dot(h, w2_ref[...], preferred_element_type=F32,
                                       precision=HIGHEST) + b2_ref[...]))
    h = _dot3(_split(h), (w3h_ref[...], w3l_ref[...]))
    r = i * rows_per + lax.broadcasted_iota(jnp.int32, (rows_per, HY_WIDTH), 0)
    backward = r < n
    pos = jnp.where(backward, n - r, r - n)
    t = pos.astype(F32) * (1.0 / (n - 1))
    decay = jnp.exp(-t * dl_ref[...])
    valid = r > 0

    @pl.when(i == 0)
    def _():
        asum_ref[...] = jnp.zeros_like(asum_ref)

    for o in range(2):
        fwd = h[:, (2 * o) * HY_WIDTH:(2 * o + 1) * HY_WIDTH]
        bwd = h[:, (2 * o + 1) * HY_WIDTH:(2 * o + 2) * HY_WIDTH]
        g = jnp.where(valid, jnp.where(backward, bwd, fwd) * decay, 0.0)
        g_ref[o] = g
        asum_ref[o] += jnp.sum(jnp.abs(g), axis=0, keepdims=True)


def _hyena_filter(n, lw):
    rows = 2 * n
    tile = min(rows, 1024)
    r = np.arange(rows)
    pos = np.where(r < n, n - r, r - n).astype(np.float64)
    t = pos / (n - 1)
    bands = (HY_EMB - 1) // 2
    freqs = np.linspace(1e-4, bands - 1, bands)[None, :]
    wpos = 2.0 * math.pi * pos[:, None] / n
    z = np.zeros((rows, LANE), np.float32)
    z[:, 0] = t
    z[:, 1:1 + bands] = np.cos(freqs * wpos)
    z[:, 1 + bands:HY_EMB] = -np.sin(freqs * wpos)
    deltas = np.abs(np.linspace(math.log(HY_DECAY_TARGET) / HY_DECAY_LONG_PCT,
                                math.log(HY_DECAY_TARGET) / HY_DECAY_SHORT_PCT, HY_WIDTH))
    deltas = jnp.asarray(deltas[None, :], F32)
    return pl.pallas_call(
        functools.partial(_filter_kernel, n=n),
        out_shape=(jax.ShapeDtypeStruct((2, rows, HY_WIDTH), F32),
                   jax.ShapeDtypeStruct((2, 1, HY_WIDTH), F32)),
        grid=(rows // tile,),
        in_specs=[
            pl.BlockSpec((tile, LANE), lambda i: (i, 0)),
            _const_spec((LANE, LANE)), _const_spec((1, LANE)), _const_spec((1, LANE)),
            _const_spec((LANE, LANE)), _const_spec((1, LANE)), _const_spec((1, LANE)),
            _const_spec((LANE, 4 * HY_WIDTH)), _const_spec((LANE, 4 * HY_WIDTH)),
            _const_spec((1, HY_WIDTH)),
        ],
        out_specs=(pl.BlockSpec((2, tile, HY_WIDTH), lambda i: (0, i, 0)),
                   pl.BlockSpec((2, 1, HY_WIDTH), lambda i: (0, 0, 0))),
        compiler_params=_params("arbitrary"),
        name="hyena_filter",
    )(jnp.asarray(z), lw["hy_w1"], lw["hy_b1"], lw["hy_f1"], lw["hy_w2"], lw["hy_b2"],
      lw["hy_f2"], *_split(lw["hy_w3"]), deltas)


def _dft_tables(blk):
    f = np.arange(blk, dtype=np.float64)[:, None]
    m = np.arange(blk, dtype=np.float64)[None, :]
    theta = math.pi * (2.0 * f + 1.0) * m / (2.0 * blk)
    return np.cos(theta), np.sin(theta)


def _spectra_kernel(g_ref, asum_ref, fwdh_ref, fwdl_ref, k_ref, prev_ref):
    e = pl.program_id(1)
    blk = g_ref.shape[1]
    s = _dot3((fwdh_ref[...], fwdl_ref[...]), _split(g_ref[0]))
    s = s / asum_ref[0]

    @pl.when(e > 0)
    def _():
        prev = prev_ref[...]
        f = lax.broadcasted_iota(jnp.int32, (blk, HY_WIDTH), 0)
        sgn = jnp.where(f % 2 == 0, 1.0, -1.0).astype(F32)
        k_ref[0, 0, 0] = s[:blk] - sgn * prev[blk:]
        k_ref[0, 0, 1] = s[blk:] + sgn * prev[:blk]

    prev_ref[...] = s


def _hyena_spectra(g, asum, blk):
    rows = g.shape[1]
    nseg = rows // blk
    cos, sin = _dft_tables(blk)
    fwd = jnp.asarray(np.concatenate([cos, -sin], axis=0), F32)
    return pl.pallas_call(
        _spectra_kernel,
        out_shape=jax.ShapeDtypeStruct((2, nseg - 1, 2, blk, HY_WIDTH), F32),
        grid=(2, nseg),
        in_specs=[
            pl.BlockSpec((1, blk, HY_WIDTH), lambda o, e: (o, e, 0)),
            pl.BlockSpec((1, 1, HY_WIDTH), lambda o, e: (o, 0, 0)),
            _const_spec((2 * blk, blk)), _const_spec((2 * blk, blk)),
        ],
        out_specs=pl.BlockSpec((1, 1, 2, blk, HY_WIDTH),
                               lambda o, e: (o, jnp.maximum(e - 1, 0), 0, 0, 0)),
        scratch_shapes=[pltpu.VMEM((2 * blk, HY_WIDTH), F32)],
        compiler_params=_params("arbitrary", "arbitrary"),
        name="hyena_spectra",
    )(g, asum, *_split(fwd))


def _short_conv_rows(src_ref, rows, w_ref, b_ref):
    r0, r1 = rows
    n = src_ref.shape[1]
    cur = src_ref[0, r0:r1, :].astype(F32)
    row = lax.broadcasted_iota(jnp.int32, cur.shape, 0)
    edge = jnp.zeros((1, cur.shape[1]), F32)
    before = src_ref[0, r0 - EDGE_ROWS:r0, :].astype(F32)[EDGE_ROWS - 1:, :] if r0 > 0 else edge
    after = src_ref[0, r1:r1 + EDGE_ROWS, :].astype(F32)[:1, :] if r1 < n else edge
    prev = jnp.where(row == 0, before, _shift_rows(cur, 1))
    nxt = jnp.where(row == r1 - r0 - 1, after, _shift_rows(cur, -1))
    return prev * w_ref[0:1, :] + cur * w_ref[1:2, :] + nxt * w_ref[2:3, :] + b_ref[...]


def _conv_kernel(u_ref, gate_ref, uw_ref, ub_ref, gw_ref, gb_ref, bias_ref, k_ref, fwd_ref,
                 inv_ref, o_ref, u32_ref, u16_ref, acc_ref, uf_ref, yf_ref, *, conv_input):
    f = pl.program_id(1)
    n = u_ref.shape[1]
    blk = fwd_ref.shape[2]
    nb = n // blk
    fb = fwd_ref.shape[1] // 2

    @pl.when(f == 0)
    def _():
        for j in range(nb):
            rows = (j * blk, (j + 1) * blk)
            if conv_input:
                u = _short_conv_rows(u_ref, rows, uw_ref, ub_ref)
            else:
                u = u_ref[0, rows[0]:rows[1], :].astype(F32)
            u32_ref[rows[0]:rows[1], :] = u
            u16_ref[rows[0]:rows[1], :] = u.astype(BF16)
        acc_ref[...] = jnp.zeros_like(acc_ref)

    fwd = fwd_ref[0]
    for j in range(nb):
        uf_ref[j] = _bdot(fwd, u16_ref[j * blk:(j + 1) * blk, :])

    def rows_step(c, carry):
        r0 = pl.multiple_of(c * FREQ_ROWS, FREQ_ROWS)
        rs = pl.ds(r0, FREQ_ROWS)
        rs_im = pl.ds(r0 + fb, FREQ_ROWS)
        for i in range(nb):
            yr = jnp.zeros((FREQ_ROWS, HY_WIDTH), F32)
            yi = jnp.zeros((FREQ_ROWS, HY_WIDTH), F32)
            for j in range(nb):
                d = i - j + nb - 1
                kr = k_ref[0, d, 0, rs, :]
                ki = k_ref[0, d, 1, rs, :]
                ur = uf_ref[j, rs, :]
                ui = uf_ref[j, rs_im, :]
                yr = yr + (kr * ur - ki * ui)
                yi = yi + (kr * ui + ki * ur)
            yf_ref[i, rs, :] = yr.astype(BF16)
            yf_ref[i, rs_im, :] = yi.astype(BF16)
        return carry

    lax.fori_loop(0, fb // FREQ_ROWS, rows_step, 0)

    inv = inv_ref[0]
    for i in range(nb):
        acc_ref[i * blk:(i + 1) * blk, :] += _bdot(inv, yf_ref[i])

    @pl.when(f == pl.num_programs(1) - 1)
    def _():
        bias = bias_ref[0]
        for j in range(nb):
            rows = (j * blk, (j + 1) * blk)
            rs = slice(rows[0], rows[1])
            gate = _short_conv_rows(gate_ref, rows, gw_ref, gb_ref)
            conv = acc_ref[rs, :] + u32_ref[rs, :] * bias
            o_ref[0, rs, :] = (gate * conv).astype(o_ref.dtype)


def _hyena_conv(u_arr, u_col, gate_arr, gate_col, lw, khat, order, blk, conv_input, out_dtype):
    bsz, n, _ = gate_arr.shape
    nb = n // blk
    fb = min(FREQ_BLOCK, blk)
    nf = blk // fb
    cos, sin = _dft_tables(blk)
    fwd = np.concatenate([cos.reshape(nf, fb, blk), -sin.reshape(nf, fb, blk)], axis=1)
    inv = np.concatenate([cos.T.reshape(blk, nf, fb), -sin.T.reshape(blk, nf, fb)], axis=2)
    inv = np.transpose(inv, (1, 0, 2)) / blk
    cw, cb = lw["hy_conv_w"], lw["hy_conv_b"]
    col = lambda c: (lambda b, f, c=c: (0, c))
    return pl.pallas_call(
        functools.partial(_conv_kernel, conv_input=conv_input),
        out_shape=jax.ShapeDtypeStruct((bsz, n, HY_WIDTH), out_dtype),
        grid=(bsz, nf),
        in_specs=[
            pl.BlockSpec((1, n, HY_WIDTH), lambda b, f, c=u_col: (b, 0, c)),
            pl.BlockSpec((1, n, HY_WIDTH), lambda b, f, c=gate_col: (b, 0, c)),
            pl.BlockSpec((3, HY_WIDTH), col(u_col if conv_input else 0)),
            pl.BlockSpec((1, HY_WIDTH), col(u_col if conv_input else 0)),
            pl.BlockSpec((3, HY_WIDTH), col(gate_col)),
            pl.BlockSpec((1, HY_WIDTH), col(gate_col)),
            pl.BlockSpec((1, 1, HY_WIDTH), lambda b, f, o=order: (o, 0, 0)),
            pl.BlockSpec((1, 2 * nb - 1, 2, fb, HY_WIDTH), lambda b, f, o=order: (o, 0, 0, f, 0)),
            pl.BlockSpec((1, 2 * fb, blk), lambda b, f: (f, 0, 0)),
            pl.BlockSpec((1, blk, 2 * fb), lambda b, f: (f, 0, 0)),
        ],
        out_specs=pl.BlockSpec((1, n, HY_WIDTH), lambda b, f: (b, 0, 0)),
        scratch_shapes=[
            pltpu.VMEM((n, HY_WIDTH), F32),
            pltpu.VMEM((n, HY_WIDTH), BF16),
            pltpu.VMEM((n, HY_WIDTH), F32),
            pltpu.VMEM((nb, 2 * fb, HY_WIDTH), F32),
            pltpu.VMEM((nb, 2 * fb, HY_WIDTH), BF16),
        ],
        compiler_params=_params("parallel", "arbitrary"),
        name="hyena_conv%d" % order,
    )(u_arr, gate_arr, cw, cb, cw, cb, lw["hy_bias"], khat,
      jnp.asarray(fwd, BF16), jnp.asarray(inv, BF16))


def _hyena(hy_u, lw):
    n = hy_u.shape[1]
    blk = min(CONV_BLOCK, n)
    g, asum = _hyena_filter(n, lw)
    khat = _hyena_spectra(g, asum, blk)
    z = _hyena_conv(hy_u, 0, hy_u, 1, lw, khat, 0, blk, True, BF16)
    return _hyena_conv(z, 0, hy_u, 2, lw, khat, 1, blk, False, F32)


def _merge_kernel(x_ref, attn_ref, pool_ref, hy_ref, g1_ref, sh2_ref, sc2_ref, g2_ref,
                  gout_ref, wout_ref, gmlp_ref, w1_ref, w2_ref, gfin_ref, o_ref, *, final_norm):
    gout = gout_ref[...]
    a = _rms(attn_ref[0], gout[:, :MLA_WIDTH]).astype(BF16)
    p = _rms(pool_ref[0], gout[:, MLA_WIDTH:MLA_WIDTH + POOL_WIDTH]).astype(BF16)
    hh = _rms(hy_ref[0], gout[:, MLA_WIDTH + POOL_WIDTH:]).astype(BF16)
    y = (_bdot(a, wout_ref[0:MLA_WIDTH, :])
         + _bdot(p, wout_ref[MLA_WIDTH:MLA_WIDTH + POOL_WIDTH, :])
         + _bdot(hh, wout_ref[MLA_WIDTH + POOL_WIDTH:, :]))
    x1 = x_ref[0] + g1_ref[0] * y
    h2 = (_rms(x1, gmlp_ref[...]) * (1.0 + sc2_ref[0]) + sh2_ref[0]).astype(BF16)
    y2 = jnp.zeros_like(x1)
    for c in range(D_FF // MLP_CHUNK):
        cs = slice(c * MLP_CHUNK, (c + 1) * MLP_CHUNK)
        hid = jnp.maximum(_bdot(h2, w1_ref[:, cs]), 0.0)
        y2 = y2 + _bdot((hid * hid).astype(BF16), w2_ref[cs, :])
    x2 = x1 + g2_ref[0] * y2
    if final_norm:
        x2 = _rms(x2, gfin_ref[...])
    o_ref[0] = x2


def _merge(x, attn, pool, hy, mods, lw, g_final, final_norm, tile):
    bsz, n, _ = x.shape
    nt = n // tile
    tok = lambda w: pl.BlockSpec((1, tile, w), lambda b, i: (b, i, 0))
    mod_spec = lambda k: pl.BlockSpec((1, 1, D_MODEL), lambda b, i, k=k: (b, 0, k))
    return pl.pallas_call(
        functools.partial(_merge_kernel, final_norm=final_norm),
        out_shape=jax.ShapeDtypeStruct((bsz, n, D_MODEL), F32),
        grid=(bsz, nt),
        in_specs=[
            tok(D_MODEL), tok(MLA_WIDTH), tok(POOL_WIDTH), tok(HY_WIDTH),
            mod_spec(2), mod_spec(3), mod_spec(4), mod_spec(5),
            _const_spec((1, D_MODEL)), _const_spec((D_MODEL, D_MODEL)),
            _const_spec((1, D_MODEL)), _const_spec((D_MODEL, D_FF)),
            _const_spec((D_FF, D_MODEL)), _const_spec((1, D_MODEL)),
        ],
        out_specs=tok(D_MODEL),
        compiler_params=_params("parallel", "parallel"),
        name="merge_mlp",
    )(x, attn, pool, hy, mods, mods, mods, mods, lw["g_out"], lw["w_out"], lw["g_mlp"],
      lw["w_mlp1"], lw["w_mlp2"], g_final)


_ROPE_SWAP = np.concatenate([np.arange(8, 16), np.arange(0, 8), np.arange(24, 32), np.arange(16, 24)])


def _rope_tables(n):
    idx = np.arange(n)
    r = (idx // GRID_W).astype(np.float32)
    c = (idx % GRID_W).astype(np.float32)
    n_freq = HEAD_ROPE // 4
    inv = (ROPE_BASE ** (-np.arange(n_freq, dtype=np.float32) / n_freq)).astype(np.float32)
    ar, ac = r[:, None] * inv, c[:, None] * inv
    cos = np.zeros((n, HEAD_PAD), np.float32)
    sin = np.zeros((n, HEAD_PAD), np.float32)
    cos[:, :HEAD_NOPE] = 1.0
    cos[:, HEAD_NOPE:HEAD_NOPE + HEAD_ROPE] = np.concatenate(
        [np.cos(ar), np.cos(ar), np.cos(ac), np.cos(ac)], axis=1)
    sin[:, HEAD_NOPE:HEAD_NOPE + HEAD_ROPE] = np.concatenate(
        [-np.sin(ar), np.sin(ar), -np.sin(ac), np.sin(ac)], axis=1)
    return jnp.asarray(cos), jnp.asarray(sin)


def _identity_tables(n):
    cos = np.zeros((n, HEAD_PAD), np.float32)
    cos[:, :HEAD_NOPE + HEAD_ROPE] = 1.0
    return jnp.asarray(cos), jnp.zeros((n, HEAD_PAD), F32)


def _layer_weights(l, w):
    w_in = w["w_in"][l]
    kr = w_in[:, COL_KR:COL_Q]
    slot = lambda m: jnp.zeros((D_MODEL, HEAD_PAD), F32).at[:, HEAD_NOPE:HEAD_NOPE + HEAD_ROPE].set(m)
    w_in_p = jnp.concatenate([w_in[:, COL_KV:COL_KR], w_in[:, COL_Q:COL_POOL],
                              w_in[:, COL_POOL:COL_HY], w_in[:, COL_HY:],
                              slot(kr), slot(kr[:, _ROPE_SWAP])], axis=1).astype(BF16)
    wkv = w["w_kv_up"][l].reshape(KV_RANK, N_HEADS, HEAD_NOPE + HEAD_V)
    wka = wkv.at[:, :, HEAD_NOPE:].set(0.0).reshape(KV_RANK, N_HEADS * HEAD_PAD).astype(BF16)
    wvt = jnp.zeros((KV_RANK, N_HEADS, V_SLOT), F32).at[:, :, :HEAD_V].set(wkv[:, :, HEAD_NOPE:])
    wvt = wvt.reshape(KV_RANK, VT_ROWS).T.astype(BF16)
    wq = w["w_q_up"][l].reshape(Q_RANK, N_HEADS, HEAD_NOPE + HEAD_ROPE)
    wa = jnp.zeros((Q_RANK, N_HEADS, HEAD_PAD), F32).at[:, :, :HEAD_NOPE + HEAD_ROPE].set(wq)
    wb = jnp.zeros((Q_RANK, N_HEADS, HEAD_PAD), F32).at[:, :, HEAD_NOPE:HEAD_NOPE + HEAD_ROPE].set(
        wq[:, :, HEAD_NOPE:][:, :, _ROPE_SWAP])
    wab = jnp.concatenate([wa.reshape(Q_RANK, -1), wb.reshape(Q_RANK, -1)], axis=1).astype(BF16)
    pool_bd = jnp.zeros((POOL_WIDTH, POOL_WIDTH), F32)
    for g in range(len(POOL_WINDOWS)):
        sl = slice(g * POOL_GROUP, (g + 1) * POOL_GROUP)
        pool_bd = pool_bd.at[sl, sl].set(w["pool_w"][l, g])
    pad_rows = lambda m, rows: jnp.zeros((rows, m.shape[1]), F32).at[:m.shape[0]].set(m)
    pad_cols = lambda m, cols: jnp.zeros((m.shape[0], cols), F32).at[:, :m.shape[1]].set(m)
    row = lambda v: v.reshape(1, -1)
    return {
        "g_mix": row(w["g_mix"][l]), "w_in": w_in_p, "g_kv": row(w["g_kv"][l]), "wka": wka,
        "wvt": wvt, "g_q": row(w["g_q"][l]), "wab": wab,
        "pool_w": pool_bd.astype(BF16), "pool_scale": row(w["pool_scale"][l]),
        "hy_conv_w": w["hy_conv_w"][l], "hy_conv_b": row(w["hy_conv_b"][l]),
        "hy_w1": pad_cols(pad_rows(w["hy_f_w1"][l], LANE), LANE),
        "hy_b1": pad_cols(row(w["hy_f_b1"][l]), LANE),
        "hy_f1": pad_cols(row(w["hy_f_freq1"][l]), LANE),
        "hy_w2": pad_cols(pad_rows(w["hy_f_w2"][l], LANE), LANE),
        "hy_b2": pad_cols(row(w["hy_f_b2"][l]), LANE),
        "hy_f2": pad_cols(row(w["hy_f_freq2"][l]), LANE),
        "hy_w3": pad_rows(w["hy_f_w3"][l], LANE),
        "hy_bias": w["hy_bias"][l].reshape(2, 1, HY_WIDTH),
        "g_out": row(w["g_out"][l]), "w_out": w["w_out"][l].astype(BF16),
        "g_mlp": row(w["g_mlp"][l]), "w_mlp1": w["w_mlp1"][l].astype(BF16),
        "w_mlp2": w["w_mlp2"][l].astype(BF16),
    }


def kernel(x, c, ctx, c_ctx, w_mod, b_mod, g_mix, g_mlp, w_in, g_q, w_q_up, g_kv, w_kv_up,
           pool_w, pool_scale, hy_conv_w, hy_conv_b, hy_f_w1, hy_f_b1, hy_f_freq1, hy_f_w2,
           hy_f_b2, hy_f_freq2, hy_f_w3, hy_bias, g_out, w_out, w_mlp1, w_mlp2, g_final):
    w = dict(g_mix=g_mix, g_mlp=g_mlp, w_in=w_in, g_q=g_q, w_q_up=w_q_up, g_kv=g_kv,
             w_kv_up=w_kv_up, pool_w=pool_w, pool_scale=pool_scale, hy_conv_w=hy_conv_w,
             hy_conv_b=hy_conv_b, hy_f_w1=hy_f_w1, hy_f_b1=hy_f_b1, hy_f_freq1=hy_f_freq1,
             hy_f_w2=hy_f_w2, hy_f_b2=hy_f_b2, hy_f_freq2=hy_f_freq2, hy_f_w3=hy_f_w3,
             hy_bias=hy_bias, g_out=g_out, w_out=w_out, w_mlp1=w_mlp1, w_mlp2=w_mlp2)
    depth = w_mod.shape[0]
    bsz, n, _ = x.shape
    n_ctx = ctx.shape[1]
    tile = min(TOKEN_TILE, n)
    tile_c = min(TOKEN_TILE, n_ctx)

    rows = -(-(bsz + 1) // 8) * 8
    cond = jnp.zeros((rows, D_MODEL), F32).at[:bsz].set(c).at[bsz].set(c_ctx)
    mods_all = _modulation(cond, w_mod, b_mod)
    cos_l, sin_l = _rope_tables(n)
    cos_c, sin_c = _identity_tables(n_ctx)
    g_fin = g_final.reshape(1, D_MODEL)

    xc = ctx
    for l in range(depth):
        last = l == depth - 1
        lw = _layer_weights(l, w)
        mods = mods_all[l, :bsz].reshape(bsz, 1, 6 * D_MODEL)
        mods_c = jnp.broadcast_to(mods_all[l, bsz].reshape(1, 1, 6 * D_MODEL),
                                  (bsz, 1, 6 * D_MODEL))
        q, k, vt, pool_u, hy_u = _inproj(x, (mods, 0), (mods, 1), lw, cos_l, sin_l, tile)
        qc, kc, vtc, pool_uc, hy_uc = _inproj(xc, (mods_c, 0), (mods_c, 1), lw, cos_c, sin_c,
                                              tile_c)
        attn = _attention(q, [(kc, vtc), (k, vt)])
        pool = _pool(pool_u, lw["pool_w"], lw["pool_scale"])
        hy = _hyena(hy_u, lw)
        x = _merge(x, attn, pool, hy, mods, lw, g_fin, last, tile)
        if not last:
            attn_c = _attention(qc, [(kc, vtc)])
            pool_c = _pool(pool_uc, lw["pool_w"], lw["pool_scale"])
            hy_c = _hyena(hy_uc, lw)
            xc = _merge(xc, attn_c, pool_c, hy_c, mods_c, lw, g_fin, False, tile_c)
    return x
```

```python
import functools
import math

import jax
import jax.numpy as jnp
import numpy as np
from jax import lax
from jax.experimental import pallas as pl
from jax.experimental.pallas import tpu as pltpu

F32 = jnp.float32
BF16 = jnp.bfloat16

D_MODEL = 1024
GRID_W = 64
EPS = 1e-6
N_HEADS = 8
HEAD_V = 64
HEAD_NOPE = 64
HEAD_ROPE = 32
HEAD_PAD = 128
Q_RANK = 256
KV_RANK = 128
MLA_WIDTH = N_HEADS * HEAD_V
V_SLOT = 80
VT_ROWS = N_HEADS * V_SLOT
POOL_WIDTH = 256
HY_WIDTH = 256
MLA_SCALE = (HEAD_NOPE + HEAD_ROPE) ** -0.5
Q_SCALE = MLA_SCALE * math.log2(math.e)
ROPE_BASE = 10000.0
POOL_WINDOWS = (2, 4, 8, 16)
POOL_GROUP = POOL_WIDTH // len(POOL_WINDOWS)
POOL_PAD = 8
EDGE_ROWS = 16
HY_EMB = 33
HY_FFN = 64
HY_DECAY_TARGET = 1e-2
HY_DECAY_SHORT_PCT = 0.3
HY_DECAY_LONG_PCT = 1.5
D_FF = 4 * D_MODEL
COL_KV, COL_KR, COL_Q, COL_POOL, COL_HY, N_IN = 0, 128, 160, 416, 672, 1440
P_KV, P_Q, P_POOL, P_HY, P_KRA, P_KRB, N_IN_P = 0, 128, 384, 640, 1408, 1536, 1664

LANE = 128
VMEM_LIMIT = 56 * 1024 * 1024
TOKEN_TILE = 512
Q_TILE = 512
Q_TILES_PER_STEP = 2
CHUNK_UNROLL = 8
CONV_BLOCK = 512
FREQ_BLOCK = 128
FREQ_ROWS = 16
MLP_CHUNK = 1024
HIGHEST = lax.Precision.HIGHEST
HIGH = lax.Precision.HIGH


def _params(*sem):
    return pltpu.CompilerParams(dimension_semantics=sem, vmem_limit_bytes=VMEM_LIMIT)


def _const_spec(shape):
    zeros = (0,) * len(shape)
    return pl.BlockSpec(shape, lambda *_: zeros, pipeline_mode=pl.Buffered(1))


def _rms(x, g):
    return x * lax.rsqrt(jnp.mean(x * x, axis=-1, keepdims=True) + EPS) * g


def _bdot(a, b):
    return jnp.dot(a, b, preferred_element_type=F32)


def _split(x):
    hi = x.astype(BF16)
    return hi, (x - hi.astype(F32)).astype(BF16)


def _dot3(a, b):
    return _bdot(a[0], b[0]) + (_bdot(a[0], b[1]) + _bdot(a[1], b[0]))


def _modulation_kernel(c_ref, w_ref, b_ref, o_ref):
    c = c_ref[...]
    s = c / (1.0 + jnp.exp(-c))
    o_ref[0] = jnp.dot(s, w_ref[0], preferred_element_type=F32, precision=HIGHEST) + b_ref[0]


def _modulation(cond, w_mod, b_mod):
    depth = w_mod.shape[0]
    rows = cond.shape[0]
    return pl.pallas_call(
        _modulation_kernel,
        out_shape=jax.ShapeDtypeStruct((depth, rows, 6 * D_MODEL), F32),
        grid=(depth, 6),
        in_specs=[
            pl.BlockSpec((rows, D_MODEL), lambda l, j: (0, 0)),
            pl.BlockSpec((1, D_MODEL, D_MODEL), lambda l, j: (l, 0, j)),
            pl.BlockSpec((1, 1, D_MODEL), lambda l, j: (l, 0, j)),
        ],
        out_specs=pl.BlockSpec((1, rows, D_MODEL), lambda l, j: (l, 0, j)),
        compiler_params=_params("arbitrary", "arbitrary"),
        name="modulation",
    )(cond, w_mod, b_mod.reshape(depth, 1, 6 * D_MODEL))


def _inproj_kernel(x_ref, sh_ref, sc_ref, gmix_ref, win_ref, gkv_ref, wka_ref, wvt_ref,
                   gq_ref, wab_ref, cos_ref, sin_ref,
                   q_ref, k_ref, vt_ref, pool_ref, hy_ref):
    x = x_ref[0]
    h = (_rms(x, gmix_ref[...]) * (1.0 + sc_ref[0]) + sh_ref[0]).astype(BF16)
    proj = _bdot(h, win_ref[...])
    pool_ref[0] = proj[:, P_POOL:P_HY]
    hy_ref[0] = proj[:, P_HY:P_KRA].astype(hy_ref.dtype)
    kvn = _rms(proj[:, P_KV:P_Q], gkv_ref[...]).astype(BF16)
    kpad = _bdot(kvn, wka_ref[...])
    vt = lax.dot_general(wvt_ref[...], kvn, (((1,), (1,)), ((), ())),
                         preferred_element_type=F32)
    row = lax.broadcasted_iota(jnp.int32, vt.shape, 0)
    vt = jnp.where(row % V_SLOT >= HEAD_V, 1.0, vt).astype(vt_ref.dtype)
    vt_ref[0, 0] = vt.reshape(N_HEADS, V_SLOT, vt.shape[1])
    qn = _rms(proj[:, P_Q:P_POOL], gq_ref[...]).astype(BF16)
    ab = _bdot(qn, wab_ref[...])
    cos = cos_ref[...]
    sin = sin_ref[...]
    kr = proj[:, P_KRA:P_KRB] * cos + proj[:, P_KRB:N_IN_P] * sin
    width = N_HEADS * HEAD_PAD
    for hd in range(N_HEADS):
        sl = slice(hd * HEAD_PAD, (hd + 1) * HEAD_PAD)
        sl_b = slice(width + hd * HEAD_PAD, width + (hd + 1) * HEAD_PAD)
        k_ref[0, hd] = (kpad[:, sl] + kr).astype(k_ref.dtype)
        q_ref[0, hd] = ((ab[:, sl] * cos + ab[:, sl_b] * sin) * Q_SCALE).astype(q_ref.dtype)


def _inproj(x, shift, scale, lw, cos_t, sin_t, tile):
    bsz, n, _ = x.shape
    nt = n // tile
    width = N_HEADS * HEAD_PAD
    mod_spec = lambda k: pl.BlockSpec((1, 1, D_MODEL), lambda b, i, k=k: (b, 0, k))
    return pl.pallas_call(
        _inproj_kernel,
        out_shape=(
            jax.ShapeDtypeStruct((bsz, N_HEADS, n, HEAD_PAD), BF16),
            jax.ShapeDtypeStruct((bsz, N_HEADS, n, HEAD_PAD), BF16),
            jax.ShapeDtypeStruct((bsz, nt, N_HEADS, V_SLOT, tile), BF16),
            jax.ShapeDtypeStruct((bsz, n, POOL_WIDTH), F32),
            jax.ShapeDtypeStruct((bsz, n, 3 * HY_WIDTH), BF16),
        ),
        grid=(bsz, nt),
        in_specs=[
            pl.BlockSpec((1, tile, D_MODEL), lambda b, i: (b, i, 0)),
            mod_spec(shift[1]), mod_spec(scale[1]),
            _const_spec((1, D_MODEL)),
            _const_spec((D_MODEL, N_IN_P)),
            _const_spec((1, KV_RANK)),
            _const_spec((KV_RANK, width)),
            _const_spec((VT_ROWS, KV_RANK)),
            _const_spec((1, Q_RANK)),
            _const_spec((Q_RANK, 2 * width)),
            pl.BlockSpec((tile, HEAD_PAD), lambda b, i: (i, 0)),
            pl.BlockSpec((tile, HEAD_PAD), lambda b, i: (i, 0)),
        ],
        out_specs=(
            pl.BlockSpec((1, N_HEADS, tile, HEAD_PAD), lambda b, i: (b, 0, i, 0)),
            pl.BlockSpec((1, N_HEADS, tile, HEAD_PAD), lambda b, i: (b, 0, i, 0)),
            pl.BlockSpec((1, 1, N_HEADS, V_SLOT, tile), lambda b, i: (b, i, 0, 0, 0)),
            pl.BlockSpec((1, tile, POOL_WIDTH), lambda b, i: (b, i, 0)),
            pl.BlockSpec((1, tile, 3 * HY_WIDTH), lambda b, i: (b, i, 0)),
        ),
        compiler_params=_params("parallel", "parallel"),
        name="inproj",
    )(x, shift[0], scale[0], lw["g_mix"], lw["w_in"], lw["g_kv"], lw["wka"], lw["wvt"],
      lw["g_q"], lw["wab"], cos_t, sin_t)


def _attention_kernel(*refs, seg_chunks):
    q_ref = refs[0]
    n_seg = len(seg_chunks)
    seg_refs = refs[1:1 + 2 * n_seg]
    o_ref = refs[1 + 2 * n_seg]
    out_ref, m_ref = refs[2 + 2 * n_seg], refs[3 + 2 * n_seg]
    s_bufs = refs[4 + 2 * n_seg:]
    tq = out_ref.shape[2]
    n_tiles = q_ref.shape[2] // tq
    items = [(t, hd) for t in range(n_tiles) for hd in range(N_HEADS)]
    step_id = pl.program_id(0) + pl.program_id(1) + len(items)

    def stage(item1, item2, parity1):
        buf1, buf2 = s_bufs[parity1], s_bufs[1 - parity1]
        m2 = m_ref[...] if item2 is not None else None
        m1 = jnp.full((1, tq), -jnp.inf, F32)
        acc2 = jnp.zeros((V_SLOT, tq), F32)
        base = 0
        for si, (n_chunks, chunk) in enumerate(seg_chunks):
            k_ref, vt_ref = seg_refs[2 * si], seg_refs[2 * si + 1]
            for c in range(n_chunks):
                rows = slice(base + c * chunk, base + (c + 1) * chunk)
                if item1 is not None:
                    t1, h1 = item1
                    s = lax.dot_general(k_ref[0, h1, c * chunk:(c + 1) * chunk, :],
                                        q_ref[0, h1, t1 * tq:(t1 + 1) * tq, :],
                                        (((1,), (1,)), ((), ())), preferred_element_type=F32)
                    buf1[rows, :] = s
                    m1 = jnp.maximum(m1, jnp.max(s, axis=0, keepdims=True))
                if item2 is not None:
                    p = jnp.exp2(buf2[rows, :] - m2).astype(BF16)
                    acc2 = acc2 + _bdot(vt_ref[0, c, item2[1]], p)
            base += n_chunks * chunk
        if item2 is not None:
            t2, h2 = item2
            out_ref[h2] = acc2[:HEAD_V, :] / acc2[HEAD_V:HEAD_V + 1, :]
            if h2 == N_HEADS - 1:
                o_ref[0, t2 * tq:(t2 + 1) * tq, :] = (
                    out_ref[...].reshape(MLA_WIDTH, tq).T.astype(o_ref.dtype))
        if item1 is not None:
            m_ref[...] = m1

    for s in range(-1, len(items)):
        item1 = items[s + 1] if s + 1 < len(items) else None
        item2 = items[s] if s >= 0 else None
        pl.when(step_id >= s)(functools.partial(stage, item1, item2, (s + 1) % 2))


def _attention(q, segments):
    bsz, _, n, _ = q.shape
    tq = min(Q_TILE, n)
    step_q = min(Q_TILES_PER_STEP * tq, n)
    in_specs = [pl.BlockSpec((1, N_HEADS, step_q, HEAD_PAD), lambda b, i: (b, 0, i, 0))]
    args = [q]
    seg_chunks = []
    for k, vt in segments:
        nk = k.shape[2]
        n_chunks, chunk = vt.shape[1], vt.shape[4]
        assert n_chunks * chunk == nk
        seg_chunks.append((n_chunks, chunk))
        in_specs.append(pl.BlockSpec((1, N_HEADS, nk, HEAD_PAD), lambda b, i: (b, 0, 0, 0)))
        in_specs.append(pl.BlockSpec((1, n_chunks, N_HEADS, V_SLOT, chunk),
                                     lambda b, i: (b, 0, 0, 0, 0)))
        args += [k, vt]
    n_keys = sum(nc * ch for nc, ch in seg_chunks)
    return pl.pallas_call(
        functools.partial(_attention_kernel, seg_chunks=tuple(seg_chunks)),
        out_shape=jax.ShapeDtypeStruct((bsz, n, MLA_WIDTH), BF16),
        grid=(bsz, n // step_q),
        in_specs=in_specs,
        out_specs=pl.BlockSpec((1, step_q, MLA_WIDTH), lambda b, i: (b, i, 0)),
        scratch_shapes=[pltpu.VMEM((N_HEADS, HEAD_V, tq), F32), pltpu.VMEM((1, tq), F32),
                        pltpu.VMEM((n_keys, tq), F32), pltpu.VMEM((n_keys, tq), F32)],
        compiler_params=_params("parallel", "arbitrary"),
        name="attention",
    )(*args)


def _shift_rows(a, k):
    n = a.shape[0]
    return pltpu.roll(a, k % n, 0)


def _pool_kernel(u_ref, w_ref, scale_ref, o_ref, ext_ref):
    n = u_ref.shape[1]
    u = u_ref[0]
    zeros = jnp.zeros((POOL_PAD, POOL_WIDTH), F32)
    ext_ref[0:POOL_PAD, :] = zeros
    ext_ref[POOL_PAD + n:, :] = zeros
    ext_ref[POOL_PAD:POOL_PAD + n, :] = u
    ext = ext_ref[...]
    s2 = ext + _shift_rows(ext, 1)
    s4 = _shift_rows(s2, 1) + _shift_rows(s2, -1)
    s8 = _shift_rows(s4, 2) + _shift_rows(s4, -2)
    s16 = _shift_rows(s8, 4) + _shift_rows(s8, -4)
    t = lax.broadcasted_iota(jnp.int32, (n, POOL_WIDTH), 0)
    lane = lax.broadcasted_iota(jnp.int32, (n, POOL_WIDTH), 1)
    mean = None
    for g, (w, s) in enumerate(zip(POOL_WINDOWS, (s2, s4, s8, s16))):
        lo = jnp.clip(t - w // 2, 0, n)
        hi = jnp.clip(t + w // 2, 0, n)
        mg = s[POOL_PAD:POOL_PAD + n, :] / (hi - lo).astype(F32)
        mean = mg if mean is None else jnp.where(lane >= g * POOL_GROUP, mg, mean)
    diff = (mean - u).astype(BF16)
    o_ref[0] = _bdot(diff, w_ref[...]) * scale_ref[...]


def _pool(u, w_bd, scale):
    bsz, n, _ = u.shape
    return pl.pallas_call(
        _pool_kernel,
        out_shape=jax.ShapeDtypeStruct((bsz, n, POOL_WIDTH), F32),
        grid=(bsz,),
        in_specs=[
            pl.BlockSpec((1, n, POOL_WIDTH), lambda b: (b, 0, 0)),
            _const_spec((POOL_WIDTH, POOL_WIDTH)),
            _const_spec((1, POOL_WIDTH)),
        ],
        out_specs=pl.BlockSpec((1, n, POOL_WIDTH), lambda b: (b, 0, 0)),
        scratch_shapes=[pltpu.VMEM((n + 2 * POOL_PAD, POOL_WIDTH), F32)],
        compiler_params=_params("parallel"),
        name="pool_mixer",
    )(u, w_bd, scale)


def _filter_kernel(z_ref, w1_ref, b1_ref, f1_ref, w2_ref, b2_ref, f2_ref, w3h_ref, w3l_ref,
                   dl_ref, g_ref, asum_ref, *, n):
    i = pl.program_id(0)
    rows_per = z_ref.shape[0]
    z = z_ref[...]
    h = jnp.sin(f1_ref[...] * (jnp.dot(z, w1_ref[...], preferred_element_type=F32,
                                       precision=HIGHEST) + b1_ref[...]))
    h = jnp.sin(f2_ref[...] * (jnp.dot(h, w2_ref[...], preferred_element_type=F32,
                                       precision=HIGHEST) + b2_ref[...]))
    h = _dot3(_split(h), (w3h_ref[...], w3l_ref[...]))
    r = i * rows_per + lax.broadcasted_iota(jnp.int32, (rows_per, HY_WIDTH), 0)
    backward = r < n
    pos = jnp.where(backward, n - r, r - n)
    t = pos.astype(F32) * (1.0 / (n - 1))
    decay = jnp.exp(-t * dl_ref[...])
    valid = r > 0

    @pl.when(i == 0)
    def _():
        asum_ref[...] = jnp.zeros_like(asum_ref)

    for o in range(2):
        fwd = h[:, (2 * o) * HY_WIDTH:(2 * o + 1) * HY_WIDTH]
        bwd = h[:, (2 * o + 1) * HY_WIDTH:(2 * o + 2) * HY_WIDTH]
        g = jnp.where(valid, jnp.where(backward, bwd, fwd) * decay, 0.0)
        g_ref[o] = g
        asum_ref[o] += jnp.sum(jnp.abs(g), axis=0, keepdims=True)


def _hyena_filter(n, lw):
    rows = 2 * n
    tile = min(rows, 1024)
    r = np.arange(rows)
    pos = np.where(r < n, n - r, r - n).astype(np.float64)
    t = pos / (n - 1)
    bands = (HY_EMB - 1) // 2
    freqs = np.linspace(1e-4, bands - 1, bands)[None, :]
    wpos = 2.0 * math.pi * pos[:, None] / n
    z = np.zeros((rows, LANE), np.float32)
    z[:, 0] = t
    z[:, 1:1 + bands] = np.cos(freqs * wpos)
    z[:, 1 + bands:HY_EMB] = -np.sin(freqs * wpos)
    deltas = np.abs(np.linspace(math.log(HY_DECAY_TARGET) / HY_DECAY_LONG_PCT,
                                math.log(HY_DECAY_TARGET) / HY_DECAY_SHORT_PCT, HY_WIDTH))
    deltas = jnp.asarray(deltas[None, :], F32)
    return pl.pallas_call(
        functools.partial(_filter_kernel, n=n),
        out_shape=(jax.ShapeDtypeStruct((2, rows, HY_WIDTH), F32),
                   jax.ShapeDtypeStruct((2, 1, HY_WIDTH), F32)),
        grid=(rows // tile,),
        in_specs=[
            pl.BlockSpec((tile, LANE), lambda i: (i, 0)),
            _const_spec((LANE, LANE)), _const_spec((1, LANE)), _const_spec((1, LANE)),
            _const_spec((LANE, LANE)), _const_spec((1, LANE)), _const_spec((1, LANE)),
            _const_spec((LANE, 4 * HY_WIDTH)), _const_spec((LANE, 4 * HY_WIDTH)),
            _const_spec((1, HY_WIDTH)),
        ],
        out_specs=(pl.BlockSpec((2, tile, HY_WIDTH), lambda i: (0, i, 0)),
                   pl.BlockSpec((2, 1, HY_WIDTH), lambda i: (0, 0, 0))),
        compiler_params=_params("arbitrary"),
        name="hyena_filter",
    )(jnp.asarray(z), lw["hy_w1"], lw["hy_b1"], lw["hy_f1"], lw["hy_w2"], lw["hy_b2"],
      lw["hy_f2"], *_split(lw["hy_w3"]), deltas)


def _dft_tables(blk):
    f = np.arange(blk, dtype=np.float64)[:, None]
    m = np.arange(blk, dtype=np.float64)[None, :]
    theta = math.pi * (2.0 * f + 1.0) * m / (2.0 * blk)
    return np.cos(theta), np.sin(theta)


def _spectra_kernel(g_ref, asum_ref, fwdh_ref, fwdl_ref, k_ref, prev_ref):
    e = pl.program_id(1)
    blk = g_ref.shape[1]
    s = _dot3((fwdh_ref[...], fwdl_ref[...]), _split(g_ref[0]))
    s = s / asum_ref[0]

    @pl.when(e > 0)
    def _():
        prev = prev_ref[...]
        f = lax.broadcasted_iota(jnp.int32, (blk, HY_WIDTH), 0)
        sgn = jnp.where(f % 2 == 0, 1.0, -1.0).astype(F32)
        k_ref[0, 0, 0] = s[:blk] - sgn * prev[blk:]
        k_ref[0, 0, 1] = s[blk:] + sgn * prev[:blk]

    prev_ref[...] = s


def _hyena_spectra(g, asum, blk):
    rows = g.shape[1]
    nseg = rows // blk
    cos, sin = _dft_tables(blk)
    fwd = jnp.asarray(np.concatenate([cos, -sin], axis=0), F32)
    return pl.pallas_call(
        _spectra_kernel,
        out_shape=jax.ShapeDtypeStruct((2, nseg - 1, 2, blk, HY_WIDTH), F32),
        grid=(2, nseg),
        in_specs=[
            pl.BlockSpec((1, blk, HY_WIDTH), lambda o, e: (o, e, 0)),
            pl.BlockSpec((1, 1, HY_WIDTH), lambda o, e: (o, 0, 0)),
            _const_spec((2 * blk, blk)), _const_spec((2 * blk, blk)),
        ],
        out_specs=pl.BlockSpec((1, 1, 2, blk, HY_WIDTH),
                               lambda o, e: (o, jnp.maximum(e - 1, 0), 0, 0, 0)),
        scratch_shapes=[pltpu.VMEM((2 * blk, HY_WIDTH), F32)],
        compiler_params=_params("arbitrary", "arbitrary"),
        name="hyena_spectra",
    )(g, asum, *_split(fwd))


def _short_conv_rows(src_ref, rows, w_ref, b_ref):
    r0, r1 = rows
    n = src_ref.shape[1]
    cur = src_ref[0, r0:r1, :].astype(F32)
    row = lax.broadcasted_iota(jnp.int32, cur.shape, 0)
    edge = jnp.zeros((1, cur.shape[1]), F32)
    before = src_ref[0, r0 - EDGE_ROWS:r0, :].astype(F32)[EDGE_ROWS - 1:, :] if r0 > 0 else edge
    after = src_ref[0, r1:r1 + EDGE_ROWS, :].astype(F32)[:1, :] if r1 < n else edge
    prev = jnp.where(row == 0, before, _shift_rows(cur, 1))
    nxt = jnp.where(row == r1 - r0 - 1, after, _shift_rows(cur, -1))
    return prev * w_ref[0:1, :] + cur * w_ref[1:2, :] + nxt * w_ref[2:3, :] + b_ref[...]


def _conv_kernel(u_ref, gate_ref, uw_ref, ub_ref, gw_ref, gb_ref, bias_ref, k_ref, fwd_ref,
                 inv_ref, o_ref, u32_ref, u16_ref, acc_ref, uf0_ref, uf1_ref, yf0_ref, yf1_ref,
                 *, conv_input):
    n = u_ref.shape[1]
    nf, blk = fwd_ref.shape[0], fwd_ref.shape[2]
    nb = n // blk
    fb = fwd_ref.shape[1] // 2
    uf_refs, yf_refs = (uf0_ref, uf1_ref), (yf0_ref, yf1_ref)
    step_id = pl.program_id(0) + nf + 2

    def forward(f):
        for j in range(nb):
            uf_refs[f % 2][j] = _bdot(fwd_ref[f], u16_ref[j * blk:(j + 1) * blk, :])

    def multiply(f):
        uf_ref, yf_ref = uf_refs[f % 2], yf_refs[f % 2]
        for c in range(fb // FREQ_ROWS):
            rs = slice(c * FREQ_ROWS, (c + 1) * FREQ_ROWS)
            rs_im = slice(fb + c * FREQ_ROWS, fb + (c + 1) * FREQ_ROWS)
            ks = slice(f * fb + c * FREQ_ROWS, f * fb + (c + 1) * FREQ_ROWS)
            for i in range(nb):
                yr = jnp.zeros((FREQ_ROWS, HY_WIDTH), F32)
                yi = jnp.zeros((FREQ_ROWS, HY_WIDTH), F32)
                for j in range(nb):
                    d = i - j + nb - 1
                    kr = k_ref[0, d, 0, ks, :]
                    ki = k_ref[0, d, 1, ks, :]
                    ur = uf_ref[j, rs, :]
                    ui = uf_ref[j, rs_im, :]
                    yr = yr + (kr * ur - ki * ui)
                    yi = yi + (kr * ui + ki * ur)
                yf_ref[i, rs, :] = yr.astype(BF16)
                yf_ref[i, rs_im, :] = yi.astype(BF16)

    def inverse(f):
        for i in range(nb):
            y = _bdot(inv_ref[f], yf_refs[f % 2][i])
            rows = slice(i * blk, (i + 1) * blk)
            acc_ref[rows, :] = y if f == 0 else acc_ref[rows, :] + y

    def prologue():
        for j in range(nb):
            rows = (j * blk, (j + 1) * blk)
            if conv_input:
                u = _short_conv_rows(u_ref, rows, uw_ref, ub_ref)
            else:
                u = u_ref[0, rows[0]:rows[1], :].astype(F32)
            u32_ref[rows[0]:rows[1], :] = u
            u16_ref[rows[0]:rows[1], :] = u.astype(BF16)
        forward(0)

    def stage(f):
        if f + 1 < nf:
            forward(f + 1)
        if f < nf:
            multiply(f)
        if f >= 1:
            inverse(f - 1)

    def epilogue():
        bias = bias_ref[0]
        for j in range(nb):
            rows = (j * blk, (j + 1) * blk)
            rs = slice(rows[0], rows[1])
            gate = _short_conv_rows(gate_ref, rows, gw_ref, gb_ref)
            conv = acc_ref[rs, :] + u32_ref[rs, :] * bias
            o_ref[0, rs, :] = (gate * conv).astype(o_ref.dtype)

    pl.when(step_id >= 0)(prologue)
    for f in range(nf + 1):
        pl.when(step_id >= f + 1)(functools.partial(stage, f))
    pl.when(step_id >= nf + 2)(epilogue)


def _hyena_conv(u_arr, u_col, gate_arr, gate_col, lw, khat, order, blk, conv_input, out_dtype):
    bsz, n, _ = gate_arr.shape
    nb = n // blk
    fb = min(FREQ_BLOCK, blk)
    nf = blk // fb
    cos, sin = _dft_tables(blk)
    fwd = np.concatenate([cos.reshape(nf, fb, blk), -sin.reshape(nf, fb, blk)], axis=1)
    inv = np.concatenate([cos.T.reshape(blk, nf, fb), -sin.T.reshape(blk, nf, fb)], axis=2)
    inv = np.transpose(inv, (1, 0, 2)) / blk
    cw, cb = lw["hy_conv_w"], lw["hy_conv_b"]
    col = lambda c: (lambda b, c=c: (0, c))
    once = pl.Buffered(1)
    return pl.pallas_call(
        functools.partial(_conv_kernel, conv_input=conv_input),
        out_shape=jax.ShapeDtypeStruct((bsz, n, HY_WIDTH), out_dtype),
        grid=(bsz,),
        in_specs=[
            pl.BlockSpec((1, n, HY_WIDTH), lambda b, c=u_col: (b, 0, c)),
            pl.BlockSpec((1, n, HY_WIDTH), lambda b, c=gate_col: (b, 0, c)),
            pl.BlockSpec((3, HY_WIDTH), col(u_col if conv_input else 0)),
            pl.BlockSpec((1, HY_WIDTH), col(u_col if conv_input else 0)),
            pl.BlockSpec((3, HY_WIDTH), col(gate_col)),
            pl.BlockSpec((1, HY_WIDTH), col(gate_col)),
            pl.BlockSpec((1, 1, HY_WIDTH), lambda b, o=order: (o, 0, 0)),
            pl.BlockSpec((1, 2 * nb - 1, 2, blk, HY_WIDTH), lambda b, o=order: (o, 0, 0, 0, 0),
                         pipeline_mode=once),
            pl.BlockSpec((nf, 2 * fb, blk), lambda b: (0, 0, 0), pipeline_mode=once),
            pl.BlockSpec((nf, blk, 2 * fb), lambda b: (0, 0, 0), pipeline_mode=once),
        ],
        out_specs=pl.BlockSpec((1, n, HY_WIDTH), lambda b: (b, 0, 0)),
        scratch_shapes=[
            pltpu.VMEM((n, HY_WIDTH), F32),
            pltpu.VMEM((n, HY_WIDTH), BF16),
            pltpu.VMEM((n, HY_WIDTH), F32),
            pltpu.VMEM((nb, 2 * fb, HY_WIDTH), F32),
            pltpu.VMEM((nb, 2 * fb, HY_WIDTH), F32),
            pltpu.VMEM((nb, 2 * fb, HY_WIDTH), BF16),
            pltpu.VMEM((nb, 2 * fb, HY_WIDTH), BF16),
        ],
        compiler_params=_params("parallel"),
        name="hyena_conv%d" % order,
    )(u_arr, gate_arr, cw, cb, cw, cb, lw["hy_bias"], khat,
      jnp.asarray(fwd, BF16), jnp.asarray(inv, BF16))


def _hyena(hy_u, lw):
    n = hy_u.shape[1]
    blk = min(CONV_BLOCK, n)
    g, asum = _hyena_filter(n, lw)
    khat = _hyena_spectra(g, asum, blk)
    z = _hyena_conv(hy_u, 0, hy_u, 1, lw, khat, 0, blk, True, BF16)
    return _hyena_conv(z, 0, hy_u, 2, lw, khat, 1, blk, False, F32)


def _merge_kernel(x_ref, attn_ref, pool_ref, hy_ref, g1_ref, sh2_ref, sc2_ref, g2_ref,
                  gout_ref, wout_ref, gmlp_ref, w1_ref, w2_ref, gfin_ref, o_ref, *, final_norm):
    gout = gout_ref[...]
    a = _rms(attn_ref[0].astype(F32), gout[:, :MLA_WIDTH]).astype(BF16)
    p = _rms(pool_ref[0], gout[:, MLA_WIDTH:MLA_WIDTH + POOL_WIDTH]).astype(BF16)
    hh = _rms(hy_ref[0], gout[:, MLA_WIDTH + POOL_WIDTH:]).astype(BF16)
    y = (_bdot(a, wout_ref[0:MLA_WIDTH, :])
         + _bdot(p, wout_ref[MLA_WIDTH:MLA_WIDTH + POOL_WIDTH, :])
         + _bdot(hh, wout_ref[MLA_WIDTH + POOL_WIDTH:, :]))
    x1 = x_ref[0] + g1_ref[0] * y
    h2 = (_rms(x1, gmlp_ref[...]) * (1.0 + sc2_ref[0]) + sh2_ref[0]).astype(BF16)
    y2 = jnp.zeros_like(x1)
    for c in range(D_FF // MLP_CHUNK):
        cs = slice(c * MLP_CHUNK, (c + 1) * MLP_CHUNK)
        hid = jnp.maximum(_bdot(h2, w1_ref[:, cs]), 0.0)
        y2 = y2 + _bdot((hid * hid).astype(BF16), w2_ref[cs, :])
    x2 = x1 + g2_ref[0] * y2
    if final_norm:
        x2 = _rms(x2, gfin_ref[...])
    o_ref[0] = x2


def _merge(x, attn, pool, hy, mods, lw, g_final, final_norm, tile):
    bsz, n, _ = x.shape
    nt = n // tile
    tok = lambda w: pl.BlockSpec((1, tile, w), lambda b, i: (b, i, 0))
    mod_spec = lambda k: pl.BlockSpec((1, 1, D_MODEL), lambda b, i, k=k: (b, 0, k))
    return pl.pallas_call(
        functools.partial(_merge_kernel, final_norm=final_norm),
        out_shape=jax.ShapeDtypeStruct((bsz, n, D_MODEL), F32),
        grid=(bsz, nt),
        in_specs=[
            tok(D_MODEL), tok(MLA_WIDTH), tok(POOL_WIDTH), tok(HY_WIDTH),
            mod_spec(2), mod_spec(3), mod_spec(4), mod_spec(5),
            _const_spec((1, D_MODEL)), _const_spec((D_MODEL, D_MODEL)),
            _const_spec((1, D_MODEL)), _const_spec((D_MODEL, D_FF)),
            _const_spec((D_FF, D_MODEL)), _const_spec((1, D_MODEL)),
        ],
        out_specs=tok(D_MODEL),
        compiler_params=_params("parallel", "parallel"),
        name="merge_mlp",
    )(x, attn, pool, hy, mods, mods, mods, mods, lw["g_out"], lw["w_out"], lw["g_mlp"],
      lw["w_mlp1"], lw["w_mlp2"], g_final)


_ROPE_SWAP = np.concatenate([np.arange(8, 16), np.arange(0, 8), np.arange(24, 32), np.arange(16, 24)])


def _rope_tables(n):
    idx = np.arange(n)
    r = (idx // GRID_W).astype(np.float32)
    c = (idx % GRID_W).astype(np.float32)
    n_freq = HEAD_ROPE // 4
    inv = (ROPE_BASE ** (-np.arange(n_freq, dtype=np.float32) / n_freq)).astype(np.float32)
    ar, ac = r[:, None] * inv, c[:, None] * inv
    cos = np.zeros((n, HEAD_PAD), np.float32)
    sin = np.zeros((n, HEAD_PAD), np.float32)
    cos[:, :HEAD_NOPE] = 1.0
    cos[:, HEAD_NOPE:HEAD_NOPE + HEAD_ROPE] = np.concatenate(
        [np.cos(ar), np.cos(ar), np.cos(ac), np.cos(ac)], axis=1)
    sin[:, HEAD_NOPE:HEAD_NOPE + HEAD_ROPE] = np.concatenate(
        [-np.sin(ar), np.sin(ar), -np.sin(ac), np.sin(ac)], axis=1)
    return jnp.asarray(cos), jnp.asarray(sin)


def _identity_tables(n):
    cos = np.zeros((n, HEAD_PAD), np.float32)
    cos[:, :HEAD_NOPE + HEAD_ROPE] = 1.0
    return jnp.asarray(cos), jnp.zeros((n, HEAD_PAD), F32)


def _layer_weights(l, w):
    w_in = w["w_in"][l]
    kr = w_in[:, COL_KR:COL_Q]
    slot = lambda m: jnp.zeros((D_MODEL, HEAD_PAD), F32).at[:, HEAD_NOPE:HEAD_NOPE + HEAD_ROPE].set(m)
    w_in_p = jnp.concatenate([w_in[:, COL_KV:COL_KR], w_in[:, COL_Q:COL_POOL],
                              w_in[:, COL_POOL:COL_HY], w_in[:, COL_HY:],
                              slot(kr), slot(kr[:, _ROPE_SWAP])], axis=1).astype(BF16)
    wkv = w["w_kv_up"][l].reshape(KV_RANK, N_HEADS, HEAD_NOPE + HEAD_V)
    wka = wkv.at[:, :, HEAD_NOPE:].set(0.0).reshape(KV_RANK, N_HEADS * HEAD_PAD).astype(BF16)
    wvt = jnp.zeros((KV_RANK, N_HEADS, V_SLOT), F32).at[:, :, :HEAD_V].set(wkv[:, :, HEAD_NOPE:])
    wvt = wvt.reshape(KV_RANK, VT_ROWS).T.astype(BF16)
    wq = w["w_q_up"][l].reshape(Q_RANK, N_HEADS, HEAD_NOPE + HEAD_ROPE)
    wa = jnp.zeros((Q_RANK, N_HEADS, HEAD_PAD), F32).at[:, :, :HEAD_NOPE + HEAD_ROPE].set(wq)
    wb = jnp.zeros((Q_RANK, N_HEADS, HEAD_PAD), F32).at[:, :, HEAD_NOPE:HEAD_NOPE + HEAD_ROPE].set(
        wq[:, :, HEAD_NOPE:][:, :, _ROPE_SWAP])
    wab = jnp.concatenate([wa.reshape(Q_RANK, -1), wb.reshape(Q_RANK, -1)], axis=1).astype(BF16)
    pool_bd = jnp.zeros((POOL_WIDTH, POOL_WIDTH), F32)
    for g in range(len(POOL_WINDOWS)):
        sl = slice(g * POOL_GROUP, (g + 1) * POOL_GROUP)
        pool_bd = pool_bd.at[sl, sl].set(w["pool_w"][l, g])
    pad_rows = lambda m, rows: jnp.zeros((rows, m.shape[1]), F32).at[:m.shape[0]].set(m)
    pad_cols = lambda m, cols: jnp.zeros((m.shape[0], cols), F32).at[:, :m.shape[1]].set(m)
    row = lambda v: v.reshape(1, -1)
    return {
        "g_mix": row(w["g_mix"][l]), "w_in": w_in_p, "g_kv": row(w["g_kv"][l]), "wka": wka,
        "wvt": wvt, "g_q": row(w["g_q"][l]), "wab": wab,
        "pool_w": pool_bd.astype(BF16), "pool_scale": row(w["pool_scale"][l]),
        "hy_conv_w": w["hy_conv_w"][l], "hy_conv_b": row(w["hy_conv_b"][l]),
        "hy_w1": pad_cols(pad_rows(w["hy_f_w1"][l], LANE), LANE),
        "hy_b1": pad_cols(row(w["hy_f_b1"][l]), LANE),
        "hy_f1": pad_cols(row(w["hy_f_freq1"][l]), LANE),
        "hy_w2": pad_cols(pad_rows(w["hy_f_w2"][l], LANE), LANE),
        "hy_b2": pad_cols(row(w["hy_f_b2"][l]), LANE),
        "hy_f2": pad_cols(row(w["hy_f_freq2"][l]), LANE),
        "hy_w3": pad_rows(w["hy_f_w3"][l], LANE),
        "hy_bias": w["hy_bias"][l].reshape(2, 1, HY_WIDTH),
        "g_out": row(w["g_out"][l]), "w_out": w["w_out"][l].astype(BF16),
        "g_mlp": row(w["g_mlp"][l]), "w_mlp1": w["w_mlp1"][l].astype(BF16),
        "w_mlp2": w["w_mlp2"][l].astype(BF16),
    }


def kernel(x, c, ctx, c_ctx, w_mod, b_mod, g_mix, g_mlp, w_in, g_q, w_q_up, g_kv, w_kv_up,
           pool_w, pool_scale, hy_conv_w, hy_conv_b, hy_f_w1, hy_f_b1, hy_f_freq1, hy_f_w2,
           hy_f_b2, hy_f_freq2, hy_f_w3, hy_bias, g_out, w_out, w_mlp1, w_mlp2, g_final):
    w = dict(g_mix=g_mix, g_mlp=g_mlp, w_in=w_in, g_q=g_q, w_q_up=w_q_up, g_kv=g_kv,
             w_kv_up=w_kv_up, pool_w=pool_w, pool_scale=pool_scale, hy_conv_w=hy_conv_w,
             hy_conv_b=hy_conv_b, hy_f_w1=hy_f_w1, hy_f_b1=hy_f_b1, hy_f_freq1=hy_f_freq1,
             hy_f_w2=hy_f_w2, hy_f_b2=hy_f_b2, hy_f_freq2=hy_f_freq2, hy_f_w3=hy_f_w3,
             hy_bias=hy_bias, g_out=g_out, w_out=w_out, w_mlp1=w_mlp1, w_mlp2=w_mlp2)
    depth = w_mod.shape[0]
    bsz, n, _ = x.shape
    n_ctx = ctx.shape[1]
    tile = min(TOKEN_TILE, n)
    tile_c = min(TOKEN_TILE, n_ctx)

    rows = -(-(bsz + 1) // 8) * 8
    cond = jnp.zeros((rows, D_MODEL), F32).at[:bsz].set(c).at[bsz].set(c_ctx)
    mods_all = _modulation(cond, w_mod, b_mod)
    cos_l, sin_l = _rope_tables(n)
    cos_c, sin_c = _identity_tables(n_ctx)
    g_fin = g_final.reshape(1, D_MODEL)

    xc = ctx
    for l in range(depth):
        last = l == depth - 1
        lw = _layer_weights(l, w)
        mods = mods_all[l, :bsz].reshape(bsz, 1, 6 * D_MODEL)
        mods_c = jnp.broadcast_to(mods_all[l, bsz].reshape(1, 1, 6 * D_MODEL),
                                  (bsz, 1, 6 * D_MODEL))
        q, k, vt, pool_u, hy_u = _inproj(x, (mods, 0), (mods, 1), lw, cos_l, sin_l, tile)
        qc, kc, vtc, pool_uc, hy_uc = _inproj(xc, (mods_c, 0), (mods_c, 1), lw, cos_c, sin_c,
                                              tile_c)
        attn = _attention(q, [(kc, vtc), (k, vt)])
        pool = _pool(pool_u, lw["pool_w"], lw["pool_scale"])
        hy = _hyena(hy_u, lw)
        x = _merge(x, attn, pool, hy, mods, lw, g_fin, last, tile)
        if not last:
            attn_c = _attention(qc, [(kc, vtc)])
            pool_c = _pool(pool_uc, lw["pool_w"], lw["pool_scale"])
            hy_c = _hyena(hy_uc, lw)
            xc = _merge(xc, attn_c, pool_c, hy_c, mods_c, lw, g_fin, False, tile_c)
    return x
```

```python
import functools
import math

import jax
import jax.numpy as jnp
import numpy as np
from jax import lax
from jax.experimental import pallas as pl
from jax.experimental.pallas import tpu as pltpu

F32 = jnp.float32
BF16 = jnp.bfloat16

D_MODEL = 1024
GRID_W = 64
EPS = 1e-6
N_HEADS = 8
HEAD_V = 64
HEAD_NOPE = 64
HEAD_ROPE = 32
HEAD_PAD = 128
Q_RANK = 256
KV_RANK = 128
MLA_WIDTH = N_HEADS * HEAD_V
V_SLOT = 80
VT_ROWS = N_HEADS * V_SLOT
POOL_WIDTH = 256
HY_WIDTH = 256
MLA_SCALE = (HEAD_NOPE + HEAD_ROPE) ** -0.5
Q_SCALE = MLA_SCALE * math.log2(math.e)
ROPE_BASE = 10000.0
POOL_WINDOWS = (2, 4, 8, 16)
POOL_GROUP = POOL_WIDTH // len(POOL_WINDOWS)
POOL_PAD = 8
EDGE_ROWS = 16
HY_EMB = 33
HY_FFN = 64
HY_DECAY_TARGET = 1e-2
HY_DECAY_SHORT_PCT = 0.3
HY_DECAY_LONG_PCT = 1.5
D_FF = 4 * D_MODEL
COL_KV, COL_KR, COL_Q, COL_POOL, COL_HY, N_IN = 0, 128, 160, 416, 672, 1440
P_KV, P_Q, P_POOL, P_HY, P_KRA, P_KRB, N_IN_P = 0, 128, 384, 640, 1408, 1536, 1664

LANE = 128
VMEM_LIMIT = 56 * 1024 * 1024
TOKEN_TILE = 512
Q_TILE = 512
Q_TILES_PER_STEP = 2
CHUNK_UNROLL = 8
CONV_BLOCK = 512
FREQ_BLOCK = 128
FREQ_ROWS = 16
MLP_CHUNK = 1024
HIGHEST = lax.Precision.HIGHEST
HIGH = lax.Precision.HIGH


def _params(*sem):
    return pltpu.CompilerParams(dimension_semantics=sem, vmem_limit_bytes=VMEM_LIMIT)


def _const_spec(shape):
    zeros = (0,) * len(shape)
    return pl.BlockSpec(shape, lambda *_: zeros, pipeline_mode=pl.Buffered(1))


def _rms(x, g):
    return x * lax.rsqrt(jnp.mean(x * x, axis=-1, keepdims=True) + EPS) * g


def _bdot(a, b):
    return jnp.dot(a, b, preferred_element_type=F32)


def _split(x):
    hi = x.astype(BF16)
    return hi, (x - hi.astype(F32)).astype(BF16)


def _dot3(a, b):
    return _bdot(a[0], b[0]) + (_bdot(a[0], b[1]) + _bdot(a[1], b[0]))


def _modulation_kernel(c_ref, w_ref, b_ref, o_ref):
    c = c_ref[...]
    s = c / (1.0 + jnp.exp(-c))
    o_ref[0] = jnp.dot(s, w_ref[0], preferred_element_type=F32, precision=HIGHEST) + b_ref[0]


def _modulation(cond, w_mod, b_mod):
    depth = w_mod.shape[0]
    rows = cond.shape[0]
    return pl.pallas_call(
        _modulation_kernel,
        out_shape=jax.ShapeDtypeStruct((depth, rows, 6 * D_MODEL), F32),
        grid=(depth, 6),
        in_specs=[
            pl.BlockSpec((rows, D_MODEL), lambda l, j: (0, 0)),
            pl.BlockSpec((1, D_MODEL, D_MODEL), lambda l, j: (l, 0, j)),
            pl.BlockSpec((1, 1, D_MODEL), lambda l, j: (l, 0, j)),
        ],
        out_specs=pl.BlockSpec((1, rows, D_MODEL), lambda l, j: (l, 0, j)),
        compiler_params=_params("arbitrary", "arbitrary"),
        name="modulation",
    )(cond, w_mod, b_mod.reshape(depth, 1, 6 * D_MODEL))


def _inproj_kernel(x_ref, sh_ref, sc_ref, gmix_ref, win_ref, gkv_ref, wka_ref, wvt_ref,
                   gq_ref, wabt_ref, cos_ref, sin_ref, cost_ref, sint_ref,
                   qt_ref, k_ref, vt_ref, pool_ref, hy_ref):
    x = x_ref[0]
    h = (_rms(x, gmix_ref[...]) * (1.0 + sc_ref[0]) + sh_ref[0]).astype(BF16)
    proj = _bdot(h, win_ref[...])
    pool_ref[0] = proj[:, P_POOL:P_HY]
    hy_ref[0] = proj[:, P_HY:P_KRA].astype(hy_ref.dtype)
    kvn = _rms(proj[:, P_KV:P_Q], gkv_ref[...]).astype(BF16)
    kpad = _bdot(kvn, wka_ref[...])
    vt = lax.dot_general(wvt_ref[...], kvn, (((1,), (1,)), ((), ())),
                         preferred_element_type=F32)
    row = lax.broadcasted_iota(jnp.int32, vt.shape, 0)
    vt = jnp.where(row % V_SLOT >= HEAD_V, 1.0, vt).astype(vt_ref.dtype)
    vt_ref[0, 0] = vt.reshape(N_HEADS, V_SLOT, vt.shape[1])
    qn = _rms(proj[:, P_Q:P_POOL], gq_ref[...]).astype(BF16)
    abt = lax.dot_general(wabt_ref[...], qn, (((1,), (1,)), ((), ())),
                          preferred_element_type=F32)
    kr = proj[:, P_KRA:P_KRB] * cos_ref[...] + proj[:, P_KRB:N_IN_P] * sin_ref[...]
    cost = cost_ref[...]
    sint = sint_ref[...]
    width = N_HEADS * HEAD_PAD
    for hd in range(N_HEADS):
        sl = slice(hd * HEAD_PAD, (hd + 1) * HEAD_PAD)
        sl_b = slice(width + hd * HEAD_PAD, width + (hd + 1) * HEAD_PAD)
        k_ref[0, hd] = (kpad[:, sl] + kr).astype(k_ref.dtype)
        qt_ref[0, hd] = ((abt[sl, :] * cost + abt[sl_b, :] * sint) * Q_SCALE).astype(qt_ref.dtype)


def _inproj(x, shift, scale, lw, cos_t, sin_t, tile):
    bsz, n, _ = x.shape
    nt = n // tile
    width = N_HEADS * HEAD_PAD
    mod_spec = lambda k: pl.BlockSpec((1, 1, D_MODEL), lambda b, i, k=k: (b, 0, k))
    return pl.pallas_call(
        _inproj_kernel,
        out_shape=(
            jax.ShapeDtypeStruct((bsz, N_HEADS, HEAD_PAD, n), BF16),
            jax.ShapeDtypeStruct((bsz, N_HEADS, n, HEAD_PAD), BF16),
            jax.ShapeDtypeStruct((bsz, nt, N_HEADS, V_SLOT, tile), BF16),
            jax.ShapeDtypeStruct((bsz, n, POOL_WIDTH), F32),
            jax.ShapeDtypeStruct((bsz, n, 3 * HY_WIDTH), BF16),
        ),
        grid=(bsz, nt),
        in_specs=[
            pl.BlockSpec((1, tile, D_MODEL), lambda b, i: (b, i, 0)),
            mod_spec(shift[1]), mod_spec(scale[1]),
            _const_spec((1, D_MODEL)),
            _const_spec((D_MODEL, N_IN_P)),
            _const_spec((1, KV_RANK)),
            _const_spec((KV_RANK, width)),
            _const_spec((VT_ROWS, KV_RANK)),
            _const_spec((1, Q_RANK)),
            _const_spec((2 * width, Q_RANK)),
            pl.BlockSpec((tile, HEAD_PAD), lambda b, i: (i, 0)),
            pl.BlockSpec((tile, HEAD_PAD), lambda b, i: (i, 0)),
            pl.BlockSpec((HEAD_PAD, tile), lambda b, i: (0, i)),
            pl.BlockSpec((HEAD_PAD, tile), lambda b, i: (0, i)),
        ],
        out_specs=(
            pl.BlockSpec((1, N_HEADS, HEAD_PAD, tile), lambda b, i: (b, 0, 0, i)),
            pl.BlockSpec((1, N_HEADS, tile, HEAD_PAD), lambda b, i: (b, 0, i, 0)),
            pl.BlockSpec((1, 1, N_HEADS, V_SLOT, tile), lambda b, i: (b, i, 0, 0, 0)),
            pl.BlockSpec((1, tile, POOL_WIDTH), lambda b, i: (b, i, 0)),
            pl.BlockSpec((1, tile, 3 * HY_WIDTH), lambda b, i: (b, i, 0)),
        ),
        compiler_params=_params("parallel", "parallel"),
        name="inproj",
    )(x, shift[0], scale[0], lw["g_mix"], lw["w_in"], lw["g_kv"], lw["wka"], lw["wvt"],
      lw["g_q"], lw["wabt"], cos_t, sin_t, cos_t.T, sin_t.T)


def _attention_kernel(*refs, seg_chunks):
    qt_ref = refs[0]
    n_seg = len(seg_chunks)
    seg_refs = refs[1:1 + 2 * n_seg]
    o_ref = refs[1 + 2 * n_seg]
    out_ref, m_ref = refs[2 + 2 * n_seg], refs[3 + 2 * n_seg]
    s_bufs = refs[4 + 2 * n_seg:]
    tq = out_ref.shape[2]
    n_tiles = qt_ref.shape[3] // tq
    items = [(t, hd) for t in range(n_tiles) for hd in range(N_HEADS)]
    step_id = pl.program_id(0) + pl.program_id(1) + len(items)

    def stage(item1, item2, parity1):
        buf1, buf2 = s_bufs[parity1], s_bufs[1 - parity1]
        m2 = m_ref[...] if item2 is not None else None
        m1 = jnp.full((1, tq), -jnp.inf, F32)
        acc2 = jnp.zeros((V_SLOT, tq), F32)
        base = 0
        for si, (n_chunks, chunk) in enumerate(seg_chunks):
            k_ref, vt_ref = seg_refs[2 * si], seg_refs[2 * si + 1]
            for c in range(n_chunks):
                rows = slice(base + c * chunk, base + (c + 1) * chunk)
                if item1 is not None:
                    t1, h1 = item1
                    s = _bdot(k_ref[0, h1, c * chunk:(c + 1) * chunk, :],
                              qt_ref[0, h1, :, t1 * tq:(t1 + 1) * tq])
                    buf1[rows, :] = s
                    m1 = jnp.maximum(m1, jnp.max(s, axis=0, keepdims=True))
                if item2 is not None:
                    p = jnp.exp2(buf2[rows, :] - m2).astype(BF16)
                    acc2 = acc2 + _bdot(vt_ref[0, c, item2[1]], p)
            base += n_chunks * chunk
        if item2 is not None:
            t2, h2 = item2
            out_ref[h2] = acc2[:HEAD_V, :] / acc2[HEAD_V:HEAD_V + 1, :]
            if h2 == N_HEADS - 1:
                o_ref[0, t2 * tq:(t2 + 1) * tq, :] = (
                    out_ref[...].reshape(MLA_WIDTH, tq).T.astype(o_ref.dtype))
        if item1 is not None:
            m_ref[...] = m1

    for s in range(-1, len(items)):
        item1 = items[s + 1] if s + 1 < len(items) else None
        item2 = items[s] if s >= 0 else None
        pl.when(step_id >= s)(functools.partial(stage, item1, item2, (s + 1) % 2))


def _attention(q, segments):
    bsz, _, _, n = q.shape
    tq = min(Q_TILE, n)
    step_q = min(Q_TILES_PER_STEP * tq, n)
    in_specs = [pl.BlockSpec((1, N_HEADS, HEAD_PAD, step_q), lambda b, i: (b, 0, 0, i))]
    args = [q]
    seg_chunks = []
    for k, vt in segments:
        nk = k.shape[2]
        n_chunks, chunk = vt.shape[1], vt.shape[4]
        assert n_chunks * chunk == nk
        seg_chunks.append((n_chunks, chunk))
        in_specs.append(pl.BlockSpec((1, N_HEADS, nk, HEAD_PAD), lambda b, i: (b, 0, 0, 0)))
        in_specs.append(pl.BlockSpec((1, n_chunks, N_HEADS, V_SLOT, chunk),
                                     lambda b, i: (b, 0, 0, 0, 0)))
        args += [k, vt]
    n_keys = sum(nc * ch for nc, ch in seg_chunks)
    return pl.pallas_call(
        functools.partial(_attention_kernel, seg_chunks=tuple(seg_chunks)),
        out_shape=jax.ShapeDtypeStruct((bsz, n, MLA_WIDTH), BF16),
        grid=(bsz, n // step_q),
        in_specs=in_specs,
        out_specs=pl.BlockSpec((1, step_q, MLA_WIDTH), lambda b, i: (b, i, 0)),
        scratch_shapes=[pltpu.VMEM((N_HEADS, HEAD_V, tq), F32), pltpu.VMEM((1, tq), F32),
                        pltpu.VMEM((n_keys, tq), F32), pltpu.VMEM((n_keys, tq), F32)],
        compiler_params=_params("parallel", "arbitrary"),
        name="attention",
    )(*args)


def _shift_rows(a, k):
    n = a.shape[0]
    return pltpu.roll(a, k % n, 0)


def _pool_kernel(u_ref, w_ref, scale_ref, o_ref, ext_ref):
    n = u_ref.shape[1]
    u = u_ref[0]
    zeros = jnp.zeros((POOL_PAD, POOL_WIDTH), F32)
    ext_ref[0:POOL_PAD, :] = zeros
    ext_ref[POOL_PAD + n:, :] = zeros
    ext_ref[POOL_PAD:POOL_PAD + n, :] = u
    ext = ext_ref[...]
    s2 = ext + _shift_rows(ext, 1)
    s4 = _shift_rows(s2, 1) + _shift_rows(s2, -1)
    s8 = _shift_rows(s4, 2) + _shift_rows(s4, -2)
    s16 = _shift_rows(s8, 4) + _shift_rows(s8, -4)
    t = lax.broadcasted_iota(jnp.int32, (n, POOL_WIDTH), 0)
    lane = lax.broadcasted_iota(jnp.int32, (n, POOL_WIDTH), 1)
    mean = None
    for g, (w, s) in enumerate(zip(POOL_WINDOWS, (s2, s4, s8, s16))):
        lo = jnp.clip(t - w // 2, 0, n)
        hi = jnp.clip(t + w // 2, 0, n)
        mg = s[POOL_PAD:POOL_PAD + n, :] / (hi - lo).astype(F32)
        mean = mg if mean is None else jnp.where(lane >= g * POOL_GROUP, mg, mean)
    diff = (mean - u).astype(BF16)
    o_ref[0] = _bdot(diff, w_ref[...]) * scale_ref[...]


def _pool(u, w_bd, scale):
    bsz, n, _ = u.shape
    return pl.pallas_call(
        _pool_kernel,
        out_shape=jax.ShapeDtypeStruct((bsz, n, POOL_WIDTH), F32),
        grid=(bsz,),
        in_specs=[
            pl.BlockSpec((1, n, POOL_WIDTH), lambda b: (b, 0, 0)),
            _const_spec((POOL_WIDTH, POOL_WIDTH)),
            _const_spec((1, POOL_WIDTH)),
        ],
        out_specs=pl.BlockSpec((1, n, POOL_WIDTH), lambda b: (b, 0, 0)),
        scratch_shapes=[pltpu.VMEM((n + 2 * POOL_PAD, POOL_WIDTH), F32)],
        compiler_params=_params("parallel"),
        name="pool_mixer",
    )(u, w_bd, scale)


def _filter_kernel(z_ref, w1_ref, b1_ref, f1_ref, w2_ref, b2_ref, f2_ref, w3h_ref, w3l_ref,
                   dl_ref, g_ref, asum_ref, *, n):
    i = pl.program_id(0)
    rows_per = z_ref.shape[0]
    z = z_ref[...]
    h = jnp.sin(f1_ref[...] * (jnp.dot(z, w1_ref[...], preferred_element_type=F32,
                                       precision=HIGHEST) + b1_ref[...]))
    h = jnp.sin(f2_ref[...] * (jnp.dot(h, w2_ref[...], preferred_element_type=F32,
                                       precision=HIGHEST) + b2_ref[...]))
    h = _dot3(_split(h), (w3h_ref[...], w3l_ref[...]))
    r = i * rows_per + lax.broadcasted_iota(jnp.int32, (rows_per, HY_WIDTH), 0)
    backward = r < n
    pos = jnp.where(backward, n - r, r - n)
    t = pos.astype(F32) * (1.0 / (n - 1))
    decay = jnp.exp(-t * dl_ref[...])
    valid = r > 0

    @pl.when(i == 0)
    def _():
        asum_ref[...] = jnp.zeros_like(asum_ref)

    for o in range(2):
        fwd = h[:, (2 * o) * HY_WIDTH:(2 * o + 1) * HY_WIDTH]
        bwd = h[:, (2 * o + 1) * HY_WIDTH:(2 * o + 2) * HY_WIDTH]
        g = jnp.where(valid, jnp.where(backward, bwd, fwd) * decay, 0.0)
        g_ref[o] = g
        asum_ref[o] += jnp.sum(jnp.abs(g), axis=0, keepdims=True)


def _hyena_filter(n, lw):
    rows = 2 * n
    tile = min(rows, 1024)
    r = np.arange(rows)
    pos = np.where(r < n, n - r, r - n).astype(np.float64)
    t = pos / (n - 1)
    bands = (HY_EMB - 1) // 2
    freqs = np.linspace(1e-4, bands - 1, bands)[None, :]
    wpos = 2.0 * math.pi * pos[:, None] / n
    z = np.zeros((rows, LANE), np.float32)
    z[:, 0] = t
    z[:, 1:1 + bands] = np.cos(freqs * wpos)
    z[:, 1 + bands:HY_EMB] = -np.sin(freqs * wpos)
    deltas = np.abs(np.linspace(math.log(HY_DECAY_TARGET) / HY_DECAY_LONG_PCT,
                                math.log(HY_DECAY_TARGET) / HY_DECAY_SHORT_PCT, HY_WIDTH))
    deltas = jnp.asarray(deltas[None, :], F32)
    return pl.pallas_call(
        functools.partial(_filter_kernel, n=n),
        out_shape=(jax.ShapeDtypeStruct((2, rows, HY_WIDTH), F32),
                   jax.ShapeDtypeStruct((2, 1, HY_WIDTH), F32)),
        grid=(rows // tile,),
        in_specs=[
            pl.BlockSpec((tile, LANE), lambda i: (i, 0)),
            _const_spec((LANE, LANE)), _const_spec((1, LANE)), _const_spec((1, LANE)),
            _const_spec((LANE, LANE)), _const_spec((1, LANE)), _const_spec((1, LANE)),
            _const_spec((LANE, 4 * HY_WIDTH)), _const_spec((LANE, 4 * HY_WIDTH)),
            _const_spec((1, HY_WIDTH)),
        ],
        out_specs=(pl.BlockSpec((2, tile, HY_WIDTH), lambda i: (0, i, 0)),
                   pl.BlockSpec((2, 1, HY_WIDTH), lambda i: (0, 0, 0))),
        compiler_params=_params("arbitrary"),
        name="hyena_filter",
    )(jnp.asarray(z), lw["hy_w1"], lw["hy_b1"], lw["hy_f1"], lw["hy_w2"], lw["hy_b2"],
      lw["hy_f2"], *_split(lw["hy_w3"]), deltas)


def _dft_tables(blk):
    f = np.arange(blk, dtype=np.float64)[:, None]
    m = np.arange(blk, dtype=np.float64)[None, :]
    theta = math.pi * (2.0 * f + 1.0) * m / (2.0 * blk)
    return np.cos(theta), np.sin(theta)


def _spectra_kernel(g_ref, asum_ref, fwdh_ref, fwdl_ref, k_ref, prev_ref):
    e = pl.program_id(1)
    blk = g_ref.shape[1]
    s = _dot3((fwdh_ref[...], fwdl_ref[...]), _split(g_ref[0]))
    s = s / asum_ref[0]

    @pl.when(e > 0)
    def _():
        prev = prev_ref[...]
        f = lax.broadcasted_iota(jnp.int32, (blk, HY_WIDTH), 0)
        sgn = jnp.where(f % 2 == 0, 1.0, -1.0).astype(F32)
        k_ref[0, 0, 0] = s[:blk] - sgn * prev[blk:]
        k_ref[0, 0, 1] = s[blk:] + sgn * prev[:blk]

    prev_ref[...] = s


def _hyena_spectra(g, asum, blk):
    rows = g.shape[1]
    nseg = rows // blk
    cos, sin = _dft_tables(blk)
    fwd = jnp.asarray(np.concatenate([cos, -sin], axis=0), F32)
    return pl.pallas_call(
        _spectra_kernel,
        out_shape=jax.ShapeDtypeStruct((2, nseg - 1, 2, blk, HY_WIDTH), F32),
        grid=(2, nseg),
        in_specs=[
            pl.BlockSpec((1, blk, HY_WIDTH), lambda o, e: (o, e, 0)),
            pl.BlockSpec((1, 1, HY_WIDTH), lambda o, e: (o, 0, 0)),
            _const_spec((2 * blk, blk)), _const_spec((2 * blk, blk)),
        ],
        out_specs=pl.BlockSpec((1, 1, 2, blk, HY_WIDTH),
                               lambda o, e: (o, jnp.maximum(e - 1, 0), 0, 0, 0)),
        scratch_shapes=[pltpu.VMEM((2 * blk, HY_WIDTH), F32)],
        compiler_params=_params("arbitrary", "arbitrary"),
        name="hyena_spectra",
    )(g, asum, *_split(fwd))


def _short_conv_rows(src_ref, rows, w_ref, b_ref):
    r0, r1 = rows
    n = src_ref.shape[1]
    cur = src_ref[0, r0:r1, :].astype(F32)
    row = lax.broadcasted_iota(jnp.int32, cur.shape, 0)
    edge = jnp.zeros((1, cur.shape[1]), F32)
    before = src_ref[0, r0 - EDGE_ROWS:r0, :].astype(F32)[EDGE_ROWS - 1:, :] if r0 > 0 else edge
    after = src_ref[0, r1:r1 + EDGE_ROWS, :].astype(F32)[:1, :] if r1 < n else edge
    prev = jnp.where(row == 0, before, _shift_rows(cur, 1))
    nxt = jnp.where(row == r1 - r0 - 1, after, _shift_rows(cur, -1))
    return prev * w_ref[0:1, :] + cur * w_ref[1:2, :] + nxt * w_ref[2:3, :] + b_ref[...]


def _conv_kernel(u_ref, gate_ref, uw_ref, ub_ref, gw_ref, gb_ref, bias_ref, k_ref, fwd_ref,
                 inv_ref, o_ref, u32_ref, u16_ref, acc_ref, uf0_ref, uf1_ref, yf0_ref, yf1_ref,
                 *, conv_input):
    n = u_ref.shape[1]
    nf, blk = fwd_ref.shape[0], fwd_ref.shape[2]
    nb = n // blk
    fb = fwd_ref.shape[1] // 2
    uf_refs, yf_refs = (uf0_ref, uf1_ref), (yf0_ref, yf1_ref)
    step_id = pl.program_id(0) + nf + 2

    def forward(f):
        for j in range(nb):
            uf_refs[f % 2][j] = _bdot(fwd_ref[f], u16_ref[j * blk:(j + 1) * blk, :])

    def multiply(f):
        uf_ref, yf_ref = uf_refs[f % 2], yf_refs[f % 2]
        for c in range(fb // FREQ_ROWS):
            rs = slice(c * FREQ_ROWS, (c + 1) * FREQ_ROWS)
            rs_im = slice(fb + c * FREQ_ROWS, fb + (c + 1) * FREQ_ROWS)
            ks = slice(f * fb + c * FREQ_ROWS, f * fb + (c + 1) * FREQ_ROWS)
            for i in range(nb):
                yr = jnp.zeros((FREQ_ROWS, HY_WIDTH), F32)
                yi = jnp.zeros((FREQ_ROWS, HY_WIDTH), F32)
                for j in range(nb):
                    d = i - j + nb - 1
                    kr = k_ref[0, d, 0, ks, :]
                    ki = k_ref[0, d, 1, ks, :]
                    ur = uf_ref[j, rs, :]
                    ui = uf_ref[j, rs_im, :]
                    yr = yr + (kr * ur - ki * ui)
                    yi = yi + (kr * ui + ki * ur)
                yf_ref[i, rs, :] = yr.astype(BF16)
                yf_ref[i, rs_im, :] = yi.astype(BF16)

    def inverse(f):
        for i in range(nb):
            y = _bdot(inv_ref[f], yf_refs[f % 2][i])
            rows = slice(i * blk, (i + 1) * blk)
            acc_ref[rows, :] = y if f == 0 else acc_ref[rows, :] + y

    def prologue():
        for j in range(nb):
            rows = (j * blk, (j + 1) * blk)
            if conv_input:
                u = _short_conv_rows(u_ref, rows, uw_ref, ub_ref)
            else:
                u = u_ref[0, rows[0]:rows[1], :].astype(F32)
            u32_ref[rows[0]:rows[1], :] = u
            u16_ref[rows[0]:rows[1], :] = u.astype(BF16)
        forward(0)

    def stage(f):
        if f + 1 < nf:
            forward(f + 1)
        if f < nf:
            multiply(f)
        if f >= 1:
            inverse(f - 1)

    def epilogue():
        bias = bias_ref[0]
        for j in range(nb):
            rows = (j * blk, (j + 1) * blk)
            rs = slice(rows[0], rows[1])
            gate = _short_conv_rows(gate_ref, rows, gw_ref, gb_ref)
            conv = acc_ref[rs, :] + u32_ref[rs, :] * bias
            o_ref[0, rs, :] = (gate * conv).astype(o_ref.dtype)

    pl.when(step_id >= 0)(prologue)
    for f in range(nf + 1):
        pl.when(step_id >= f + 1)(functools.partial(stage, f))
    pl.when(step_id >= nf + 2)(epilogue)


def _hyena_conv(u_arr, u_col, gate_arr, gate_col, lw, khat, order, blk, conv_input, out_dtype):
    bsz, n, _ = gate_arr.shape
    nb = n // blk
    fb = min(FREQ_BLOCK, blk)
    nf = blk // fb
    cos, sin = _dft_tables(blk)
    fwd = np.concatenate([cos.reshape(nf, fb, blk), -sin.reshape(nf, fb, blk)], axis=1)
    inv = np.concatenate([cos.T.reshape(blk, nf, fb), -sin.T.reshape(blk, nf, fb)], axis=2)
    inv = np.transpose(inv, (1, 0, 2)) / blk
    cw, cb = lw["hy_conv_w"], lw["hy_conv_b"]
    col = lambda c: (lambda b, c=c: (0, c))
    once = pl.Buffered(1)
    return pl.pallas_call(
        functools.partial(_conv_kernel, conv_input=conv_input),
        out_shape=jax.ShapeDtypeStruct((bsz, n, HY_WIDTH), out_dtype),
        grid=(bsz,),
        in_specs=[
            pl.BlockSpec((1, n, HY_WIDTH), lambda b, c=u_col: (b, 0, c)),
            pl.BlockSpec((1, n, HY_WIDTH), lambda b, c=gate_col: (b, 0, c)),
            pl.BlockSpec((3, HY_WIDTH), col(u_col if conv_input else 0)),
            pl.BlockSpec((1, HY_WIDTH), col(u_col if conv_input else 0)),
            pl.BlockSpec((3, HY_WIDTH), col(gate_col)),
            pl.BlockSpec((1, HY_WIDTH), col(gate_col)),
            pl.BlockSpec((1, 1, HY_WIDTH), lambda b, o=order: (o, 0, 0)),
            pl.BlockSpec((1, 2 * nb - 1, 2, blk, HY_WIDTH), lambda b, o=order: (o, 0, 0, 0, 0),
                         pipeline_mode=once),
            pl.BlockSpec((nf, 2 * fb, blk), lambda b: (0, 0, 0), pipeline_mode=once),
            pl.BlockSpec((nf, blk, 2 * fb), lambda b: (0, 0, 0), pipeline_mode=once),
        ],
        out_specs=pl.BlockSpec((1, n, HY_WIDTH), lambda b: (b, 0, 0)),
        scratch_shapes=[
            pltpu.VMEM((n, HY_WIDTH), F32),
            pltpu.VMEM((n, HY_WIDTH), BF16),
            pltpu.VMEM((n, HY_WIDTH), F32),
            pltpu.VMEM((nb, 2 * fb, HY_WIDTH), F32),
            pltpu.VMEM((nb, 2 * fb, HY_WIDTH), F32),
            pltpu.VMEM((nb, 2 * fb, HY_WIDTH), BF16),
            pltpu.VMEM((nb, 2 * fb, HY_WIDTH), BF16),
        ],
        compiler_params=_params("parallel"),
        name="hyena_conv%d" % order,
    )(u_arr, gate_arr, cw, cb, cw, cb, lw["hy_bias"], khat,
      jnp.asarray(fwd, BF16), jnp.asarray(inv, BF16))


def _hyena(hy_u, lw):
    n = hy_u.shape[1]
    blk = min(CONV_BLOCK, n)
    g, asum = _hyena_filter(n, lw)
    khat = _hyena_spectra(g, asum, blk)
    z = _hyena_conv(hy_u, 0, hy_u, 1, lw, khat, 0, blk, True, BF16)
    return _hyena_conv(z, 0, hy_u, 2, lw, khat, 1, blk, False, F32)


def _merge_kernel(x_ref, attn_ref, pool_ref, hy_ref, g1_ref, sh2_ref, sc2_ref, g2_ref,
                  gout_ref, wout_ref, gmlp_ref, w1_ref, w2_ref, gfin_ref, o_ref, *, final_norm):
    gout = gout_ref[...]
    a = _rms(attn_ref[0].astype(F32), gout[:, :MLA_WIDTH]).astype(BF16)
    p = _rms(pool_ref[0], gout[:, MLA_WIDTH:MLA_WIDTH + POOL_WIDTH]).astype(BF16)
    hh = _rms(hy_ref[0], gout[:, MLA_WIDTH + POOL_WIDTH:]).astype(BF16)
    y = (_bdot(a, wout_ref[0:MLA_WIDTH, :])
         + _bdot(p, wout_ref[MLA_WIDTH:MLA_WIDTH + POOL_WIDTH, :])
         + _bdot(hh, wout_ref[MLA_WIDTH + POOL_WIDTH:, :]))
    x1 = x_ref[0] + g1_ref[0] * y
    h2 = (_rms(x1, gmlp_ref[...]) * (1.0 + sc2_ref[0]) + sh2_ref[0]).astype(BF16)
    y2 = jnp.zeros_like(x1)
    for c in range(D_FF // MLP_CHUNK):
        cs = slice(c * MLP_CHUNK, (c + 1) * MLP_CHUNK)
        hid = jnp.maximum(_bdot(h2, w1_ref[:, cs]), 0.0)
        y2 = y2 + _bdot((hid * hid).astype(BF16), w2_ref[cs, :])
    x2 = x1 + g2_ref[0] * y2
    if final_norm:
        x2 = _rms(x2, gfin_ref[...])
    o_ref[0] = x2


def _merge(x, attn, pool, hy, mods, lw, g_final, final_norm, tile):
    bsz, n, _ = x.shape
    nt = n // tile
    tok = lambda w: pl.BlockSpec((1, tile, w), lambda b, i: (b, i, 0))
    mod_spec = lambda k: pl.BlockSpec((1, 1, D_MODEL), lambda b, i, k=k: (b, 0, k))
    return pl.pallas_call(
        functools.partial(_merge_kernel, final_norm=final_norm),
        out_shape=jax.ShapeDtypeStruct((bsz, n, D_MODEL), F32),
        grid=(bsz, nt),
        in_specs=[
            tok(D_MODEL), tok(MLA_WIDTH), tok(POOL_WIDTH), tok(HY_WIDTH),
            mod_spec(2), mod_spec(3), mod_spec(4), mod_spec(5),
            _const_spec((1, D_MODEL)), _const_spec((D_MODEL, D_MODEL)),
            _const_spec((1, D_MODEL)), _const_spec((D_MODEL, D_FF)),
            _const_spec((D_FF, D_MODEL)), _const_spec((1, D_MODEL)),
        ],
        out_specs=tok(D_MODEL),
        compiler_params=_params("parallel", "parallel"),
        name="merge_mlp",
    )(x, attn, pool, hy, mods, mods, mods, mods, lw["g_out"], lw["w_out"], lw["g_mlp"],
      lw["w_mlp1"], lw["w_mlp2"], g_final)


_ROPE_SWAP = np.concatenate([np.arange(8, 16), np.arange(0, 8), np.arange(24, 32), np.arange(16, 24)])


def _rope_tables(n):
    idx = np.arange(n)
    r = (idx // GRID_W).astype(np.float32)
    c = (idx % GRID_W).astype(np.float32)
    n_freq = HEAD_ROPE // 4
    inv = (ROPE_BASE ** (-np.arange(n_freq, dtype=np.float32) / n_freq)).astype(np.float32)
    ar, ac = r[:, None] * inv, c[:, None] * inv
    cos = np.zeros((n, HEAD_PAD), np.float32)
    sin = np.zeros((n, HEAD_PAD), np.float32)
    cos[:, :HEAD_NOPE] = 1.0
    cos[:, HEAD_NOPE:HEAD_NOPE + HEAD_ROPE] = np.concatenate(
        [np.cos(ar), np.cos(ar), np.cos(ac), np.cos(ac)], axis=1)
    sin[:, HEAD_NOPE:HEAD_NOPE + HEAD_ROPE] = np.concatenate(
        [-np.sin(ar), np.sin(ar), -np.sin(ac), np.sin(ac)], axis=1)
    return jnp.asarray(cos), jnp.asarray(sin)


def _identity_tables(n):
    cos = np.zeros((n, HEAD_PAD), np.float32)
    cos[:, :HEAD_NOPE + HEAD_ROPE] = 1.0
    return jnp.asarray(cos), jnp.zeros((n, HEAD_PAD), F32)


def _layer_weights(l, w):
    w_in = w["w_in"][l]
    kr = w_in[:, COL_KR:COL_Q]
    slot = lambda m: jnp.zeros((D_MODEL, HEAD_PAD), F32).at[:, HEAD_NOPE:HEAD_NOPE + HEAD_ROPE].set(m)
    w_in_p = jnp.concatenate([w_in[:, COL_KV:COL_KR], w_in[:, COL_Q:COL_POOL],
                              w_in[:, COL_POOL:COL_HY], w_in[:, COL_HY:],
                              slot(kr), slot(kr[:, _ROPE_SWAP])], axis=1).astype(BF16)
    wkv = w["w_kv_up"][l].reshape(KV_RANK, N_HEADS, HEAD_NOPE + HEAD_V)
    wka = wkv.at[:, :, HEAD_NOPE:].set(0.0).reshape(KV_RANK, N_HEADS * HEAD_PAD).astype(BF16)
    wvt = jnp.zeros((KV_RANK, N_HEADS, V_SLOT), F32).at[:, :, :HEAD_V].set(wkv[:, :, HEAD_NOPE:])
    wvt = wvt.reshape(KV_RANK, VT_ROWS).T.astype(BF16)
    wq = w["w_q_up"][l].reshape(Q_RANK, N_HEADS, HEAD_NOPE + HEAD_ROPE)
    wa = jnp.zeros((Q_RANK, N_HEADS, HEAD_PAD), F32).at[:, :, :HEAD_NOPE + HEAD_ROPE].set(wq)
    wb = jnp.zeros((Q_RANK, N_HEADS, HEAD_PAD), F32).at[:, :, HEAD_NOPE:HEAD_NOPE + HEAD_ROPE].set(
        wq[:, :, HEAD_NOPE:][:, :, _ROPE_SWAP])
    wabt = jnp.concatenate([wa.reshape(Q_RANK, -1), wb.reshape(Q_RANK, -1)], axis=1).T.astype(BF16)
    pool_bd = jnp.zeros((POOL_WIDTH, POOL_WIDTH), F32)
    for g in range(len(POOL_WINDOWS)):
        sl = slice(g * POOL_GROUP, (g + 1) * POOL_GROUP)
        pool_bd = pool_bd.at[sl, sl].set(w["pool_w"][l, g])
    pad_rows = lambda m, rows: jnp.zeros((rows, m.shape[1]), F32).at[:m.shape[0]].set(m)
    pad_cols = lambda m, cols: jnp.zeros((m.shape[0], cols), F32).at[:, :m.shape[1]].set(m)
    row = lambda v: v.reshape(1, -1)
    return {
        "g_mix": row(w["g_mix"][l]), "w_in": w_in_p, "g_kv": row(w["g_kv"][l]), "wka": wka,
        "wvt": wvt, "g_q": row(w["g_q"][l]), "wabt": wabt,
        "pool_w": pool_bd.astype(BF16), "pool_scale": row(w["pool_scale"][l]),
        "hy_conv_w": w["hy_conv_w"][l], "hy_conv_b": row(w["hy_conv_b"][l]),
        "hy_w1": pad_cols(pad_rows(w["hy_f_w1"][l], LANE), LANE),
        "hy_b1": pad_cols(row(w["hy_f_b1"][l]), LANE),
        "hy_f1": pad_cols(row(w["hy_f_freq1"][l]), LANE),
        "hy_w2": pad_cols(pad_rows(w["hy_f_w2"][l], LANE), LANE),
        "hy_b2": pad_cols(row(w["hy_f_b2"][l]), LANE),
        "hy_f2": pad_cols(row(w["hy_f_freq2"][l]), LANE),
        "hy_w3": pad_rows(w["hy_f_w3"][l], LANE),
        "hy_bias": w["hy_bias"][l].reshape(2, 1, HY_WIDTH),
        "g_out": row(w["g_out"][l]), "w_out": w["w_out"][l].astype(BF16),
        "g_mlp": row(w["g_mlp"][l]), "w_mlp1": w["w_mlp1"][l].astype(BF16),
        "w_mlp2": w["w_mlp2"][l].astype(BF16),
    }


def kernel(x, c, ctx, c_ctx, w_mod, b_mod, g_mix, g_mlp, w_in, g_q, w_q_up, g_kv, w_kv_up,
           pool_w, pool_scale, hy_conv_w, hy_conv_b, hy_f_w1, hy_f_b1, hy_f_freq1, hy_f_w2,
           hy_f_b2, hy_f_freq2, hy_f_w3, hy_bias, g_out, w_out, w_mlp1, w_mlp2, g_final):
    w = dict(g_mix=g_mix, g_mlp=g_mlp, w_in=w_in, g_q=g_q, w_q_up=w_q_up, g_kv=g_kv,
             w_kv_up=w_kv_up, pool_w=pool_w, pool_scale=pool_scale, hy_conv_w=hy_conv_w,
             hy_conv_b=hy_conv_b, hy_f_w1=hy_f_w1, hy_f_b1=hy_f_b1, hy_f_freq1=hy_f_freq1,
             hy_f_w2=hy_f_w2, hy_f_b2=hy_f_b2, hy_f_freq2=hy_f_freq2, hy_f_w3=hy_f_w3,
             hy_bias=hy_bias, g_out=g_out, w_out=w_out, w_mlp1=w_mlp1, w_mlp2=w_mlp2)
    depth = w_mod.shape[0]
    bsz, n, _ = x.shape
    n_ctx = ctx.shape[1]
    tile = min(TOKEN_TILE, n)
    tile_c = min(TOKEN_TILE, n_ctx)

    rows = -(-(bsz + 1) // 8) * 8
    cond = jnp.zeros((rows, D_MODEL), F32).at[:bsz].set(c).at[bsz].set(c_ctx)
    mods_all = _modulation(cond, w_mod, b_mod)
    cos_l, sin_l = _rope_tables(n)
    cos_c, sin_c = _identity_tables(n_ctx)
    g_fin = g_final.reshape(1, D_MODEL)

    xc = ctx
    for l in range(depth):
        last = l == depth - 1
        lw = _layer_weights(l, w)
        mods = mods_all[l, :bsz].reshape(bsz, 1, 6 * D_MODEL)
        mods_c = jnp.broadcast_to(mods_all[l, bsz].reshape(1, 1, 6 * D_MODEL),
                                  (bsz, 1, 6 * D_MODEL))
        q, k, vt, pool_u, hy_u = _inproj(x, (mods, 0), (mods, 1), lw, cos_l, sin_l, tile)
        qc, kc, vtc, pool_uc, hy_uc = _inproj(xc, (mods_c, 0), (mods_c, 1), lw, cos_c, sin_c,
                                              tile_c)
        attn = _attention(q, [(kc, vtc), (k, vt)])
        pool = _pool(pool_u, lw["pool_w"], lw["pool_scale"])
        hy = _hyena(hy_u, lw)
        x = _merge(x, attn, pool, hy, mods, lw, g_fin, last, tile)
        if not last:
            attn_c = _attention(qc, [(kc, vtc)])
            pool_c = _pool(pool_uc, lw["pool_w"], lw["pool_scale"])
            hy_c = _hyena(hy_uc, lw)
            xc = _merge(xc, attn_c, pool_c, hy_c, mods_c, lw, g_fin, False, tile_c)
    return x
```

```python
import functools
import math

import jax
import jax.numpy as jnp
import numpy as np
from jax import lax
from jax.experimental import pallas as pl
from jax.experimental.pallas import tpu as pltpu

F32 = jnp.float32
BF16 = jnp.bfloat16

D_MODEL = 1024
GRID_W = 64
EPS = 1e-6
N_HEADS = 8
HEAD_V = 64
HEAD_NOPE = 64
HEAD_ROPE = 32
HEAD_PAD = 128
Q_RANK = 256
KV_RANK = 128
MLA_WIDTH = N_HEADS * HEAD_V
V_SLOT = 80
VT_ROWS = N_HEADS * V_SLOT
Q_ROWS = N_HEADS * (HEAD_NOPE + 2 * HEAD_ROPE)
POOL_WIDTH = 256
HY_WIDTH = 256
MLA_SCALE = (HEAD_NOPE + HEAD_ROPE) ** -0.5
Q_SCALE = MLA_SCALE * math.log2(math.e)
ROPE_BASE = 10000.0
POOL_WINDOWS = (2, 4, 8, 16)
POOL_GROUP = POOL_WIDTH // len(POOL_WINDOWS)
POOL_PAD = 8
EDGE_ROWS = 16
HY_EMB = 33
HY_FFN = 64
HY_DECAY_TARGET = 1e-2
HY_DECAY_SHORT_PCT = 0.3
HY_DECAY_LONG_PCT = 1.5
D_FF = 4 * D_MODEL
COL_KV, COL_KR, COL_Q, COL_POOL, COL_HY, N_IN = 0, 128, 160, 416, 672, 1440
P_KV, P_Q, P_POOL, P_HY, P_KR, N_IN_P = 0, 128, 384, 640, 1408, 1536

LANE = 128
VMEM_LIMIT = 56 * 1024 * 1024
TOKEN_TILE = 512
Q_TILE = 512
Q_TILES_PER_STEP = 2
CHUNK_UNROLL = 8
CONV_BLOCK = 512
FREQ_BLOCK = 128
FREQ_ROWS = 16
MLP_CHUNK = 1024
HIGHEST = lax.Precision.HIGHEST
HIGH = lax.Precision.HIGH


def _params(*sem):
    return pltpu.CompilerParams(dimension_semantics=sem, vmem_limit_bytes=VMEM_LIMIT)


def _const_spec(shape):
    zeros = (0,) * len(shape)
    return pl.BlockSpec(shape, lambda *_: zeros, pipeline_mode=pl.Buffered(1))


def _rms(x, g):
    return x * lax.rsqrt(jnp.mean(x * x, axis=-1, keepdims=True) + EPS) * g


def _bdot(a, b):
    return jnp.dot(a, b, preferred_element_type=F32)


def _split(x):
    hi = x.astype(BF16)
    return hi, (x - hi.astype(F32)).astype(BF16)


def _dot3(a, b):
    return _bdot(a[0], b[0]) + (_bdot(a[0], b[1]) + _bdot(a[1], b[0]))


def _modulation_kernel(c_ref, w_ref, b_ref, o_ref):
    c = c_ref[...]
    s = c / (1.0 + jnp.exp(-c))
    o_ref[0] = jnp.dot(s, w_ref[0], preferred_element_type=F32, precision=HIGHEST) + b_ref[0]


def _modulation(cond, w_mod, b_mod):
    depth = w_mod.shape[0]
    rows = cond.shape[0]
    return pl.pallas_call(
        _modulation_kernel,
        out_shape=jax.ShapeDtypeStruct((depth, rows, 6 * D_MODEL), F32),
        grid=(depth, 6),
        in_specs=[
            pl.BlockSpec((rows, D_MODEL), lambda l, j: (0, 0)),
            pl.BlockSpec((1, D_MODEL, D_MODEL), lambda l, j: (l, 0, j)),
            pl.BlockSpec((1, 1, D_MODEL), lambda l, j: (l, 0, j)),
        ],
        out_specs=pl.BlockSpec((1, rows, D_MODEL), lambda l, j: (l, 0, j)),
        compiler_params=_params("arbitrary", "arbitrary"),
        name="modulation",
    )(cond, w_mod, b_mod.reshape(depth, 1, 6 * D_MODEL))


def _inproj_kernel(x_ref, sh_ref, sc_ref, gmix_ref, win_ref, gkv_ref, wka_ref, wvt_ref,
                   gq_ref, wqt_ref, cos_ref, sin_ref, cost_ref, sint_ref,
                   qt_ref, k_ref, vt_ref, pool_ref, hy_ref):
    x = x_ref[0]
    h = (_rms(x, gmix_ref[...]) * (1.0 + sc_ref[0]) + sh_ref[0]).astype(BF16)
    proj = _bdot(h, win_ref[...])
    pool_ref[0] = proj[:, P_POOL:P_HY]
    hy_ref[0] = proj[:, P_HY:P_KR].astype(hy_ref.dtype)
    kvn = _rms(proj[:, P_KV:P_Q], gkv_ref[...]).astype(BF16)
    kpad = _bdot(kvn, wka_ref[...])
    vt = lax.dot_general(wvt_ref[...], kvn, (((1,), (1,)), ((), ())),
                         preferred_element_type=F32)
    row = lax.broadcasted_iota(jnp.int32, vt.shape, 0)
    vt = jnp.where(row % V_SLOT >= HEAD_V, 1.0, vt).astype(vt_ref.dtype)
    vt_ref[0, 0] = vt.reshape(N_HEADS, V_SLOT, vt.shape[1])
    qn = _rms(proj[:, P_Q:P_POOL], gq_ref[...]).astype(BF16)
    qt = lax.dot_general(wqt_ref[...], qn, (((1,), (1,)), ((), ())),
                         preferred_element_type=F32)
    krab = proj[:, P_KR:N_IN_P]
    kr = krab * cos_ref[...] + pltpu.roll(krab, HEAD_PAD - HEAD_ROPE, 1) * sin_ref[...]
    cost = cost_ref[...]
    sint = sint_ref[...]
    head_q = HEAD_NOPE + HEAD_ROPE
    pad = jnp.zeros((HEAD_PAD - head_q, qt.shape[1]), F32)
    for hd in range(N_HEADS):
        sl = slice(hd * HEAD_PAD, (hd + 1) * HEAD_PAD)
        k_ref[0, hd] = (kpad[:, sl] + kr).astype(k_ref.dtype)
        a = qt[hd * head_q:(hd + 1) * head_q, :]
        b = qt[N_HEADS * head_q + hd * HEAD_ROPE:N_HEADS * head_q + (hd + 1) * HEAD_ROPE, :]
        rope = a[HEAD_NOPE:, :] * cost + b * sint
        q_h = jnp.concatenate([a[:HEAD_NOPE, :], rope, pad], axis=0) * Q_SCALE
        qt_ref[0, hd] = q_h.astype(qt_ref.dtype)


def _inproj(x, shift, scale, lw, cos_t, sin_t, tile):
    bsz, n, _ = x.shape
    nt = n // tile
    width = N_HEADS * HEAD_PAD
    mod_spec = lambda k: pl.BlockSpec((1, 1, D_MODEL), lambda b, i, k=k: (b, 0, k))
    return pl.pallas_call(
        _inproj_kernel,
        out_shape=(
            jax.ShapeDtypeStruct((bsz, N_HEADS, HEAD_PAD, n), BF16),
            jax.ShapeDtypeStruct((bsz, N_HEADS, n, HEAD_PAD), BF16),
            jax.ShapeDtypeStruct((bsz, nt, N_HEADS, V_SLOT, tile), BF16),
            jax.ShapeDtypeStruct((bsz, n, POOL_WIDTH), F32),
            jax.ShapeDtypeStruct((bsz, n, 3 * HY_WIDTH), BF16),
        ),
        grid=(bsz, nt),
        in_specs=[
            pl.BlockSpec((1, tile, D_MODEL), lambda b, i: (b, i, 0)),
            mod_spec(shift[1]), mod_spec(scale[1]),
            _const_spec((1, D_MODEL)),
            _const_spec((D_MODEL, N_IN_P)),
            _const_spec((1, KV_RANK)),
            _const_spec((KV_RANK, width)),
            _const_spec((VT_ROWS, KV_RANK)),
            _const_spec((1, Q_RANK)),
            _const_spec((Q_ROWS, Q_RANK)),
            pl.BlockSpec((tile, HEAD_PAD), lambda b, i: (i, 0)),
            pl.BlockSpec((tile, HEAD_PAD), lambda b, i: (i, 0)),
            pl.BlockSpec((HEAD_ROPE, tile), lambda b, i: (0, i)),
            pl.BlockSpec((HEAD_ROPE, tile), lambda b, i: (0, i)),
        ],
        out_specs=(
            pl.BlockSpec((1, N_HEADS, HEAD_PAD, tile), lambda b, i: (b, 0, 0, i)),
            pl.BlockSpec((1, N_HEADS, tile, HEAD_PAD), lambda b, i: (b, 0, i, 0)),
            pl.BlockSpec((1, 1, N_HEADS, V_SLOT, tile), lambda b, i: (b, i, 0, 0, 0)),
            pl.BlockSpec((1, tile, POOL_WIDTH), lambda b, i: (b, i, 0)),
            pl.BlockSpec((1, tile, 3 * HY_WIDTH), lambda b, i: (b, i, 0)),
        ),
        compiler_params=_params("parallel", "parallel"),
        name="inproj",
    )(x, shift[0], scale[0], lw["g_mix"], lw["w_in"], lw["g_kv"], lw["wka"], lw["wvt"],
      lw["g_q"], lw["wqt"], cos_t, sin_t,
      cos_t[:, HEAD_NOPE:HEAD_NOPE + HEAD_ROPE].T, sin_t[:, HEAD_NOPE:HEAD_NOPE + HEAD_ROPE].T)


def _attention_kernel(*refs, seg_chunks):
    qt_ref = refs[0]
    n_seg = len(seg_chunks)
    seg_refs = refs[1:1 + 2 * n_seg]
    o_ref = refs[1 + 2 * n_seg]
    out_ref, m_ref = refs[2 + 2 * n_seg], refs[3 + 2 * n_seg]
    s_bufs = refs[4 + 2 * n_seg:]
    tq = out_ref.shape[2]
    n_tiles = qt_ref.shape[3] // tq
    items = [(t, hd) for t in range(n_tiles) for hd in range(N_HEADS)]
    step_id = pl.program_id(0) + pl.program_id(1) + len(items)

    def stage(item1, item2, parity1):
        buf1, buf2 = s_bufs[parity1], s_bufs[1 - parity1]
        m2 = m_ref[...] if item2 is not None else None
        m1 = jnp.full((1, tq), -jnp.inf, F32)
        acc2 = jnp.zeros((V_SLOT, tq), F32)
        base = 0
        for si, (n_chunks, chunk) in enumerate(seg_chunks):
            k_ref, vt_ref = seg_refs[2 * si], seg_refs[2 * si + 1]
            for c in range(n_chunks):
                rows = slice(base + c * chunk, base + (c + 1) * chunk)
                if item1 is not None:
                    t1, h1 = item1
                    s = _bdot(k_ref[0, h1, c * chunk:(c + 1) * chunk, :],
                              qt_ref[0, h1, :, t1 * tq:(t1 + 1) * tq])
                    buf1[rows, :] = s
                    m1 = jnp.maximum(m1, jnp.max(s, axis=0, keepdims=True))
                if item2 is not None:
                    p = jnp.exp2(buf2[rows, :] - m2).astype(BF16)
                    acc2 = acc2 + _bdot(vt_ref[0, c, item2[1]], p)
            base += n_chunks * chunk
        if item2 is not None:
            t2, h2 = item2
            out_ref[h2] = acc2[:HEAD_V, :] / acc2[HEAD_V:HEAD_V + 1, :]
            if h2 == N_HEADS - 1:
                o_ref[0, t2 * tq:(t2 + 1) * tq, :] = (
                    out_ref[...].reshape(MLA_WIDTH, tq).T.astype(o_ref.dtype))
        if item1 is not None:
            m_ref[...] = m1

    for s in range(-1, len(items)):
        item1 = items[s + 1] if s + 1 < len(items) else None
        item2 = items[s] if s >= 0 else None
        pl.when(step_id >= s)(functools.partial(stage, item1, item2, (s + 1) % 2))


def _attention(q, segments):
    bsz, _, _, n = q.shape
    tq = min(Q_TILE, n)
    step_q = min(Q_TILES_PER_STEP * tq, n)
    in_specs = [pl.BlockSpec((1, N_HEADS, HEAD_PAD, step_q), lambda b, i: (b, 0, 0, i))]
    args = [q]
    seg_chunks = []
    for k, vt in segments:
        nk = k.shape[2]
        n_chunks, chunk = vt.shape[1], vt.shape[4]
        assert n_chunks * chunk == nk
        seg_chunks.append((n_chunks, chunk))
        in_specs.append(pl.BlockSpec((1, N_HEADS, nk, HEAD_PAD), lambda b, i: (b, 0, 0, 0)))
        in_specs.append(pl.BlockSpec((1, n_chunks, N_HEADS, V_SLOT, chunk),
                                     lambda b, i: (b, 0, 0, 0, 0)))
        args += [k, vt]
    n_keys = sum(nc * ch for nc, ch in seg_chunks)
    return pl.pallas_call(
        functools.partial(_attention_kernel, seg_chunks=tuple(seg_chunks)),
        out_shape=jax.ShapeDtypeStruct((bsz, n, MLA_WIDTH), BF16),
        grid=(bsz, n // step_q),
        in_specs=in_specs,
        out_specs=pl.BlockSpec((1, step_q, MLA_WIDTH), lambda b, i: (b, i, 0)),
        scratch_shapes=[pltpu.VMEM((N_HEADS, HEAD_V, tq), F32), pltpu.VMEM((1, tq), F32),
                        pltpu.VMEM((n_keys, tq), F32), pltpu.VMEM((n_keys, tq), F32)],
        compiler_params=_params("parallel", "arbitrary"),
        name="attention",
    )(*args)


def _shift_rows(a, k):
    n = a.shape[0]
    return pltpu.roll(a, k % n, 0)


def _pool_kernel(u_ref, inv_ref, w_ref, scale_ref, o_ref, ext_ref):
    n = u_ref.shape[1]
    u = u_ref[0]
    zeros = jnp.zeros((POOL_PAD, POOL_WIDTH), F32)
    ext_ref[0:POOL_PAD, :] = zeros
    ext_ref[POOL_PAD + n:, :] = zeros
    ext_ref[POOL_PAD:POOL_PAD + n, :] = u
    ext = ext_ref[...]
    s2 = ext + _shift_rows(ext, 1)
    s4 = _shift_rows(s2, 1) + _shift_rows(s2, -1)
    s8 = _shift_rows(s4, 2) + _shift_rows(s4, -2)
    s16 = _shift_rows(s8, 4) + _shift_rows(s8, -4)
    lane = lax.broadcasted_iota(jnp.int32, (n, POOL_WIDTH), 1)
    wsum = None
    for g, s in enumerate((s2, s4, s8, s16)):
        sg = s[POOL_PAD:POOL_PAD + n, :]
        wsum = sg if wsum is None else jnp.where(lane >= g * POOL_GROUP, sg, wsum)
    diff = (wsum * inv_ref[...] - u).astype(BF16)
    o_ref[0] = _bdot(diff, w_ref[...]) * scale_ref[...]


def _pool(u, w_bd, scale):
    bsz, n, _ = u.shape
    t = np.arange(n)
    inv = np.concatenate(
        [np.repeat((1.0 / (np.clip(t + w // 2, 0, n) - np.clip(t - w // 2, 0, n)))[:, None],
                   POOL_GROUP, axis=1) for w in POOL_WINDOWS], axis=1).astype(np.float32)
    return pl.pallas_call(
        _pool_kernel,
        out_shape=jax.ShapeDtypeStruct((bsz, n, POOL_WIDTH), F32),
        grid=(bsz,),
        in_specs=[
            pl.BlockSpec((1, n, POOL_WIDTH), lambda b: (b, 0, 0)),
            _const_spec((n, POOL_WIDTH)),
            _const_spec((POOL_WIDTH, POOL_WIDTH)),
            _const_spec((1, POOL_WIDTH)),
        ],
        out_specs=pl.BlockSpec((1, n, POOL_WIDTH), lambda b: (b, 0, 0)),
        scratch_shapes=[pltpu.VMEM((n + 2 * POOL_PAD, POOL_WIDTH), F32)],
        compiler_params=_params("parallel"),
        name="pool_mixer",
    )(u, jnp.asarray(inv), w_bd, scale)


def _filter_kernel(z_ref, w1_ref, b1_ref, f1_ref, w2_ref, b2_ref, f2_ref, w3h_ref, w3l_ref,
                   dl_ref, g_ref, asum_ref, *, n):
    i = pl.program_id(0)
    rows_per = z_ref.shape[0]
    z = z_ref[...]
    h = jnp.sin(f1_ref[...] * (jnp.dot(z, w1_ref[...], preferred_element_type=F32,
                                       precision=HIGHEST) + b1_ref[...]))
    h = jnp.sin(f2_ref[...] * (jnp.dot(h, w2_ref[...], preferred_element_type=F32,
                                       precision=HIGHEST) + b2_ref[...]))
    h = _dot3(_split(h), (w3h_ref[...], w3l_ref[...]))
    r = i * rows_per + lax.broadcasted_iota(jnp.int32, (rows_per, HY_WIDTH), 0)
    backward = r < n
    pos = jnp.where(backward, n - r, r - n)
    t = pos.astype(F32) * (1.0 / (n - 1))
    decay = jnp.exp(-t * dl_ref[...])
    valid = r > 0

    @pl.when(i == 0)
    def _():
        asum_ref[...] = jnp.zeros_like(asum_ref)

    for o in range(2):
        fwd = h[:, (2 * o) * HY_WIDTH:(2 * o + 1) * HY_WIDTH]
        bwd = h[:, (2 * o + 1) * HY_WIDTH:(2 * o + 2) * HY_WIDTH]
        g = jnp.where(valid, jnp.where(backward, bwd, fwd) * decay, 0.0)
        g_ref[o] = g
        asum_ref[o] += jnp.sum(jnp.abs(g), axis=0, keepdims=True)


def _hyena_filter(n, lw):
    rows = 2 * n
    tile = min(rows, 1024)
    r = np.arange(rows)
    pos = np.where(r < n, n - r, r - n).astype(np.float64)
    t = pos / (n - 1)
    bands = (HY_EMB - 1) // 2
    freqs = np.linspace(1e-4, bands - 1, bands)[None, :]
    wpos = 2.0 * math.pi * pos[:, None] / n
    z = np.zeros((rows, LANE), np.float32)
    z[:, 0] = t
    z[:, 1:1 + bands] = np.cos(freqs * wpos)
    z[:, 1 + bands:HY_EMB] = -np.sin(freqs * wpos)
    deltas = np.abs(np.linspace(math.log(HY_DECAY_TARGET) / HY_DECAY_LONG_PCT,
                                math.log(HY_DECAY_TARGET) / HY_DECAY_SHORT_PCT, HY_WIDTH))
    deltas = jnp.asarray(deltas[None, :], F32)
    return pl.pallas_call(
        functools.partial(_filter_kernel, n=n),
        out_shape=(jax.ShapeDtypeStruct((2, rows, HY_WIDTH), F32),
                   jax.ShapeDtypeStruct((2, 1, HY_WIDTH), F32)),
        grid=(rows // tile,),
        in_specs=[
            pl.BlockSpec((tile, LANE), lambda i: (i, 0)),
            _const_spec((LANE, LANE)), _const_spec((1, LANE)), _const_spec((1, LANE)),
            _const_spec((LANE, LANE)), _const_spec((1, LANE)), _const_spec((1, LANE)),
            _const_spec((LANE, 4 * HY_WIDTH)), _const_spec((LANE, 4 * HY_WIDTH)),
            _const_spec((1, HY_WIDTH)),
        ],
        out_specs=(pl.BlockSpec((2, tile, HY_WIDTH), lambda i: (0, i, 0)),
                   pl.BlockSpec((2, 1, HY_WIDTH), lambda i: (0, 0, 0))),
        compiler_params=_params("arbitrary"),
        name="hyena_filter",
    )(jnp.asarray(z), lw["hy_w1"], lw["hy_b1"], lw["hy_f1"], lw["hy_w2"], lw["hy_b2"],
      lw["hy_f2"], *_split(lw["hy_w3"]), deltas)


def _dft_tables(blk):
    f = np.arange(blk, dtype=np.float64)[:, None]
    m = np.arange(blk, dtype=np.float64)[None, :]
    theta = math.pi * (2.0 * f + 1.0) * m / (2.0 * blk)
    return np.cos(theta), np.sin(theta)


def _spectra_kernel(g_ref, asum_ref, fwdh_ref, fwdl_ref, k_ref, prev_ref):
    e = pl.program_id(1)
    blk = g_ref.shape[1]
    s = _dot3((fwdh_ref[...], fwdl_ref[...]), _split(g_ref[0]))
    s = s / asum_ref[0]

    @pl.when(e > 0)
    def _():
        prev = prev_ref[...]
        f = lax.broadcasted_iota(jnp.int32, (blk, HY_WIDTH), 0)
        sgn = jnp.where(f % 2 == 0, 1.0, -1.0).astype(F32)
        k_ref[0, 0, 0] = s[:blk] - sgn * prev[blk:]
        k_ref[0, 0, 1] = s[blk:] + sgn * prev[:blk]

    prev_ref[...] = s


def _hyena_spectra(g, asum, blk):
    rows = g.shape[1]
    nseg = rows // blk
    cos, sin = _dft_tables(blk)
    fwd = jnp.asarray(np.concatenate([cos, -sin], axis=0), F32)
    return pl.pallas_call(
        _spectra_kernel,
        out_shape=jax.ShapeDtypeStruct((2, nseg - 1, 2, blk, HY_WIDTH), F32),
        grid=(2, nseg),
        in_specs=[
            pl.BlockSpec((1, blk, HY_WIDTH), lambda o, e: (o, e, 0)),
            pl.BlockSpec((1, 1, HY_WIDTH), lambda o, e: (o, 0, 0)),
            _const_spec((2 * blk, blk)), _const_spec((2 * blk, blk)),
        ],
        out_specs=pl.BlockSpec((1, 1, 2, blk, HY_WIDTH),
                               lambda o, e: (o, jnp.maximum(e - 1, 0), 0, 0, 0)),
        scratch_shapes=[pltpu.VMEM((2 * blk, HY_WIDTH), F32)],
        compiler_params=_params("arbitrary", "arbitrary"),
        name="hyena_spectra",
    )(g, asum, *_split(fwd))


def _short_conv_rows(src_ref, rows, w_ref, b_ref):
    r0, r1 = rows
    n = src_ref.shape[1]
    cur = src_ref[0, r0:r1, :].astype(F32)
    row = lax.broadcasted_iota(jnp.int32, cur.shape, 0)
    edge = jnp.zeros((1, cur.shape[1]), F32)
    before = src_ref[0, r0 - EDGE_ROWS:r0, :].astype(F32)[EDGE_ROWS - 1:, :] if r0 > 0 else edge
    after = src_ref[0, r1:r1 + EDGE_ROWS, :].astype(F32)[:1, :] if r1 < n else edge
    prev = jnp.where(row == 0, before, _shift_rows(cur, 1))
    nxt = jnp.where(row == r1 - r0 - 1, after, _shift_rows(cur, -1))
    return prev * w_ref[0:1, :] + cur * w_ref[1:2, :] + nxt * w_ref[2:3, :] + b_ref[...]


def _conv_kernel(u_ref, gate_ref, uw_ref, ub_ref, gw_ref, gb_ref, bias_ref, k_ref, fwd_ref,
                 inv_ref, o_ref, u32_ref, u16_ref, acc_ref, uf0_ref, uf1_ref, yf0_ref, yf1_ref,
                 *, conv_input):
    n = u_ref.shape[1]
    nf, blk = fwd_ref.shape[0], fwd_ref.shape[2]
    nb = n // blk
    fb = fwd_ref.shape[1] // 2
    uf_refs, yf_refs = (uf0_ref, uf1_ref), (yf0_ref, yf1_ref)
    step_id = pl.program_id(0) + nf + 2

    def forward(f):
        for j in range(nb):
            uf_refs[f % 2][j] = _bdot(fwd_ref[f], u16_ref[j * blk:(j + 1) * blk, :])

    def multiply(f):
        uf_ref, yf_ref = uf_refs[f % 2], yf_refs[f % 2]
        for c in range(fb // FREQ_ROWS):
            rs = slice(c * FREQ_ROWS, (c + 1) * FREQ_ROWS)
            rs_im = slice(fb + c * FREQ_ROWS, fb + (c + 1) * FREQ_ROWS)
            ks = slice(f * fb + c * FREQ_ROWS, f * fb + (c + 1) * FREQ_ROWS)
            for i in range(nb):
                yr = jnp.zeros((FREQ_ROWS, HY_WIDTH), F32)
                yi = jnp.zeros((FREQ_ROWS, HY_WIDTH), F32)
                for j in range(nb):
                    d = i - j + nb - 1
                    kr = k_ref[0, d, 0, ks, :]
                    ki = k_ref[0, d, 1, ks, :]
                    ur = uf_ref[j, rs, :]
                    ui = uf_ref[j, rs_im, :]
                    yr = yr + (kr * ur - ki * ui)
                    yi = yi + (kr * ui + ki * ur)
                yf_ref[i, rs, :] = yr.astype(BF16)
                yf_ref[i, rs_im, :] = yi.astype(BF16)

    def inverse(f):
        for i in range(nb):
            y = _bdot(inv_ref[f], yf_refs[f % 2][i])
            rows = slice(i * blk, (i + 1) * blk)
            acc_ref[rows, :] = y if f == 0 else acc_ref[rows, :] + y

    def prologue():
        for j in range(nb):
            rows = (j * blk, (j + 1) * blk)
            if conv_input:
                u = _short_conv_rows(u_ref, rows, uw_ref, ub_ref)
            else:
                u = u_ref[0, rows[0]:rows[1], :].astype(F32)
            u32_ref[rows[0]:rows[1], :] = u
            u16_ref[rows[0]:rows[1], :] = u.astype(BF16)
        forward(0)

    def stage(f):
        if f + 1 < nf:
            forward(f + 1)
        if f < nf:
            multiply(f)
        if f >= 1:
            inverse(f - 1)

    def epilogue():
        bias = bias_ref[0]
        for j in range(nb):
            rows = (j * blk, (j + 1) * blk)
            rs = slice(rows[0], rows[1])
            gate = _short_conv_rows(gate_ref, rows, gw_ref, gb_ref)
            conv = acc_ref[rs, :] + u32_ref[rs, :] * bias
            o_ref[0, rs, :] = (gate * conv).astype(o_ref.dtype)

    pl.when(step_id >= 0)(prologue)
    for f in range(nf + 1):
        pl.when(step_id >= f + 1)(functools.partial(stage, f))
    pl.when(step_id >= nf + 2)(epilogue)


def _hyena_conv(u_arr, u_col, gate_arr, gate_col, lw, khat, order, blk, conv_input, out_dtype):
    bsz, n, _ = gate_arr.shape
    nb = n // blk
    fb = min(FREQ_BLOCK, blk)
    nf = blk // fb
    cos, sin = _dft_tables(blk)
    fwd = np.concatenate([cos.reshape(nf, fb, blk), -sin.reshape(nf, fb, blk)], axis=1)
    inv = np.concatenate([cos.T.reshape(blk, nf, fb), -sin.T.reshape(blk, nf, fb)], axis=2)
    inv = np.transpose(inv, (1, 0, 2)) / blk
    cw, cb = lw["hy_conv_w"], lw["hy_conv_b"]
    col = lambda c: (lambda b, c=c: (0, c))
    once = pl.Buffered(1)
    return pl.pallas_call(
        functools.partial(_conv_kernel, conv_input=conv_input),
        out_shape=jax.ShapeDtypeStruct((bsz, n, HY_WIDTH), out_dtype),
        grid=(bsz,),
        in_specs=[
            pl.BlockSpec((1, n, HY_WIDTH), lambda b, c=u_col: (b, 0, c)),
            pl.BlockSpec((1, n, HY_WIDTH), lambda b, c=gate_col: (b, 0, c)),
            pl.BlockSpec((3, HY_WIDTH), col(u_col if conv_input else 0)),
            pl.BlockSpec((1, HY_WIDTH), col(u_col if conv_input else 0)),
            pl.BlockSpec((3, HY_WIDTH), col(gate_col)),
            pl.BlockSpec((1, HY_WIDTH), col(gate_col)),
            pl.BlockSpec((1, 1, HY_WIDTH), lambda b, o=order: (o, 0, 0)),
            pl.BlockSpec((1, 2 * nb - 1, 2, blk, HY_WIDTH), lambda b, o=order: (o, 0, 0, 0, 0),
                         pipeline_mode=once),
            pl.BlockSpec((nf, 2 * fb, blk), lambda b: (0, 0, 0), pipeline_mode=once),
            pl.BlockSpec((nf, blk, 2 * fb), lambda b: (0, 0, 0), pipeline_mode=once),
        ],
        out_specs=pl.BlockSpec((1, n, HY_WIDTH), lambda b: (b, 0, 0)),
        scratch_shapes=[
            pltpu.VMEM((n, HY_WIDTH), F32),
            pltpu.VMEM((n, HY_WIDTH), BF16),
            pltpu.VMEM((n, HY_WIDTH), F32),
            pltpu.VMEM((nb, 2 * fb, HY_WIDTH), F32),
            pltpu.VMEM((nb, 2 * fb, HY_WIDTH), F32),
            pltpu.VMEM((nb, 2 * fb, HY_WIDTH), BF16),
            pltpu.VMEM((nb, 2 * fb, HY_WIDTH), BF16),
        ],
        compiler_params=_params("parallel"),
        name="hyena_conv%d" % order,
    )(u_arr, gate_arr, cw, cb, cw, cb, lw["hy_bias"], khat,
      jnp.asarray(fwd, BF16), jnp.asarray(inv, BF16))


def _hyena(hy_u, lw):
    n = hy_u.shape[1]
    blk = min(CONV_BLOCK, n)
    g, asum = _hyena_filter(n, lw)
    khat = _hyena_spectra(g, asum, blk)
    z = _hyena_conv(hy_u, 0, hy_u, 1, lw, khat, 0, blk, True, BF16)
    return _hyena_conv(z, 0, hy_u, 2, lw, khat, 1, blk, False, F32)


def _merge_kernel(x_ref, attn_ref, pool_ref, hy_ref, g1_ref, sh2_ref, sc2_ref, g2_ref,
                  gout_ref, wout_ref, gmlp_ref, w1_ref, w2_ref, gfin_ref, o_ref, *, final_norm):
    gout = gout_ref[...]
    a = _rms(attn_ref[0].astype(F32), gout[:, :MLA_WIDTH]).astype(BF16)
    p = _rms(pool_ref[0], gout[:, MLA_WIDTH:MLA_WIDTH + POOL_WIDTH]).astype(BF16)
    hh = _rms(hy_ref[0], gout[:, MLA_WIDTH + POOL_WIDTH:]).astype(BF16)
    y = (_bdot(a, wout_ref[0:MLA_WIDTH, :])
         + _bdot(p, wout_ref[MLA_WIDTH:MLA_WIDTH + POOL_WIDTH, :])
         + _bdot(hh, wout_ref[MLA_WIDTH + POOL_WIDTH:, :]))
    x1 = x_ref[0] + g1_ref[0] * y
    h2 = (_rms(x1, gmlp_ref[...]) * (1.0 + sc2_ref[0]) + sh2_ref[0]).astype(BF16)
    y2 = jnp.zeros_like(x1)
    for c in range(D_FF // MLP_CHUNK):
        cs = slice(c * MLP_CHUNK, (c + 1) * MLP_CHUNK)
        hid = jnp.maximum(_bdot(h2, w1_ref[:, cs]), 0.0)
        y2 = y2 + _bdot((hid * hid).astype(BF16), w2_ref[cs, :])
    x2 = x1 + g2_ref[0] * y2
    if final_norm:
        x2 = _rms(x2, gfin_ref[...])
    o_ref[0] = x2


def _merge(x, attn, pool, hy, mods, lw, g_final, final_norm, tile):
    bsz, n, _ = x.shape
    nt = n // tile
    tok = lambda w: pl.BlockSpec((1, tile, w), lambda b, i: (b, i, 0))
    mod_spec = lambda k: pl.BlockSpec((1, 1, D_MODEL), lambda b, i, k=k: (b, 0, k))
    return pl.pallas_call(
        functools.partial(_merge_kernel, final_norm=final_norm),
        out_shape=jax.ShapeDtypeStruct((bsz, n, D_MODEL), F32),
        grid=(bsz, nt),
        in_specs=[
            tok(D_MODEL), tok(MLA_WIDTH), tok(POOL_WIDTH), tok(HY_WIDTH),
            mod_spec(2), mod_spec(3), mod_spec(4), mod_spec(5),
            _const_spec((1, D_MODEL)), _const_spec((D_MODEL, D_MODEL)),
            _const_spec((1, D_MODEL)), _const_spec((D_MODEL, D_FF)),
            _const_spec((D_FF, D_MODEL)), _const_spec((1, D_MODEL)),
        ],
        out_specs=tok(D_MODEL),
        compiler_params=_params("parallel", "parallel"),
        name="merge_mlp",
    )(x, attn, pool, hy, mods, mods, mods, mods, lw["g_out"], lw["w_out"], lw["g_mlp"],
      lw["w_mlp1"], lw["w_mlp2"], g_final)


_ROPE_SWAP = np.concatenate([np.arange(8, 16), np.arange(0, 8), np.arange(24, 32), np.arange(16, 24)])


def _rope_tables(n):
    idx = np.arange(n)
    r = (idx // GRID_W).astype(np.float32)
    c = (idx % GRID_W).astype(np.float32)
    n_freq = HEAD_ROPE // 4
    inv = (ROPE_BASE ** (-np.arange(n_freq, dtype=np.float32) / n_freq)).astype(np.float32)
    ar, ac = r[:, None] * inv, c[:, None] * inv
    cos = np.zeros((n, HEAD_PAD), np.float32)
    sin = np.zeros((n, HEAD_PAD), np.float32)
    cos[:, :HEAD_NOPE] = 1.0
    cos[:, HEAD_NOPE:HEAD_NOPE + HEAD_ROPE] = np.concatenate(
        [np.cos(ar), np.cos(ar), np.cos(ac), np.cos(ac)], axis=1)
    sin[:, HEAD_NOPE:HEAD_NOPE + HEAD_ROPE] = np.concatenate(
        [-np.sin(ar), np.sin(ar), -np.sin(ac), np.sin(ac)], axis=1)
    return jnp.asarray(cos), jnp.asarray(sin)


def _identity_tables(n):
    cos = np.zeros((n, HEAD_PAD), np.float32)
    cos[:, :HEAD_NOPE + HEAD_ROPE] = 1.0
    return jnp.asarray(cos), jnp.zeros((n, HEAD_PAD), F32)


def _layer_weights(l, w):
    w_in = w["w_in"][l]
    kr = w_in[:, COL_KR:COL_Q]
    w_in_p = jnp.concatenate([w_in[:, COL_KV:COL_KR], w_in[:, COL_Q:COL_POOL],
                              w_in[:, COL_POOL:COL_HY], w_in[:, COL_HY:],
                              jnp.zeros((D_MODEL, HEAD_NOPE), F32), kr, kr[:, _ROPE_SWAP]],
                             axis=1).astype(BF16)
    wkv = w["w_kv_up"][l].reshape(KV_RANK, N_HEADS, HEAD_NOPE + HEAD_V)
    wka = jnp.concatenate([wkv[:, :, :HEAD_NOPE], jnp.zeros((KV_RANK, N_HEADS, HEAD_V), F32)],
                          axis=2).reshape(KV_RANK, N_HEADS * HEAD_PAD).astype(BF16)
    wvt = jnp.concatenate([wkv[:, :, HEAD_NOPE:],
                           jnp.zeros((KV_RANK, N_HEADS, V_SLOT - HEAD_V), F32)], axis=2)
    wvt = wvt.reshape(KV_RANK, VT_ROWS).T.astype(BF16)
    wq = w["w_q_up"][l].reshape(Q_RANK, N_HEADS, HEAD_NOPE + HEAD_ROPE)
    wqt = jnp.concatenate([wq.reshape(Q_RANK, -1),
                           wq[:, :, HEAD_NOPE:][:, :, _ROPE_SWAP].reshape(Q_RANK, -1)],
                          axis=1).T.astype(BF16)
    pool_bd = jnp.zeros((POOL_WIDTH, POOL_WIDTH), F32)
    for g in range(len(POOL_WINDOWS)):
        sl = slice(g * POOL_GROUP, (g + 1) * POOL_GROUP)
        pool_bd = pool_bd.at[sl, sl].set(w["pool_w"][l, g])
    pad_rows = lambda m, rows: jnp.zeros((rows, m.shape[1]), F32).at[:m.shape[0]].set(m)
    pad_cols = lambda m, cols: jnp.zeros((m.shape[0], cols), F32).at[:, :m.shape[1]].set(m)
    row = lambda v: v.reshape(1, -1)
    return {
        "g_mix": row(w["g_mix"][l]), "w_in": w_in_p, "g_kv": row(w["g_kv"][l]), "wka": wka,
        "wvt": wvt, "g_q": row(w["g_q"][l]), "wqt": wqt,
        "pool_w": pool_bd.astype(BF16), "pool_scale": row(w["pool_scale"][l]),
        "hy_conv_w": w["hy_conv_w"][l], "hy_conv_b": row(w["hy_conv_b"][l]),
        "hy_w1": pad_cols(pad_rows(w["hy_f_w1"][l], LANE), LANE),
        "hy_b1": pad_cols(row(w["hy_f_b1"][l]), LANE),
        "hy_f1": pad_cols(row(w["hy_f_freq1"][l]), LANE),
        "hy_w2": pad_cols(pad_rows(w["hy_f_w2"][l], LANE), LANE),
        "hy_b2": pad_cols(row(w["hy_f_b2"][l]), LANE),
        "hy_f2": pad_cols(row(w["hy_f_freq2"][l]), LANE),
        "hy_w3": pad_rows(w["hy_f_w3"][l], LANE),
        "hy_bias": w["hy_bias"][l].reshape(2, 1, HY_WIDTH),
        "g_out": row(w["g_out"][l]), "w_out": w["w_out"][l].astype(BF16),
        "g_mlp": row(w["g_mlp"][l]), "w_mlp1": w["w_mlp1"][l].astype(BF16),
        "w_mlp2": w["w_mlp2"][l].astype(BF16),
    }


def kernel(x, c, ctx, c_ctx, w_mod, b_mod, g_mix, g_mlp, w_in, g_q, w_q_up, g_kv, w_kv_up,
           pool_w, pool_scale, hy_conv_w, hy_conv_b, hy_f_w1, hy_f_b1, hy_f_freq1, hy_f_w2,
           hy_f_b2, hy_f_freq2, hy_f_w3, hy_bias, g_out, w_out, w_mlp1, w_mlp2, g_final):
    w = dict(g_mix=g_mix, g_mlp=g_mlp, w_in=w_in, g_q=g_q, w_q_up=w_q_up, g_kv=g_kv,
             w_kv_up=w_kv_up, pool_w=pool_w, pool_scale=pool_scale, hy_conv_w=hy_conv_w,
             hy_conv_b=hy_conv_b, hy_f_w1=hy_f_w1, hy_f_b1=hy_f_b1, hy_f_freq1=hy_f_freq1,
             hy_f_w2=hy_f_w2, hy_f_b2=hy_f_b2, hy_f_freq2=hy_f_freq2, hy_f_w3=hy_f_w3,
             hy_bias=hy_bias, g_out=g_out, w_out=w_out, w_mlp1=w_mlp1, w_mlp2=w_mlp2)
    depth = w_mod.shape[0]
    bsz, n, _ = x.shape
    n_ctx = ctx.shape[1]
    tile = min(TOKEN_TILE, n)
    tile_c = min(TOKEN_TILE, n_ctx)

    rows = -(-(bsz + 1) // 8) * 8
    cond = jnp.zeros((rows, D_MODEL), F32).at[:bsz].set(c).at[bsz].set(c_ctx)
    mods_all = _modulation(cond, w_mod, b_mod)
    cos_l, sin_l = _rope_tables(n)
    cos_c, sin_c = _identity_tables(n_ctx)
    g_fin = g_final.reshape(1, D_MODEL)

    xc = ctx
    for l in range(depth):
        last = l == depth - 1
        lw = _layer_weights(l, w)
        mods = mods_all[l, :bsz].reshape(bsz, 1, 6 * D_MODEL)
        mods_c = jnp.broadcast_to(mods_all[l, bsz].reshape(1, 1, 6 * D_MODEL),
                                  (bsz, 1, 6 * D_MODEL))
        q, k, vt, pool_u, hy_u = _inproj(x, (mods, 0), (mods, 1), lw, cos_l, sin_l, tile)
        qc, kc, vtc, pool_uc, hy_uc = _inproj(xc, (mods_c, 0), (mods_c, 1), lw, cos_c, sin_c,
                                              tile_c)
        attn = _attention(q, [(kc, vtc), (k, vt)])
        pool = _pool(pool_u, lw["pool_w"], lw["pool_scale"])
        hy = _hyena(hy_u, lw)
        x = _merge(x, attn, pool, hy, mods, lw, g_fin, last, tile)
        if not last:
            attn_c = _attention(qc, [(kc, vtc)])
            pool_c = _pool(pool_uc, lw["pool_w"], lw["pool_scale"])
            hy_c = _hyena(hy_uc, lw)
            xc = _merge(xc, attn_c, pool_c, hy_c, mods_c, lw, g_fin, False, tile_c)
    return x
```

```python
import functools
import math

import jax
import jax.numpy as jnp
import numpy as np
from jax import lax
from jax.experimental import pallas as pl
from jax.experimental.pallas import tpu as pltpu

F32 = jnp.float32
BF16 = jnp.bfloat16

D_MODEL = 1024
GRID_W = 64
EPS = 1e-6
N_HEADS = 8
HEAD_V = 64
HEAD_NOPE = 64
HEAD_ROPE = 32
HEAD_PAD = 128
Q_RANK = 256
KV_RANK = 128
MLA_WIDTH = N_HEADS * HEAD_V
V_SLOT = 80
VT_ROWS = N_HEADS * V_SLOT
Q_ROWS = N_HEADS * (HEAD_NOPE + 2 * HEAD_ROPE)
POOL_WIDTH = 256
HY_WIDTH = 256
MLA_SCALE = (HEAD_NOPE + HEAD_ROPE) ** -0.5
Q_SCALE = MLA_SCALE * math.log2(math.e)
ROPE_BASE = 10000.0
POOL_WINDOWS = (2, 4, 8, 16)
POOL_GROUP = POOL_WIDTH // len(POOL_WINDOWS)
POOL_PAD = 8
EDGE_ROWS = 16
HY_EMB = 33
HY_FFN = 64
HY_DECAY_TARGET = 1e-2
HY_DECAY_SHORT_PCT = 0.3
HY_DECAY_LONG_PCT = 1.5
D_FF = 4 * D_MODEL
COL_KV, COL_KR, COL_Q, COL_POOL, COL_HY, N_IN = 0, 128, 160, 416, 672, 1440
P_KV, P_Q, P_POOL, P_HY, P_KR, N_IN_P = 0, 128, 384, 640, 1408, 1536

LANE = 128
VMEM_LIMIT = 56 * 1024 * 1024
ATTN_VMEM_LIMIT = 60 * 1024 * 1024
TOKEN_TILE = 512
Q_TILE = 512
Q_TILES_PER_STEP = 2
CHUNK_UNROLL = 8
CONV_BLOCK = 512
FREQ_BLOCK = 128
FREQ_ROWS = 16
MLP_CHUNK = 1024
HIGHEST = lax.Precision.HIGHEST
HIGH = lax.Precision.HIGH


def _params(*sem):
    return pltpu.CompilerParams(dimension_semantics=sem, vmem_limit_bytes=VMEM_LIMIT)


def _const_spec(shape):
    zeros = (0,) * len(shape)
    return pl.BlockSpec(shape, lambda *_: zeros, pipeline_mode=pl.Buffered(1))


def _rms(x, g):
    return x * lax.rsqrt(jnp.mean(x * x, axis=-1, keepdims=True) + EPS) * g


def _bdot(a, b):
    return jnp.dot(a, b, preferred_element_type=F32)


def _split(x):
    hi = x.astype(BF16)
    return hi, (x - hi.astype(F32)).astype(BF16)


def _dot3(a, b):
    return _bdot(a[0], b[0]) + (_bdot(a[0], b[1]) + _bdot(a[1], b[0]))


def _modulation_kernel(c_ref, w_ref, b_ref, o_ref):
    c = c_ref[...]
    s = c / (1.0 + jnp.exp(-c))
    o_ref[0] = jnp.dot(s, w_ref[0], preferred_element_type=F32, precision=HIGHEST) + b_ref[0]


def _modulation(cond, w_mod, b_mod):
    depth = w_mod.shape[0]
    rows = cond.shape[0]
    return pl.pallas_call(
        _modulation_kernel,
        out_shape=jax.ShapeDtypeStruct((depth, rows, 6 * D_MODEL), F32),
        grid=(depth, 6),
        in_specs=[
            pl.BlockSpec((rows, D_MODEL), lambda l, j: (0, 0)),
            pl.BlockSpec((1, D_MODEL, D_MODEL), lambda l, j: (l, 0, j)),
            pl.BlockSpec((1, 1, D_MODEL), lambda l, j: (l, 0, j)),
        ],
        out_specs=pl.BlockSpec((1, rows, D_MODEL), lambda l, j: (l, 0, j)),
        compiler_params=_params("arbitrary", "arbitrary"),
        name="modulation",
    )(cond, w_mod, b_mod.reshape(depth, 1, 6 * D_MODEL))


def _inproj_kernel(x_ref, sh_ref, sc_ref, gmix_ref, win_ref, gkv_ref, wka_ref, wvt_ref,
                   gq_ref, wqt_ref, cos_ref, sin_ref, cost_ref, sint_ref,
                   qt_ref, k_ref, vt_ref, pool_ref, hy_ref):
    x = x_ref[0]
    h = (_rms(x, gmix_ref[...]) * (1.0 + sc_ref[0]) + sh_ref[0]).astype(BF16)
    proj = _bdot(h, win_ref[...])
    pool_ref[0] = proj[:, P_POOL:P_HY]
    hy_ref[0] = proj[:, P_HY:P_KR].astype(hy_ref.dtype)
    kvn = _rms(proj[:, P_KV:P_Q], gkv_ref[...]).astype(BF16)
    kpad = _bdot(kvn, wka_ref[...])
    vt = lax.dot_general(wvt_ref[...], kvn, (((1,), (1,)), ((), ())),
                         preferred_element_type=F32)
    row = lax.broadcasted_iota(jnp.int32, vt.shape, 0)
    vt = jnp.where(row % V_SLOT >= HEAD_V, 1.0, vt).astype(vt_ref.dtype)
    vt_ref[0, 0] = vt.reshape(N_HEADS, V_SLOT, vt.shape[1])
    qn = _rms(proj[:, P_Q:P_POOL], gq_ref[...]).astype(BF16)
    qt = lax.dot_general(wqt_ref[...], qn, (((1,), (1,)), ((), ())),
                         preferred_element_type=F32)
    krab = proj[:, P_KR:N_IN_P]
    kr = krab * cos_ref[...] + pltpu.roll(krab, HEAD_PAD - HEAD_ROPE, 1) * sin_ref[...]
    cost = cost_ref[...]
    sint = sint_ref[...]
    head_q = HEAD_NOPE + HEAD_ROPE
    pad = jnp.zeros((HEAD_PAD - head_q, qt.shape[1]), F32)
    for hd in range(N_HEADS):
        sl = slice(hd * HEAD_PAD, (hd + 1) * HEAD_PAD)
        k_ref[0, hd] = (kpad[:, sl] + kr).astype(k_ref.dtype)
        a = qt[hd * head_q:(hd + 1) * head_q, :]
        b = qt[N_HEADS * head_q + hd * HEAD_ROPE:N_HEADS * head_q + (hd + 1) * HEAD_ROPE, :]
        rope = a[HEAD_NOPE:, :] * cost + b * sint
        q_h = jnp.concatenate([a[:HEAD_NOPE, :], rope, pad], axis=0) * Q_SCALE
        qt_ref[0, hd] = q_h.astype(qt_ref.dtype)


def _inproj(x, shift, scale, lw, cos_t, sin_t, tile):
    bsz, n, _ = x.shape
    nt = n // tile
    width = N_HEADS * HEAD_PAD
    mod_spec = lambda k: pl.BlockSpec((1, 1, D_MODEL), lambda b, i, k=k: (b, 0, k))
    return pl.pallas_call(
        _inproj_kernel,
        out_shape=(
            jax.ShapeDtypeStruct((bsz, N_HEADS, HEAD_PAD, n), BF16),
            jax.ShapeDtypeStruct((bsz, N_HEADS, n, HEAD_PAD), BF16),
            jax.ShapeDtypeStruct((bsz, nt, N_HEADS, V_SLOT, tile), BF16),
            jax.ShapeDtypeStruct((bsz, n, POOL_WIDTH), F32),
            jax.ShapeDtypeStruct((bsz, n, 3 * HY_WIDTH), BF16),
        ),
        grid=(bsz, nt),
        in_specs=[
            pl.BlockSpec((1, tile, D_MODEL), lambda b, i: (b, i, 0)),
            mod_spec(shift[1]), mod_spec(scale[1]),
            _const_spec((1, D_MODEL)),
            _const_spec((D_MODEL, N_IN_P)),
            _const_spec((1, KV_RANK)),
            _const_spec((KV_RANK, width)),
            _const_spec((VT_ROWS, KV_RANK)),
            _const_spec((1, Q_RANK)),
            _const_spec((Q_ROWS, Q_RANK)),
            pl.BlockSpec((tile, HEAD_PAD), lambda b, i: (i, 0)),
            pl.BlockSpec((tile, HEAD_PAD), lambda b, i: (i, 0)),
            pl.BlockSpec((HEAD_ROPE, tile), lambda b, i: (0, i)),
            pl.BlockSpec((HEAD_ROPE, tile), lambda b, i: (0, i)),
        ],
        out_specs=(
            pl.BlockSpec((1, N_HEADS, HEAD_PAD, tile), lambda b, i: (b, 0, 0, i)),
            pl.BlockSpec((1, N_HEADS, tile, HEAD_PAD), lambda b, i: (b, 0, i, 0)),
            pl.BlockSpec((1, 1, N_HEADS, V_SLOT, tile), lambda b, i: (b, i, 0, 0, 0)),
            pl.BlockSpec((1, tile, POOL_WIDTH), lambda b, i: (b, i, 0)),
            pl.BlockSpec((1, tile, 3 * HY_WIDTH), lambda b, i: (b, i, 0)),
        ),
        compiler_params=_params("parallel", "parallel"),
        name="inproj",
    )(x, shift[0], scale[0], lw["g_mix"], lw["w_in"], lw["g_kv"], lw["wka"], lw["wvt"],
      lw["g_q"], lw["wqt"], cos_t, sin_t,
      cos_t[:, HEAD_NOPE:HEAD_NOPE + HEAD_ROPE].T, sin_t[:, HEAD_NOPE:HEAD_NOPE + HEAD_ROPE].T)


def _attention_kernel(*refs, seg_chunks):
    qt_ref, qt_next_ref = refs[0], refs[1]
    n_seg = len(seg_chunks)
    seg_refs = refs[2:2 + 2 * n_seg]
    o_ref = refs[2 + 2 * n_seg]
    out_ref, m_ref = refs[3 + 2 * n_seg], refs[4 + 2 * n_seg]
    s_bufs = refs[5 + 2 * n_seg:]
    tq = out_ref.shape[2]
    n_tiles = qt_ref.shape[3] // tq
    items = [(qt_ref, t, hd) for t in range(n_tiles) for hd in range(N_HEADS)]
    assert len(items) % 2 == 0
    step, last_step = pl.program_id(1), pl.num_programs(1) - 1
    step_id = pl.program_id(0) + step + len(items)

    def stage(item1, item2, parity1):
        buf1, buf2 = s_bufs[parity1], s_bufs[1 - parity1]
        m2 = m_ref[...] if item2 is not None else None
        m1 = jnp.full((1, tq), -jnp.inf, F32)
        acc2 = jnp.zeros((V_SLOT, tq), F32)
        base = 0
        for si, (n_chunks, chunk) in enumerate(seg_chunks):
            k_ref, vt_ref = seg_refs[2 * si], seg_refs[2 * si + 1]
            for c in range(n_chunks):
                rows = slice(base + c * chunk, base + (c + 1) * chunk)
                if item1 is not None:
                    q1_ref, t1, h1 = item1
                    s = _bdot(k_ref[0, h1, c * chunk:(c + 1) * chunk, :],
                              q1_ref[0, h1, :, t1 * tq:(t1 + 1) * tq])
                    buf1[rows, :] = s
                    m1 = jnp.maximum(m1, jnp.max(s, axis=0, keepdims=True))
                if item2 is not None:
                    p = jnp.exp2(buf2[rows, :] - m2).astype(BF16)
                    acc2 = acc2 + _bdot(vt_ref[0, c, item2[2]], p)
            base += n_chunks * chunk
        if item2 is not None:
            _, t2, h2 = item2
            out_ref[h2] = acc2[:HEAD_V, :] / acc2[HEAD_V:HEAD_V + 1, :]
            if h2 == N_HEADS - 1:
                o_ref[0, t2 * tq:(t2 + 1) * tq, :] = (
                    out_ref[...].reshape(MLA_WIDTH, tq).T.astype(o_ref.dtype))
        if item1 is not None:
            m_ref[...] = m1

    pl.when(step == 0)(functools.partial(stage, items[0], None, 0))
    for s in range(len(items) - 1):
        pl.when(step_id >= s)(functools.partial(stage, items[s + 1], items[s], (s + 1) % 2))
    next_first = (qt_next_ref, 0, 0)
    pl.when(step < last_step)(functools.partial(stage, next_first, items[-1], 0))
    pl.when(step == last_step)(functools.partial(stage, None, items[-1], 0))


def _attention(q, segments):
    bsz, _, _, n = q.shape
    tq = min(Q_TILE, n)
    step_q = min(Q_TILES_PER_STEP * tq, n)
    last = n // step_q - 1
    in_specs = [pl.BlockSpec((1, N_HEADS, HEAD_PAD, step_q), lambda b, i: (b, 0, 0, i)),
                pl.BlockSpec((1, N_HEADS, HEAD_PAD, step_q),
                             lambda b, i: (b, 0, 0, jnp.minimum(i + 1, last)))]
    args = [q, q]
    seg_chunks = []
    for k, vt in segments:
        nk = k.shape[2]
        n_chunks, chunk = vt.shape[1], vt.shape[4]
        assert n_chunks * chunk == nk
        seg_chunks.append((n_chunks, chunk))
        in_specs.append(pl.BlockSpec((1, N_HEADS, nk, HEAD_PAD), lambda b, i: (b, 0, 0, 0)))
        in_specs.append(pl.BlockSpec((1, n_chunks, N_HEADS, V_SLOT, chunk),
                                     lambda b, i: (b, 0, 0, 0, 0)))
        args += [k, vt]
    n_keys = sum(nc * ch for nc, ch in seg_chunks)
    return pl.pallas_call(
        functools.partial(_attention_kernel, seg_chunks=tuple(seg_chunks)),
        out_shape=jax.ShapeDtypeStruct((bsz, n, MLA_WIDTH), BF16),
        grid=(bsz, n // step_q),
        in_specs=in_specs,
        out_specs=pl.BlockSpec((1, step_q, MLA_WIDTH), lambda b, i: (b, i, 0)),
        scratch_shapes=[pltpu.VMEM((N_HEADS, HEAD_V, tq), F32), pltpu.VMEM((1, tq), F32),
                        pltpu.VMEM((n_keys, tq), F32), pltpu.VMEM((n_keys, tq), F32)],
        compiler_params=pltpu.CompilerParams(dimension_semantics=("arbitrary", "arbitrary"),
                                             vmem_limit_bytes=ATTN_VMEM_LIMIT),
        name="attention",
    )(*args)


def _shift_rows(a, k):
    n = a.shape[0]
    return pltpu.roll(a, k % n, 0)


def _pool_kernel(u_ref, inv_ref, w_ref, scale_ref, o_ref, ext_ref):
    n = u_ref.shape[1]
    u = u_ref[0]
    zeros = jnp.zeros((POOL_PAD, POOL_WIDTH), F32)
    ext_ref[0:POOL_PAD, :] = zeros
    ext_ref[POOL_PAD + n:, :] = zeros
    ext_ref[POOL_PAD:POOL_PAD + n, :] = u
    ext = ext_ref[...]
    s2 = ext + _shift_rows(ext, 1)
    s4 = _shift_rows(s2, 1) + _shift_rows(s2, -1)
    s8 = _shift_rows(s4, 2) + _shift_rows(s4, -2)
    s16 = _shift_rows(s8, 4) + _shift_rows(s8, -4)
    lane = lax.broadcasted_iota(jnp.int32, (n, POOL_WIDTH), 1)
    wsum = None
    for g, s in enumerate((s2, s4, s8, s16)):
        sg = s[POOL_PAD:POOL_PAD + n, :]
        wsum = sg if wsum is None else jnp.where(lane >= g * POOL_GROUP, sg, wsum)
    diff = (wsum * inv_ref[...] - u).astype(BF16)
    o_ref[0] = _bdot(diff, w_ref[...]) * scale_ref[...]


def _pool(u, w_bd, scale):
    bsz, n, _ = u.shape
    t = np.arange(n)
    inv = np.concatenate(
        [np.repeat((1.0 / (np.clip(t + w // 2, 0, n) - np.clip(t - w // 2, 0, n)))[:, None],
                   POOL_GROUP, axis=1) for w in POOL_WINDOWS], axis=1).astype(np.float32)
    return pl.pallas_call(
        _pool_kernel,
        out_shape=jax.ShapeDtypeStruct((bsz, n, POOL_WIDTH), F32),
        grid=(bsz,),
        in_specs=[
            pl.BlockSpec((1, n, POOL_WIDTH), lambda b: (b, 0, 0)),
            _const_spec((n, POOL_WIDTH)),
            _const_spec((POOL_WIDTH, POOL_WIDTH)),
            _const_spec((1, POOL_WIDTH)),
        ],
        out_specs=pl.BlockSpec((1, n, POOL_WIDTH), lambda b: (b, 0, 0)),
        scratch_shapes=[pltpu.VMEM((n + 2 * POOL_PAD, POOL_WIDTH), F32)],
        compiler_params=_params("parallel"),
        name="pool_mixer",
    )(u, jnp.asarray(inv), w_bd, scale)


def _filter_kernel(z_ref, w1_ref, b1_ref, f1_ref, w2_ref, b2_ref, f2_ref, w3h_ref, w3l_ref,
                   dl_ref, g_ref, asum_ref, *, n):
    i = pl.program_id(0)
    rows_per = z_ref.shape[0]
    z = z_ref[...]
    h = jnp.sin(f1_ref[...] * (jnp.dot(z, w1_ref[...], preferred_element_type=F32,
                                       precision=HIGHEST) + b1_ref[...]))
    h = jnp.sin(f2_ref[...] * (jnp.dot(h, w2_ref[...], preferred_element_type=F32,
                                       precision=HIGHEST) + b2_ref[...]))
    h = _dot3(_split(h), (w3h_ref[...], w3l_ref[...]))
    r = i * rows_per + lax.broadcasted_iota(jnp.int32, (rows_per, HY_WIDTH), 0)
    backward = r < n
    pos = jnp.where(backward, n - r, r - n)
    t = pos.astype(F32) * (1.0 / (n - 1))
    decay = jnp.exp(-t * dl_ref[...])
    valid = r > 0

    @pl.when(i == 0)
    def _():
        asum_ref[...] = jnp.zeros_like(asum_ref)

    for o in range(2):
        fwd = h[:, (2 * o) * HY_WIDTH:(2 * o + 1) * HY_WIDTH]
        bwd = h[:, (2 * o + 1) * HY_WIDTH:(2 * o + 2) * HY_WIDTH]
        g = jnp.where(valid, jnp.where(backward, bwd, fwd) * decay, 0.0)
        g_ref[o] = g
        asum_ref[o] += jnp.sum(jnp.abs(g), axis=0, keepdims=True)


def _hyena_filter(n, lw):
    rows = 2 * n
    tile = min(rows, 1024)
    r = np.arange(rows)
    pos = np.where(r < n, n - r, r - n).astype(np.float64)
    t = pos / (n - 1)
    bands = (HY_EMB - 1) // 2
    freqs = np.linspace(1e-4, bands - 1, bands)[None, :]
    wpos = 2.0 * math.pi * pos[:, None] / n
    z = np.zeros((rows, LANE), np.float32)
    z[:, 0] = t
    z[:, 1:1 + bands] = np.cos(freqs * wpos)
    z[:, 1 + bands:HY_EMB] = -np.sin(freqs * wpos)
    deltas = np.abs(np.linspace(math.log(HY_DECAY_TARGET) / HY_DECAY_LONG_PCT,
                                math.log(HY_DECAY_TARGET) / HY_DECAY_SHORT_PCT, HY_WIDTH))
    deltas = jnp.asarray(deltas[None, :], F32)
    return pl.pallas_call(
        functools.partial(_filter_kernel, n=n),
        out_shape=(jax.ShapeDtypeStruct((2, rows, HY_WIDTH), F32),
                   jax.ShapeDtypeStruct((2, 1, HY_WIDTH), F32)),
        grid=(rows // tile,),
        in_specs=[
            pl.BlockSpec((tile, LANE), lambda i: (i, 0)),
            _const_spec((LANE, LANE)), _const_spec((1, LANE)), _const_spec((1, LANE)),
            _const_spec((LANE, LANE)), _const_spec((1, LANE)), _const_spec((1, LANE)),
            _const_spec((LANE, 4 * HY_WIDTH)), _const_spec((LANE, 4 * HY_WIDTH)),
            _const_spec((1, HY_WIDTH)),
        ],
        out_specs=(pl.BlockSpec((2, tile, HY_WIDTH), lambda i: (0, i, 0)),
                   pl.BlockSpec((2, 1, HY_WIDTH), lambda i: (0, 0, 0))),
        compiler_params=_params("arbitrary"),
        name="hyena_filter",
    )(jnp.asarray(z), lw["hy_w1"], lw["hy_b1"], lw["hy_f1"], lw["hy_w2"], lw["hy_b2"],
      lw["hy_f2"], *_split(lw["hy_w3"]), deltas)


def _dft_tables(blk):
    f = np.arange(blk, dtype=np.float64)[:, None]
    m = np.arange(blk, dtype=np.float64)[None, :]
    theta = math.pi * (2.0 * f + 1.0) * m / (2.0 * blk)
    return np.cos(theta), np.sin(theta)


def _spectra_kernel(g_ref, asum_ref, fwdh_ref, fwdl_ref, k_ref, prev_ref):
    e = pl.program_id(1)
    blk = g_ref.shape[1]
    s = _dot3((fwdh_ref[...], fwdl_ref[...]), _split(g_ref[0]))
    s = s / asum_ref[0]

    @pl.when(e > 0)
    def _():
        prev = prev_ref[...]
        f = lax.broadcasted_iota(jnp.int32, (blk, HY_WIDTH), 0)
        sgn = jnp.where(f % 2 == 0, 1.0, -1.0).astype(F32)
        k_ref[0, 0, 0] = s[:blk] - sgn * prev[blk:]
        k_ref[0, 0, 1] = s[blk:] + sgn * prev[:blk]

    prev_ref[...] = s


def _hyena_spectra(g, asum, blk):
    rows = g.shape[1]
    nseg = rows // blk
    cos, sin = _dft_tables(blk)
    fwd = jnp.asarray(np.concatenate([cos, -sin], axis=0), F32)
    return pl.pallas_call(
        _spectra_kernel,
        out_shape=jax.ShapeDtypeStruct((2, nseg - 1, 2, blk, HY_WIDTH), F32),
        grid=(2, nseg),
        in_specs=[
            pl.BlockSpec((1, blk, HY_WIDTH), lambda o, e: (o, e, 0)),
            pl.BlockSpec((1, 1, HY_WIDTH), lambda o, e: (o, 0, 0)),
            _const_spec((2 * blk, blk)), _const_spec((2 * blk, blk)),
        ],
        out_specs=pl.BlockSpec((1, 1, 2, blk, HY_WIDTH),
                               lambda o, e: (o, jnp.maximum(e - 1, 0), 0, 0, 0)),
        scratch_shapes=[pltpu.VMEM((2 * blk, HY_WIDTH), F32)],
        compiler_params=_params("arbitrary", "arbitrary"),
        name="hyena_spectra",
    )(g, asum, *_split(fwd))


def _short_conv_rows(src_ref, rows, w_ref, b_ref):
    r0, r1 = rows
    n = src_ref.shape[1]
    cur = src_ref[0, r0:r1, :].astype(F32)
    row = lax.broadcasted_iota(jnp.int32, cur.shape, 0)
    edge = jnp.zeros((1, cur.shape[1]), F32)
    before = src_ref[0, r0 - EDGE_ROWS:r0, :].astype(F32)[EDGE_ROWS - 1:, :] if r0 > 0 else edge
    after = src_ref[0, r1:r1 + EDGE_ROWS, :].astype(F32)[:1, :] if r1 < n else edge
    prev = jnp.where(row == 0, before, _shift_rows(cur, 1))
    nxt = jnp.where(row == r1 - r0 - 1, after, _shift_rows(cur, -1))
    return prev * w_ref[0:1, :] + cur * w_ref[1:2, :] + nxt * w_ref[2:3, :] + b_ref[...]


def _conv_kernel(u_ref, gate_ref, uw_ref, ub_ref, gw_ref, gb_ref, bias_ref, k_ref, fwd_ref,
                 inv_ref, o_ref, u32_ref, u16_ref, acc_ref, uf0_ref, uf1_ref, yf0_ref, yf1_ref,
                 *, conv_input):
    n = u_ref.shape[1]
    nf, blk = fwd_ref.shape[0], fwd_ref.shape[2]
    nb = n // blk
    fb = fwd_ref.shape[1] // 2
    uf_refs, yf_refs = (uf0_ref, uf1_ref), (yf0_ref, yf1_ref)
    step_id = pl.program_id(0) + nf + 2

    def forward(f):
        for j in range(nb):
            uf_refs[f % 2][j] = _bdot(fwd_ref[f], u16_ref[j * blk:(j + 1) * blk, :])

    def multiply(f):
        uf_ref, yf_ref = uf_refs[f % 2], yf_refs[f % 2]
        for c in range(fb // FREQ_ROWS):
            rs = slice(c * FREQ_ROWS, (c + 1) * FREQ_ROWS)
            rs_im = slice(fb + c * FREQ_ROWS, fb + (c + 1) * FREQ_ROWS)
            ks = slice(f * fb + c * FREQ_ROWS, f * fb + (c + 1) * FREQ_ROWS)
            for i in range(nb):
                yr = jnp.zeros((FREQ_ROWS, HY_WIDTH), F32)
                yi = jnp.zeros((FREQ_ROWS, HY_WIDTH), F32)
                for j in range(nb):
                    d = i - j + nb - 1
                    kr = k_ref[0, d, 0, ks, :]
                    ki = k_ref[0, d, 1, ks, :]
                    ur = uf_ref[j, rs, :]
                    ui = uf_ref[j, rs_im, :]
                    yr = yr + (kr * ur - ki * ui)
                    yi = yi + (kr * ui + ki * ur)
                yf_ref[i, rs, :] = yr.astype(BF16)
                yf_ref[i, rs_im, :] = yi.astype(BF16)

    def inverse(f):
        for i in range(nb):
            y = _bdot(inv_ref[f], yf_refs[f % 2][i])
            rows = slice(i * blk, (i + 1) * blk)
            acc_ref[rows, :] = y if f == 0 else acc_ref[rows, :] + y

    def prologue():
        for j in range(nb):
            rows = (j * blk, (j + 1) * blk)
            if conv_input:
                u = _short_conv_rows(u_ref, rows, uw_ref, ub_ref)
            else:
                u = u_ref[0, rows[0]:rows[1], :].astype(F32)
            u32_ref[rows[0]:rows[1], :] = u
            u16_ref[rows[0]:rows[1], :] = u.astype(BF16)
        forward(0)

    def stage(f):
        if f + 1 < nf:
            forward(f + 1)
        if f < nf:
            multiply(f)
        if f >= 1:
            inverse(f - 1)

    def epilogue():
        bias = bias_ref[0]
        for j in range(nb):
            rows = (j * blk, (j + 1) * blk)
            rs = slice(rows[0], rows[1])
            gate = _short_conv_rows(gate_ref, rows, gw_ref, gb_ref)
            conv = acc_ref[rs, :] + u32_ref[rs, :] * bias
            o_ref[0, rs, :] = (gate * conv).astype(o_ref.dtype)

    pl.when(step_id >= 0)(prologue)
    for f in range(nf + 1):
        pl.when(step_id >= f + 1)(functools.partial(stage, f))
    pl.when(step_id >= nf + 2)(epilogue)


def _hyena_conv(u_arr, u_col, gate_arr, gate_col, lw, khat, order, blk, conv_input, out_dtype):
    bsz, n, _ = gate_arr.shape
    nb = n // blk
    fb = min(FREQ_BLOCK, blk)
    nf = blk // fb
    cos, sin = _dft_tables(blk)
    fwd = np.concatenate([cos.reshape(nf, fb, blk), -sin.reshape(nf, fb, blk)], axis=1)
    inv = np.concatenate([cos.T.reshape(blk, nf, fb), -sin.T.reshape(blk, nf, fb)], axis=2)
    inv = np.transpose(inv, (1, 0, 2)) / blk
    cw, cb = lw["hy_conv_w"], lw["hy_conv_b"]
    col = lambda c: (lambda b, c=c: (0, c))
    once = pl.Buffered(1)
    return pl.pallas_call(
        functools.partial(_conv_kernel, conv_input=conv_input),
        out_shape=jax.ShapeDtypeStruct((bsz, n, HY_WIDTH), out_dtype),
        grid=(bsz,),
        in_specs=[
            pl.BlockSpec((1, n, HY_WIDTH), lambda b, c=u_col: (b, 0, c)),
            pl.BlockSpec((1, n, HY_WIDTH), lambda b, c=gate_col: (b, 0, c)),
            pl.BlockSpec((3, HY_WIDTH), col(u_col if conv_input else 0)),
            pl.BlockSpec((1, HY_WIDTH), col(u_col if conv_input else 0)),
            pl.BlockSpec((3, HY_WIDTH), col(gate_col)),
            pl.BlockSpec((1, HY_WIDTH), col(gate_col)),
            pl.BlockSpec((1, 1, HY_WIDTH), lambda b, o=order: (o, 0, 0)),
            pl.BlockSpec((1, 2 * nb - 1, 2, blk, HY_WIDTH), lambda b, o=order: (o, 0, 0, 0, 0),
                         pipeline_mode=once),
            pl.BlockSpec((nf, 2 * fb, blk), lambda b: (0, 0, 0), pipeline_mode=once),
            pl.BlockSpec((nf, blk, 2 * fb), lambda b: (0, 0, 0), pipeline_mode=once),
        ],
        out_specs=pl.BlockSpec((1, n, HY_WIDTH), lambda b: (b, 0, 0)),
        scratch_shapes=[
            pltpu.VMEM((n, HY_WIDTH), F32),
            pltpu.VMEM((n, HY_WIDTH), BF16),
            pltpu.VMEM((n, HY_WIDTH), F32),
            pltpu.VMEM((nb, 2 * fb, HY_WIDTH), F32),
            pltpu.VMEM((nb, 2 * fb, HY_WIDTH), F32),
            pltpu.VMEM((nb, 2 * fb, HY_WIDTH), BF16),
            pltpu.VMEM((nb, 2 * fb, HY_WIDTH), BF16),
        ],
        compiler_params=_params("parallel"),
        name="hyena_conv%d" % order,
    )(u_arr, gate_arr, cw, cb, cw, cb, lw["hy_bias"], khat,
      jnp.asarray(fwd, BF16), jnp.asarray(inv, BF16))


def _hyena(hy_u, lw):
    n = hy_u.shape[1]
    blk = min(CONV_BLOCK, n)
    g, asum = _hyena_filter(n, lw)
    khat = _hyena_spectra(g, asum, blk)
    z = _hyena_conv(hy_u, 0, hy_u, 1, lw, khat, 0, blk, True, BF16)
    return _hyena_conv(z, 0, hy_u, 2, lw, khat, 1, blk, False, F32)


def _merge_kernel(x_ref, attn_ref, pool_ref, hy_ref, g1_ref, sh2_ref, sc2_ref, g2_ref,
                  gout_ref, wout_ref, gmlp_ref, w1_ref, w2_ref, gfin_ref, o_ref, *, final_norm):
    gout = gout_ref[...]
    a = _rms(attn_ref[0].astype(F32), gout[:, :MLA_WIDTH]).astype(BF16)
    p = _rms(pool_ref[0], gout[:, MLA_WIDTH:MLA_WIDTH + POOL_WIDTH]).astype(BF16)
    hh = _rms(hy_ref[0], gout[:, MLA_WIDTH + POOL_WIDTH:]).astype(BF16)
    y = (_bdot(a, wout_ref[0:MLA_WIDTH, :])
         + _bdot(p, wout_ref[MLA_WIDTH:MLA_WIDTH + POOL_WIDTH, :])
         + _bdot(hh, wout_ref[MLA_WIDTH + POOL_WIDTH:, :]))
    x1 = x_ref[0] + g1_ref[0] * y
    h2 = (_rms(x1, gmlp_ref[...]) * (1.0 + sc2_ref[0]) + sh2_ref[0]).astype(BF16)
    y2 = jnp.zeros_like(x1)
    for c in range(D_FF // MLP_CHUNK):
        cs = slice(c * MLP_CHUNK, (c + 1) * MLP_CHUNK)
        hid = jnp.maximum(_bdot(h2, w1_ref[:, cs]), 0.0)
        y2 = y2 + _bdot((hid * hid).astype(BF16), w2_ref[cs, :])
    x2 = x1 + g2_ref[0] * y2
    if final_norm:
        x2 = _rms(x2, gfin_ref[...])
    o_ref[0] = x2


def _merge(x, attn, pool, hy, mods, lw, g_final, final_norm, tile):
    bsz, n, _ = x.shape
    nt = n // tile
    tok = lambda w: pl.BlockSpec((1, tile, w), lambda b, i: (b, i, 0))
    mod_spec = lambda k: pl.BlockSpec((1, 1, D_MODEL), lambda b, i, k=k: (b, 0, k))
    return pl.pallas_call(
        functools.partial(_merge_kernel, final_norm=final_norm),
        out_shape=jax.ShapeDtypeStruct((bsz, n, D_MODEL), F32),
        grid=(bsz, nt),
        in_specs=[
            tok(D_MODEL), tok(MLA_WIDTH), tok(POOL_WIDTH), tok(HY_WIDTH),
            mod_spec(2), mod_spec(3), mod_spec(4), mod_spec(5),
            _const_spec((1, D_MODEL)), _const_spec((D_MODEL, D_MODEL)),
            _const_spec((1, D_MODEL)), _const_spec((D_MODEL, D_FF)),
            _const_spec((D_FF, D_MODEL)), _const_spec((1, D_MODEL)),
        ],
        out_specs=tok(D_MODEL),
        compiler_params=_params("parallel", "parallel"),
        name="merge_mlp",
    )(x, attn, pool, hy, mods, mods, mods, mods, lw["g_out"], lw["w_out"], lw["g_mlp"],
      lw["w_mlp1"], lw["w_mlp2"], g_final)


_ROPE_SWAP = np.concatenate([np.arange(8, 16), np.arange(0, 8), np.arange(24, 32), np.arange(16, 24)])


def _rope_tables(n):
    idx = np.arange(n)
    r = (idx // GRID_W).astype(np.float32)
    c = (idx % GRID_W).astype(np.float32)
    n_freq = HEAD_ROPE // 4
    inv = (ROPE_BASE ** (-np.arange(n_freq, dtype=np.float32) / n_freq)).astype(np.float32)
    ar, ac = r[:, None] * inv, c[:, None] * inv
    cos = np.zeros((n, HEAD_PAD), np.float32)
    sin = np.zeros((n, HEAD_PAD), np.float32)
    cos[:, :HEAD_NOPE] = 1.0
    cos[:, HEAD_NOPE:HEAD_NOPE + HEAD_ROPE] = np.concatenate(
        [np.cos(ar), np.cos(ar), np.cos(ac), np.cos(ac)], axis=1)
    sin[:, HEAD_NOPE:HEAD_NOPE + HEAD_ROPE] = np.concatenate(
        [-np.sin(ar), np.sin(ar), -np.sin(ac), np.sin(ac)], axis=1)
    return jnp.asarray(cos), jnp.asarray(sin)


def _identity_tables(n):
    cos = np.zeros((n, HEAD_PAD), np.float32)
    cos[:, :HEAD_NOPE + HEAD_ROPE] = 1.0
    return jnp.asarray(cos), jnp.zeros((n, HEAD_PAD), F32)


def _layer_weights(l, w):
    w_in = w["w_in"][l]
    kr = w_in[:, COL_KR:COL_Q]
    w_in_p = jnp.concatenate([w_in[:, COL_KV:COL_KR], w_in[:, COL_Q:COL_POOL],
                              w_in[:, COL_POOL:COL_HY], w_in[:, COL_HY:],
                              jnp.zeros((D_MODEL, HEAD_NOPE), F32), kr, kr[:, _ROPE_SWAP]],
                             axis=1).astype(BF16)
    wkv = w["w_kv_up"][l].reshape(KV_RANK, N_HEADS, HEAD_NOPE + HEAD_V)
    wka = jnp.concatenate([wkv[:, :, :HEAD_NOPE], jnp.zeros((KV_RANK, N_HEADS, HEAD_V), F32)],
                          axis=2).reshape(KV_RANK, N_HEADS * HEAD_PAD).astype(BF16)
    wvt = jnp.concatenate([wkv[:, :, HEAD_NOPE:],
                           jnp.zeros((KV_RANK, N_HEADS, V_SLOT - HEAD_V), F32)], axis=2)
    wvt = wvt.reshape(KV_RANK, VT_ROWS).T.astype(BF16)
    wq = w["w_q_up"][l].reshape(Q_RANK, N_HEADS, HEAD_NOPE + HEAD_ROPE)
    wqt = jnp.concatenate([wq.reshape(Q_RANK, -1),
                           wq[:, :, HEAD_NOPE:][:, :, _ROPE_SWAP].reshape(Q_RANK, -1)],
                          axis=1).T.astype(BF16)
    pool_bd = jnp.zeros((POOL_WIDTH, POOL_WIDTH), F32)
    for g in range(len(POOL_WINDOWS)):
        sl = slice(g * POOL_GROUP, (g + 1) * POOL_GROUP)
        pool_bd = pool_bd.at[sl, sl].set(w["pool_w"][l, g])
    pad_rows = lambda m, rows: jnp.zeros((rows, m.shape[1]), F32).at[:m.shape[0]].set(m)
    pad_cols = lambda m, cols: jnp.zeros((m.shape[0], cols), F32).at[:, :m.shape[1]].set(m)
    row = lambda v: v.reshape(1, -1)
    return {
        "g_mix": row(w["g_mix"][l]), "w_in": w_in_p, "g_kv": row(w["g_kv"][l]), "wka": wka,
        "wvt": wvt, "g_q": row(w["g_q"][l]), "wqt": wqt,
        "pool_w": pool_bd.astype(BF16), "pool_scale": row(w["pool_scale"][l]),
        "hy_conv_w": w["hy_conv_w"][l], "hy_conv_b": row(w["hy_conv_b"][l]),
        "hy_w1": pad_cols(pad_rows(w["hy_f_w1"][l], LANE), LANE),
        "hy_b1": pad_cols(row(w["hy_f_b1"][l]), LANE),
        "hy_f1": pad_cols(row(w["hy_f_freq1"][l]), LANE),
        "hy_w2": pad_cols(pad_rows(w["hy_f_w2"][l], LANE), LANE),
        "hy_b2": pad_cols(row(w["hy_f_b2"][l]), LANE),
        "hy_f2": pad_cols(row(w["hy_f_freq2"][l]), LANE),
        "hy_w3": pad_rows(w["hy_f_w3"][l], LANE),
        "hy_bias": w["hy_bias"][l].reshape(2, 1, HY_WIDTH),
        "g_out": row(w["g_out"][l]), "w_out": w["w_out"][l].astype(BF16),
        "g_mlp": row(w["g_mlp"][l]), "w_mlp1": w["w_mlp1"][l].astype(BF16),
        "w_mlp2": w["w_mlp2"][l].astype(BF16),
    }


def kernel(x, c, ctx, c_ctx, w_mod, b_mod, g_mix, g_mlp, w_in, g_q, w_q_up, g_kv, w_kv_up,
           pool_w, pool_scale, hy_conv_w, hy_conv_b, hy_f_w1, hy_f_b1, hy_f_freq1, hy_f_w2,
           hy_f_b2, hy_f_freq2, hy_f_w3, hy_bias, g_out, w_out, w_mlp1, w_mlp2, g_final):
    w = dict(g_mix=g_mix, g_mlp=g_mlp, w_in=w_in, g_q=g_q, w_q_up=w_q_up, g_kv=g_kv,
             w_kv_up=w_kv_up, pool_w=pool_w, pool_scale=pool_scale, hy_conv_w=hy_conv_w,
             hy_conv_b=hy_conv_b, hy_f_w1=hy_f_w1, hy_f_b1=hy_f_b1, hy_f_freq1=hy_f_freq1,
             hy_f_w2=hy_f_w2, hy_f_b2=hy_f_b2, hy_f_freq2=hy_f_freq2, hy_f_w3=hy_f_w3,
             hy_bias=hy_bias, g_out=g_out, w_out=w_out, w_mlp1=w_mlp1, w_mlp2=w_mlp2)
    depth = w_mod.shape[0]
    bsz, n, _ = x.shape
    n_ctx = ctx.shape[1]
    tile = min(TOKEN_TILE, n)
    tile_c = min(TOKEN_TILE, n_ctx)

    rows = -(-(bsz + 1) // 8) * 8
    cond = jnp.zeros((rows, D_MODEL), F32).at[:bsz].set(c).at[bsz].set(c_ctx)
    mods_all = _modulation(cond, w_mod, b_mod)
    cos_l, sin_l = _rope_tables(n)
    cos_c, sin_c = _identity_tables(n_ctx)
    g_fin = g_final.reshape(1, D_MODEL)

    xc = ctx
    for l in range(depth):
        last = l == depth - 1
        lw = _layer_weights(l, w)
        mods = mods_all[l, :bsz].reshape(bsz, 1, 6 * D_MODEL)
        mods_c = jnp.broadcast_to(mods_all[l, bsz].reshape(1, 1, 6 * D_MODEL),
                                  (bsz, 1, 6 * D_MODEL))
        q, k, vt, pool_u, hy_u = _inproj(x, (mods, 0), (mods, 1), lw, cos_l, sin_l, tile)
        qc, kc, vtc, pool_uc, hy_uc = _inproj(xc, (mods_c, 0), (mods_c, 1), lw, cos_c, sin_c,
                                              tile_c)
        attn = _attention(q, [(kc, vtc), (k, vt)])
        pool = _pool(pool_u, lw["pool_w"], lw["pool_scale"])
        hy = _hyena(hy_u, lw)
        x = _merge(x, attn, pool, hy, mods, lw, g_fin, last, tile)
        if not last:
            attn_c = _attention(qc, [(kc, vtc)])
            pool_c = _pool(pool_uc, lw["pool_w"], lw["pool_scale"])
            hy_c = _hyena(hy_uc, lw)
            xc = _merge(xc, attn_c, pool_c, hy_c, mods_c, lw, g_fin, False, tile_c)
    return x
```

```python
import functools
import math

import jax
import jax.numpy as jnp
import numpy as np
from jax import lax
from jax.experimental import pallas as pl
from jax.experimental.pallas import tpu as pltpu

F32 = jnp.float32
BF16 = jnp.bfloat16

D_MODEL = 1024
GRID_W = 64
EPS = 1e-6
N_HEADS = 8
HEAD_V = 64
HEAD_NOPE = 64
HEAD_ROPE = 32
HEAD_PAD = 128
Q_RANK = 256
KV_RANK = 128
MLA_WIDTH = N_HEADS * HEAD_V
V_SLOT = 80
VT_ROWS = N_HEADS * V_SLOT
Q_ROWS = N_HEADS * (HEAD_NOPE + 2 * HEAD_ROPE)
POOL_WIDTH = 256
HY_WIDTH = 256
MLA_SCALE = (HEAD_NOPE + HEAD_ROPE) ** -0.5
Q_SCALE = MLA_SCALE * math.log2(math.e)
ROPE_BASE = 10000.0
POOL_WINDOWS = (2, 4, 8, 16)
POOL_GROUP = POOL_WIDTH // len(POOL_WINDOWS)
POOL_PAD = 8
EDGE_ROWS = 16
HY_EMB = 33
HY_FFN = 64
HY_DECAY_TARGET = 1e-2
HY_DECAY_SHORT_PCT = 0.3
HY_DECAY_LONG_PCT = 1.5
D_FF = 4 * D_MODEL
COL_KV, COL_KR, COL_Q, COL_POOL, COL_HY, N_IN = 0, 128, 160, 416, 672, 1440
P_KV, P_Q, P_POOL, P_HY, P_KR, N_IN_P = 0, 128, 384, 640, 1408, 1536

LANE = 128
VMEM_LIMIT = 56 * 1024 * 1024
ATTN_VMEM_LIMIT = 60 * 1024 * 1024
TOKEN_TILE = 512
Q_TILE = 512
Q_TILES_PER_STEP = 2
QK_CHUNKS = 2
CONV_BLOCK = 512
FREQ_BLOCK = 128
FREQ_ROWS = 16
MLP_CHUNK = 1024
HIGHEST = lax.Precision.HIGHEST
HIGH = lax.Precision.HIGH


def _params(*sem):
    return pltpu.CompilerParams(dimension_semantics=sem, vmem_limit_bytes=VMEM_LIMIT)


def _const_spec(shape):
    zeros = (0,) * len(shape)
    return pl.BlockSpec(shape, lambda *_: zeros, pipeline_mode=pl.Buffered(1))


def _rms(x, g):
    return x * lax.rsqrt(jnp.mean(x * x, axis=-1, keepdims=True) + EPS) * g


def _bdot(a, b):
    return jnp.dot(a, b, preferred_element_type=F32)


def _split(x):
    hi = x.astype(BF16)
    return hi, (x - hi.astype(F32)).astype(BF16)


def _dot3(a, b):
    return _bdot(a[0], b[0]) + (_bdot(a[0], b[1]) + _bdot(a[1], b[0]))


def _modulation_kernel(c_ref, w_ref, b_ref, o_ref):
    c = c_ref[...]
    s = c / (1.0 + jnp.exp(-c))
    o_ref[0] = jnp.dot(s, w_ref[0], preferred_element_type=F32, precision=HIGHEST) + b_ref[0]


def _modulation(cond, w_mod, b_mod):
    depth = w_mod.shape[0]
    rows = cond.shape[0]
    return pl.pallas_call(
        _modulation_kernel,
        out_shape=jax.ShapeDtypeStruct((depth, rows, 6 * D_MODEL), F32),
        grid=(depth, 6),
        in_specs=[
            pl.BlockSpec((rows, D_MODEL), lambda l, j: (0, 0)),
            pl.BlockSpec((1, D_MODEL, D_MODEL), lambda l, j: (l, 0, j)),
            pl.BlockSpec((1, 1, D_MODEL), lambda l, j: (l, 0, j)),
        ],
        out_specs=pl.BlockSpec((1, rows, D_MODEL), lambda l, j: (l, 0, j)),
        compiler_params=_params("arbitrary", "arbitrary"),
        name="modulation",
    )(cond, w_mod, b_mod.reshape(depth, 1, 6 * D_MODEL))


def _inproj_kernel(x_ref, sh_ref, sc_ref, gmix_ref, win_ref, gkv_ref, wka_ref, wvt_ref,
                   gq_ref, wqt_ref, cos_ref, sin_ref, cost_ref, sint_ref,
                   qt_ref, k_ref, vt_ref, pool_ref, hy_ref):
    x = x_ref[0]
    h = (_rms(x, gmix_ref[...]) * (1.0 + sc_ref[0]) + sh_ref[0]).astype(BF16)
    proj = _bdot(h, win_ref[...])
    pool_ref[0] = proj[:, P_POOL:P_HY]
    hy_ref[0] = proj[:, P_HY:P_KR].astype(hy_ref.dtype)
    kvn = _rms(proj[:, P_KV:P_Q], gkv_ref[...]).astype(BF16)
    kpad = _bdot(kvn, wka_ref[...])
    vt = lax.dot_general(wvt_ref[...], kvn, (((1,), (1,)), ((), ())),
                         preferred_element_type=F32)
    row = lax.broadcasted_iota(jnp.int32, vt.shape, 0)
    vt = jnp.where(row % V_SLOT >= HEAD_V, 1.0, vt).astype(vt_ref.dtype)
    vt_ref[0, 0] = vt.reshape(N_HEADS, V_SLOT, vt.shape[1])
    qn = _rms(proj[:, P_Q:P_POOL], gq_ref[...]).astype(BF16)
    qt = lax.dot_general(wqt_ref[...], qn, (((1,), (1,)), ((), ())),
                         preferred_element_type=F32)
    krab = proj[:, P_KR:N_IN_P]
    kr = krab * cos_ref[...] + pltpu.roll(krab, HEAD_PAD - HEAD_ROPE, 1) * sin_ref[...]
    cost = cost_ref[...]
    sint = sint_ref[...]
    head_q = HEAD_NOPE + HEAD_ROPE
    pad = jnp.zeros((HEAD_PAD - head_q, qt.shape[1]), F32)
    for hd in range(N_HEADS):
        sl = slice(hd * HEAD_PAD, (hd + 1) * HEAD_PAD)
        k_ref[0, hd] = (kpad[:, sl] + kr).astype(k_ref.dtype)
        a = qt[hd * head_q:(hd + 1) * head_q, :]
        b = qt[N_HEADS * head_q + hd * HEAD_ROPE:N_HEADS * head_q + (hd + 1) * HEAD_ROPE, :]
        rope = a[HEAD_NOPE:, :] * cost + b * sint
        q_h = jnp.concatenate([a[:HEAD_NOPE, :], rope, pad], axis=0) * Q_SCALE
        qt_ref[0, hd] = q_h.astype(qt_ref.dtype)


def _inproj(x, shift, scale, lw, cos_t, sin_t, tile):
    bsz, n, _ = x.shape
    nt = n // tile
    width = N_HEADS * HEAD_PAD
    mod_spec = lambda k: pl.BlockSpec((1, 1, D_MODEL), lambda b, i, k=k: (b, 0, k))
    return pl.pallas_call(
        _inproj_kernel,
        out_shape=(
            jax.ShapeDtypeStruct((bsz, N_HEADS, HEAD_PAD, n), BF16),
            jax.ShapeDtypeStruct((bsz, N_HEADS, n, HEAD_PAD), BF16),
            jax.ShapeDtypeStruct((bsz, nt, N_HEADS, V_SLOT, tile), BF16),
            jax.ShapeDtypeStruct((bsz, n, POOL_WIDTH), F32),
            jax.ShapeDtypeStruct((bsz, n, 3 * HY_WIDTH), BF16),
        ),
        grid=(bsz, nt),
        in_specs=[
            pl.BlockSpec((1, tile, D_MODEL), lambda b, i: (b, i, 0)),
            mod_spec(shift[1]), mod_spec(scale[1]),
            _const_spec((1, D_MODEL)),
            _const_spec((D_MODEL, N_IN_P)),
            _const_spec((1, KV_RANK)),
            _const_spec((KV_RANK, width)),
            _const_spec((VT_ROWS, KV_RANK)),
            _const_spec((1, Q_RANK)),
            _const_spec((Q_ROWS, Q_RANK)),
            pl.BlockSpec((tile, HEAD_PAD), lambda b, i: (i, 0)),
            pl.BlockSpec((tile, HEAD_PAD), lambda b, i: (i, 0)),
            pl.BlockSpec((HEAD_ROPE, tile), lambda b, i: (0, i)),
            pl.BlockSpec((HEAD_ROPE, tile), lambda b, i: (0, i)),
        ],
        out_specs=(
            pl.BlockSpec((1, N_HEADS, HEAD_PAD, tile), lambda b, i: (b, 0, 0, i)),
            pl.BlockSpec((1, N_HEADS, tile, HEAD_PAD), lambda b, i: (b, 0, i, 0)),
            pl.BlockSpec((1, 1, N_HEADS, V_SLOT, tile), lambda b, i: (b, i, 0, 0, 0)),
            pl.BlockSpec((1, tile, POOL_WIDTH), lambda b, i: (b, i, 0)),
            pl.BlockSpec((1, tile, 3 * HY_WIDTH), lambda b, i: (b, i, 0)),
        ),
        compiler_params=_params("parallel", "parallel"),
        name="inproj",
    )(x, shift[0], scale[0], lw["g_mix"], lw["w_in"], lw["g_kv"], lw["wka"], lw["wvt"],
      lw["g_q"], lw["wqt"], cos_t, sin_t,
      cos_t[:, HEAD_NOPE:HEAD_NOPE + HEAD_ROPE].T, sin_t[:, HEAD_NOPE:HEAD_NOPE + HEAD_ROPE].T)


def _attention_kernel(*refs, seg_chunks):
    qt_ref, qt_next_ref = refs[0], refs[1]
    n_seg = len(seg_chunks)
    seg_refs = refs[2:2 + 2 * n_seg]
    o_ref = refs[2 + 2 * n_seg]
    out_ref, m_ref = refs[3 + 2 * n_seg], refs[4 + 2 * n_seg]
    s_bufs = refs[5 + 2 * n_seg:]
    tq = out_ref.shape[2]
    n_tiles = qt_ref.shape[3] // tq
    items = [(qt_ref, t, hd) for t in range(n_tiles) for hd in range(N_HEADS)]
    assert len(items) % 2 == 0
    step, last_step = pl.program_id(1), pl.num_programs(1) - 1
    step_id = pl.program_id(0) + step + len(items)

    def stage(item1, item2, parity1):
        buf1, buf2 = s_bufs[parity1], s_bufs[1 - parity1]
        m2 = m_ref[...] if item2 is not None else None
        m1 = jnp.full((1, tq), -jnp.inf, F32)
        acc2 = jnp.zeros((V_SLOT, tq), F32)
        base = 0
        for si, (n_chunks, chunk) in enumerate(seg_chunks):
            k_ref, vt_ref = seg_refs[2 * si], seg_refs[2 * si + 1]
            group = QK_CHUNKS if n_chunks % QK_CHUNKS == 0 else 1
            for c in range(n_chunks):
                rows = slice(base + c * chunk, base + (c + 1) * chunk)
                if item1 is not None and c % group == 0:
                    q1_ref, t1, h1 = item1
                    s = _bdot(k_ref[0, h1, c * chunk:(c + group) * chunk, :],
                              q1_ref[0, h1, :, t1 * tq:(t1 + 1) * tq])
                    buf1[base + c * chunk:base + (c + group) * chunk, :] = s
                    m1 = jnp.maximum(m1, jnp.max(s, axis=0, keepdims=True))
                if item2 is not None:
                    p = jnp.exp2(buf2[rows, :] - m2).astype(BF16)
                    acc2 = acc2 + _bdot(vt_ref[0, c, item2[2]], p)
            base += n_chunks * chunk
        if item2 is not None:
            _, t2, h2 = item2
            out_ref[h2] = acc2[:HEAD_V, :] / acc2[HEAD_V:HEAD_V + 1, :]
            if h2 == N_HEADS - 1:
                o_ref[0, t2 * tq:(t2 + 1) * tq, :] = (
                    out_ref[...].reshape(MLA_WIDTH, tq).T.astype(o_ref.dtype))
        if item1 is not None:
            m_ref[...] = m1

    pl.when(step == 0)(functools.partial(stage, items[0], None, 0))
    for s in range(len(items) - 1):
        pl.when(step_id >= s)(functools.partial(stage, items[s + 1], items[s], (s + 1) % 2))
    next_first = (qt_next_ref, 0, 0)
    pl.when(step < last_step)(functools.partial(stage, next_first, items[-1], 0))
    pl.when(step == last_step)(functools.partial(stage, None, items[-1], 0))


def _attention(q, segments):
    bsz, _, _, n = q.shape
    tq = min(Q_TILE, n)
    step_q = min(Q_TILES_PER_STEP * tq, n)
    last = n // step_q - 1
    in_specs = [pl.BlockSpec((1, N_HEADS, HEAD_PAD, step_q), lambda b, i: (b, 0, 0, i)),
                pl.BlockSpec((1, N_HEADS, HEAD_PAD, step_q),
                             lambda b, i: (b, 0, 0, jnp.minimum(i + 1, last)))]
    args = [q, q]
    seg_chunks = []
    for k, vt in segments:
        nk = k.shape[2]
        n_chunks, chunk = vt.shape[1], vt.shape[4]
        assert n_chunks * chunk == nk
        seg_chunks.append((n_chunks, chunk))
        in_specs.append(pl.BlockSpec((1, N_HEADS, nk, HEAD_PAD), lambda b, i: (b, 0, 0, 0)))
        in_specs.append(pl.BlockSpec((1, n_chunks, N_HEADS, V_SLOT, chunk),
                                     lambda b, i: (b, 0, 0, 0, 0)))
        args += [k, vt]
    n_keys = sum(nc * ch for nc, ch in seg_chunks)
    return pl.pallas_call(
        functools.partial(_attention_kernel, seg_chunks=tuple(seg_chunks)),
        out_shape=jax.ShapeDtypeStruct((bsz, n, MLA_WIDTH), BF16),
        grid=(bsz, n // step_q),
        in_specs=in_specs,
        out_specs=pl.BlockSpec((1, step_q, MLA_WIDTH), lambda b, i: (b, i, 0)),
        scratch_shapes=[pltpu.VMEM((N_HEADS, HEAD_V, tq), F32), pltpu.VMEM((1, tq), F32),
                        pltpu.VMEM((n_keys, tq), F32), pltpu.VMEM((n_keys, tq), F32)],
        compiler_params=pltpu.CompilerParams(dimension_semantics=("arbitrary", "arbitrary"),
                                             vmem_limit_bytes=ATTN_VMEM_LIMIT),
        name="attention",
    )(*args)


def _shift_rows(a, k):
    n = a.shape[0]
    return pltpu.roll(a, k % n, 0)


def _pool_kernel(u_ref, inv_ref, w_ref, scale_ref, o_ref, ext_ref):
    n = u_ref.shape[1]
    u = u_ref[0]
    zeros = jnp.zeros((POOL_PAD, POOL_WIDTH), F32)
    ext_ref[0:POOL_PAD, :] = zeros
    ext_ref[POOL_PAD + n:, :] = zeros
    ext_ref[POOL_PAD:POOL_PAD + n, :] = u
    ext = ext_ref[...]
    s2 = ext + _shift_rows(ext, 1)
    s4 = _shift_rows(s2, 1) + _shift_rows(s2, -1)
    s8 = _shift_rows(s4, 2) + _shift_rows(s4, -2)
    s16 = _shift_rows(s8, 4) + _shift_rows(s8, -4)
    lane = lax.broadcasted_iota(jnp.int32, (n, POOL_WIDTH), 1)
    wsum = None
    for g, s in enumerate((s2, s4, s8, s16)):
        sg = s[POOL_PAD:POOL_PAD + n, :]
        wsum = sg if wsum is None else jnp.where(lane >= g * POOL_GROUP, sg, wsum)
    diff = (wsum * inv_ref[...] - u).astype(BF16)
    o_ref[0] = _bdot(diff, w_ref[...]) * scale_ref[...]


def _pool(u, w_bd, scale):
    bsz, n, _ = u.shape
    t = np.arange(n)
    inv = np.concatenate(
        [np.repeat((1.0 / (np.clip(t + w // 2, 0, n) - np.clip(t - w // 2, 0, n)))[:, None],
                   POOL_GROUP, axis=1) for w in POOL_WINDOWS], axis=1).astype(np.float32)
    return pl.pallas_call(
        _pool_kernel,
        out_shape=jax.ShapeDtypeStruct((bsz, n, POOL_WIDTH), F32),
        grid=(bsz,),
        in_specs=[
            pl.BlockSpec((1, n, POOL_WIDTH), lambda b: (b, 0, 0)),
            _const_spec((n, POOL_WIDTH)),
            _const_spec((POOL_WIDTH, POOL_WIDTH)),
            _const_spec((1, POOL_WIDTH)),
        ],
        out_specs=pl.BlockSpec((1, n, POOL_WIDTH), lambda b: (b, 0, 0)),
        scratch_shapes=[pltpu.VMEM((n + 2 * POOL_PAD, POOL_WIDTH), F32)],
        compiler_params=_params("parallel"),
        name="pool_mixer",
    )(u, jnp.asarray(inv), w_bd, scale)


def _filter_kernel(z_ref, w1_ref, b1_ref, f1_ref, w2_ref, b2_ref, f2_ref, w3h_ref, w3l_ref,
                   dl_ref, g_ref, asum_ref, *, n):
    i = pl.program_id(0)
    rows_per = z_ref.shape[0]
    z = z_ref[...]
    h = jnp.sin(f1_ref[...] * (jnp.dot(z, w1_ref[...], preferred_element_type=F32,
                                       precision=HIGHEST) + b1_ref[...]))
    h = jnp.sin(f2_ref[...] * (jnp.dot(h, w2_ref[...], preferred_element_type=F32,
                                       precision=HIGHEST) + b2_ref[...]))
    h = _dot3(_split(h), (w3h_ref[...], w3l_ref[...]))
    r = i * rows_per + lax.broadcasted_iota(jnp.int32, (rows_per, HY_WIDTH), 0)
    backward = r < n
    pos = jnp.where(backward, n - r, r - n)
    t = pos.astype(F32) * (1.0 / (n - 1))
    decay = jnp.exp(-t * dl_ref[...])
    valid = r > 0

    @pl.when(i == 0)
    def _():
        asum_ref[...] = jnp.zeros_like(asum_ref)

    for o in range(2):
        fwd = h[:, (2 * o) * HY_WIDTH:(2 * o + 1) * HY_WIDTH]
        bwd = h[:, (2 * o + 1) * HY_WIDTH:(2 * o + 2) * HY_WIDTH]
        g = jnp.where(valid, jnp.where(backward, bwd, fwd) * decay, 0.0)
        g_ref[o] = g
        asum_ref[o] += jnp.sum(jnp.abs(g), axis=0, keepdims=True)


def _hyena_filter(n, lw):
    rows = 2 * n
    tile = min(rows, 1024)
    r = np.arange(rows)
    pos = np.where(r < n, n - r, r - n).astype(np.float64)
    t = pos / (n - 1)
    bands = (HY_EMB - 1) // 2
    freqs = np.linspace(1e-4, bands - 1, bands)[None, :]
    wpos = 2.0 * math.pi * pos[:, None] / n
    z = np.zeros((rows, LANE), np.float32)
    z[:, 0] = t
    z[:, 1:1 + bands] = np.cos(freqs * wpos)
    z[:, 1 + bands:HY_EMB] = -np.sin(freqs * wpos)
    deltas = np.abs(np.linspace(math.log(HY_DECAY_TARGET) / HY_DECAY_LONG_PCT,
                                math.log(HY_DECAY_TARGET) / HY_DECAY_SHORT_PCT, HY_WIDTH))
    deltas = jnp.asarray(deltas[None, :], F32)
    return pl.pallas_call(
        functools.partial(_filter_kernel, n=n),
        out_shape=(jax.ShapeDtypeStruct((2, rows, HY_WIDTH), F32),
                   jax.ShapeDtypeStruct((2, 1, HY_WIDTH), F32)),
        grid=(rows // tile,),
        in_specs=[
            pl.BlockSpec((tile, LANE), lambda i: (i, 0)),
            _const_spec((LANE, LANE)), _const_spec((1, LANE)), _const_spec((1, LANE)),
            _const_spec((LANE, LANE)), _const_spec((1, LANE)), _const_spec((1, LANE)),
            _const_spec((LANE, 4 * HY_WIDTH)), _const_spec((LANE, 4 * HY_WIDTH)),
            _const_spec((1, HY_WIDTH)),
        ],
        out_specs=(pl.BlockSpec((2, tile, HY_WIDTH), lambda i: (0, i, 0)),
                   pl.BlockSpec((2, 1, HY_WIDTH), lambda i: (0, 0, 0))),
        compiler_params=_params("arbitrary"),
        name="hyena_filter",
    )(jnp.asarray(z), lw["hy_w1"], lw["hy_b1"], lw["hy_f1"], lw["hy_w2"], lw["hy_b2"],
      lw["hy_f2"], *_split(lw["hy_w3"]), deltas)


def _dft_tables(blk):
    f = np.arange(blk, dtype=np.float64)[:, None]
    m = np.arange(blk, dtype=np.float64)[None, :]
    theta = math.pi * (2.0 * f + 1.0) * m / (2.0 * blk)
    return np.cos(theta), np.sin(theta)


def _spectra_kernel(g_ref, asum_ref, fwdh_ref, fwdl_ref, k_ref, prev_ref):
    e = pl.program_id(1)
    blk = g_ref.shape[1]
    s = _dot3((fwdh_ref[...], fwdl_ref[...]), _split(g_ref[0]))
    s = s / asum_ref[0]

    @pl.when(e > 0)
    def _():
        prev = prev_ref[...]
        f = lax.broadcasted_iota(jnp.int32, (blk, HY_WIDTH), 0)
        sgn = jnp.where(f % 2 == 0, 1.0, -1.0).astype(F32)
        k_ref[0, 0, 0] = s[:blk] - sgn * prev[blk:]
        k_ref[0, 0, 1] = s[blk:] + sgn * prev[:blk]

    prev_ref[...] = s


def _hyena_spectra(g, asum, blk):
    rows = g.shape[1]
    nseg = rows // blk
    cos, sin = _dft_tables(blk)
    fwd = jnp.asarray(np.concatenate([cos, -sin], axis=0), F32)
    return pl.pallas_call(
        _spectra_kernel,
        out_shape=jax.ShapeDtypeStruct((2, nseg - 1, 2, blk, HY_WIDTH), F32),
        grid=(2, nseg),
        in_specs=[
            pl.BlockSpec((1, blk, HY_WIDTH), lambda o, e: (o, e, 0)),
            pl.BlockSpec((1, 1, HY_WIDTH), lambda o, e: (o, 0, 0)),
            _const_spec((2 * blk, blk)), _const_spec((2 * blk, blk)),
        ],
        out_specs=pl.BlockSpec((1, 1, 2, blk, HY_WIDTH),
                               lambda o, e: (o, jnp.maximum(e - 1, 0), 0, 0, 0)),
        scratch_shapes=[pltpu.VMEM((2 * blk, HY_WIDTH), F32)],
        compiler_params=_params("arbitrary", "arbitrary"),
        name="hyena_spectra",
    )(g, asum, *_split(fwd))


def _short_conv_rows(src_ref, rows, w_ref, b_ref):
    r0, r1 = rows
    n = src_ref.shape[1]
    cur = src_ref[0, r0:r1, :].astype(F32)
    row = lax.broadcasted_iota(jnp.int32, cur.shape, 0)
    edge = jnp.zeros((1, cur.shape[1]), F32)
    before = src_ref[0, r0 - EDGE_ROWS:r0, :].astype(F32)[EDGE_ROWS - 1:, :] if r0 > 0 else edge
    after = src_ref[0, r1:r1 + EDGE_ROWS, :].astype(F32)[:1, :] if r1 < n else edge
    prev = jnp.where(row == 0, before, _shift_rows(cur, 1))
    nxt = jnp.where(row == r1 - r0 - 1, after, _shift_rows(cur, -1))
    return prev * w_ref[0:1, :] + cur * w_ref[1:2, :] + nxt * w_ref[2:3, :] + b_ref[...]


def _conv_kernel(u_ref, gate_ref, uw_ref, ub_ref, gw_ref, gb_ref, bias_ref, k_ref, fwd_ref,
                 inv_ref, o_ref, u32_ref, u16_ref, acc_ref, uf0_ref, uf1_ref, yf0_ref, yf1_ref,
                 *, conv_input):
    n = u_ref.shape[1]
    nf, blk = fwd_ref.shape[0], fwd_ref.shape[2]
    nb = n // blk
    fb = fwd_ref.shape[1] // 2
    uf_refs, yf_refs = (uf0_ref, uf1_ref), (yf0_ref, yf1_ref)
    step_id = pl.program_id(0) + nf + 2

    def forward(f):
        for j in range(nb):
            uf_refs[f % 2][j] = _bdot(fwd_ref[f], u16_ref[j * blk:(j + 1) * blk, :])

    def multiply(f):
        uf_ref, yf_ref = uf_refs[f % 2], yf_refs[f % 2]
        for c in range(fb // FREQ_ROWS):
            rs = slice(c * FREQ_ROWS, (c + 1) * FREQ_ROWS)
            rs_im = slice(fb + c * FREQ_ROWS, fb + (c + 1) * FREQ_ROWS)
            ks = slice(f * fb + c * FREQ_ROWS, f * fb + (c + 1) * FREQ_ROWS)
            for i in range(nb):
                yr = jnp.zeros((FREQ_ROWS, HY_WIDTH), F32)
                yi = jnp.zeros((FREQ_ROWS, HY_WIDTH), F32)
                for j in range(nb):
                    d = i - j + nb - 1
                    kr = k_ref[0, d, 0, ks, :]
                    ki = k_ref[0, d, 1, ks, :]
                    ur = uf_ref[j, rs, :]
                    ui = uf_ref[j, rs_im, :]
                    yr = yr + (kr * ur - ki * ui)
                    yi = yi + (kr * ui + ki * ur)
                yf_ref[i, rs, :] = yr.astype(BF16)
                yf_ref[i, rs_im, :] = yi.astype(BF16)

    def inverse(f):
        for i in range(nb):
            y = _bdot(inv_ref[f], yf_refs[f % 2][i])
            rows = slice(i * blk, (i + 1) * blk)
            acc_ref[rows, :] = y if f == 0 else acc_ref[rows, :] + y

    def prologue():
        for j in range(nb):
            rows = (j * blk, (j + 1) * blk)
            if conv_input:
                u = _short_conv_rows(u_ref, rows, uw_ref, ub_ref)
            else:
                u = u_ref[0, rows[0]:rows[1], :].astype(F32)
            u32_ref[rows[0]:rows[1], :] = u
            u16_ref[rows[0]:rows[1], :] = u.astype(BF16)
        forward(0)

    def stage(f):
        if f + 1 < nf:
            forward(f + 1)
        if f < nf:
            multiply(f)
        if f >= 1:
            inverse(f - 1)

    def epilogue():
        bias = bias_ref[0]
        for j in range(nb):
            rows = (j * blk, (j + 1) * blk)
            rs = slice(rows[0], rows[1])
            gate = _short_conv_rows(gate_ref, rows, gw_ref, gb_ref)
            conv = acc_ref[rs, :] + u32_ref[rs, :] * bias
            o_ref[0, rs, :] = (gate * conv).astype(o_ref.dtype)

    pl.when(step_id >= 0)(prologue)
    for f in range(nf + 1):
        pl.when(step_id >= f + 1)(functools.partial(stage, f))
    pl.when(step_id >= nf + 2)(epilogue)


def _hyena_conv(u_arr, u_col, gate_arr, gate_col, lw, khat, order, blk, conv_input, out_dtype):
    bsz, n, _ = gate_arr.shape
    nb = n // blk
    fb = min(FREQ_BLOCK, blk)
    nf = blk // fb
    cos, sin = _dft_tables(blk)
    fwd = np.concatenate([cos.reshape(nf, fb, blk), -sin.reshape(nf, fb, blk)], axis=1)
    inv = np.concatenate([cos.T.reshape(blk, nf, fb), -sin.T.reshape(blk, nf, fb)], axis=2)
    inv = np.transpose(inv, (1, 0, 2)) / blk
    cw, cb = lw["hy_conv_w"], lw["hy_conv_b"]
    col = lambda c: (lambda b, c=c: (0, c))
    once = pl.Buffered(1)
    return pl.pallas_call(
        functools.partial(_conv_kernel, conv_input=conv_input),
        out_shape=jax.ShapeDtypeStruct((bsz, n, HY_WIDTH), out_dtype),
        grid=(bsz,),
        in_specs=[
            pl.BlockSpec((1, n, HY_WIDTH), lambda b, c=u_col: (b, 0, c)),
            pl.BlockSpec((1, n, HY_WIDTH), lambda b, c=gate_col: (b, 0, c)),
            pl.BlockSpec((3, HY_WIDTH), col(u_col if conv_input else 0)),
            pl.BlockSpec((1, HY_WIDTH), col(u_col if conv_input else 0)),
            pl.BlockSpec((3, HY_WIDTH), col(gate_col)),
            pl.BlockSpec((1, HY_WIDTH), col(gate_col)),
            pl.BlockSpec((1, 1, HY_WIDTH), lambda b, o=order: (o, 0, 0)),
            pl.BlockSpec((1, 2 * nb - 1, 2, blk, HY_WIDTH), lambda b, o=order: (o, 0, 0, 0, 0),
                         pipeline_mode=once),
            pl.BlockSpec((nf, 2 * fb, blk), lambda b: (0, 0, 0), pipeline_mode=once),
            pl.BlockSpec((nf, blk, 2 * fb), lambda b: (0, 0, 0), pipeline_mode=once),
        ],
        out_specs=pl.BlockSpec((1, n, HY_WIDTH), lambda b: (b, 0, 0)),
        scratch_shapes=[
            pltpu.VMEM((n, HY_WIDTH), F32),
            pltpu.VMEM((n, HY_WIDTH), BF16),
            pltpu.VMEM((n, HY_WIDTH), F32),
            pltpu.VMEM((nb, 2 * fb, HY_WIDTH), F32),
            pltpu.VMEM((nb, 2 * fb, HY_WIDTH), F32),
            pltpu.VMEM((nb, 2 * fb, HY_WIDTH), BF16),
            pltpu.VMEM((nb, 2 * fb, HY_WIDTH), BF16),
        ],
        compiler_params=_params("parallel"),
        name="hyena_conv%d" % order,
    )(u_arr, gate_arr, cw, cb, cw, cb, lw["hy_bias"], khat,
      jnp.asarray(fwd, BF16), jnp.asarray(inv, BF16))


def _hyena(hy_u, lw):
    n = hy_u.shape[1]
    blk = min(CONV_BLOCK, n)
    g, asum = _hyena_filter(n, lw)
    khat = _hyena_spectra(g, asum, blk)
    z = _hyena_conv(hy_u, 0, hy_u, 1, lw, khat, 0, blk, True, BF16)
    return _hyena_conv(z, 0, hy_u, 2, lw, khat, 1, blk, False, F32)


def _merge_kernel(x_ref, attn_ref, pool_ref, hy_ref, g1_ref, sh2_ref, sc2_ref, g2_ref,
                  gout_ref, wout_ref, gmlp_ref, w1_ref, w2_ref, gfin_ref, o_ref, *, final_norm):
    gout = gout_ref[...]
    a = _rms(attn_ref[0].astype(F32), gout[:, :MLA_WIDTH]).astype(BF16)
    p = _rms(pool_ref[0], gout[:, MLA_WIDTH:MLA_WIDTH + POOL_WIDTH]).astype(BF16)
    hh = _rms(hy_ref[0], gout[:, MLA_WIDTH + POOL_WIDTH:]).astype(BF16)
    y = (_bdot(a, wout_ref[0:MLA_WIDTH, :])
         + _bdot(p, wout_ref[MLA_WIDTH:MLA_WIDTH + POOL_WIDTH, :])
         + _bdot(hh, wout_ref[MLA_WIDTH + POOL_WIDTH:, :]))
    x1 = x_ref[0] + g1_ref[0] * y
    h2 = (_rms(x1, gmlp_ref[...]) * (1.0 + sc2_ref[0]) + sh2_ref[0]).astype(BF16)
    y2 = jnp.zeros_like(x1)
    for c in range(D_FF // MLP_CHUNK):
        cs = slice(c * MLP_CHUNK, (c + 1) * MLP_CHUNK)
        hid = jnp.maximum(_bdot(h2, w1_ref[:, cs]), 0.0)
        y2 = y2 + _bdot((hid * hid).astype(BF16), w2_ref[cs, :])
    x2 = x1 + g2_ref[0] * y2
    if final_norm:
        x2 = _rms(x2, gfin_ref[...])
    o_ref[0] = x2


def _merge(x, attn, pool, hy, mods, lw, g_final, final_norm, tile):
    bsz, n, _ = x.shape
    nt = n // tile
    tok = lambda w: pl.BlockSpec((1, tile, w), lambda b, i: (b, i, 0))
    mod_spec = lambda k: pl.BlockSpec((1, 1, D_MODEL), lambda b, i, k=k: (b, 0, k))
    return pl.pallas_call(
        functools.partial(_merge_kernel, final_norm=final_norm),
        out_shape=jax.ShapeDtypeStruct((bsz, n, D_MODEL), F32),
        grid=(bsz, nt),
        in_specs=[
            tok(D_MODEL), tok(MLA_WIDTH), tok(POOL_WIDTH), tok(HY_WIDTH),
            mod_spec(2), mod_spec(3), mod_spec(4), mod_spec(5),
            _const_spec((1, D_MODEL)), _const_spec((D_MODEL, D_MODEL)),
            _const_spec((1, D_MODEL)), _const_spec((D_MODEL, D_FF)),
            _const_spec((D_FF, D_MODEL)), _const_spec((1, D_MODEL)),
        ],
        out_specs=tok(D_MODEL),
        compiler_params=_params("parallel", "parallel"),
        name="merge_mlp",
    )(x, attn, pool, hy, mods, mods, mods, mods, lw["g_out"], lw["w_out"], lw["g_mlp"],
      lw["w_mlp1"], lw["w_mlp2"], g_final)


_ROPE_SWAP = np.concatenate([np.arange(8, 16), np.arange(0, 8), np.arange(24, 32), np.arange(16, 24)])


def _rope_tables(n):
    idx = np.arange(n)
    r = (idx // GRID_W).astype(np.float32)
    c = (idx % GRID_W).astype(np.float32)
    n_freq = HEAD_ROPE // 4
    inv = (ROPE_BASE ** (-np.arange(n_freq, dtype=np.float32) / n_freq)).astype(np.float32)
    ar, ac = r[:, None] * inv, c[:, None] * inv
    cos = np.zeros((n, HEAD_PAD), np.float32)
    sin = np.zeros((n, HEAD_PAD), np.float32)
    cos[:, :HEAD_NOPE] = 1.0
    cos[:, HEAD_NOPE:HEAD_NOPE + HEAD_ROPE] = np.concatenate(
        [np.cos(ar), np.cos(ar), np.cos(ac), np.cos(ac)], axis=1)
    sin[:, HEAD_NOPE:HEAD_NOPE + HEAD_ROPE] = np.concatenate(
        [-np.sin(ar), np.sin(ar), -np.sin(ac), np.sin(ac)], axis=1)
    return jnp.asarray(cos), jnp.asarray(sin)


def _identity_tables(n):
    cos = np.zeros((n, HEAD_PAD), np.float32)
    cos[:, :HEAD_NOPE + HEAD_ROPE] = 1.0
    return jnp.asarray(cos), jnp.zeros((n, HEAD_PAD), F32)


def _layer_weights(l, w):
    w_in = w["w_in"][l]
    kr = w_in[:, COL_KR:COL_Q]
    w_in_p = jnp.concatenate([w_in[:, COL_KV:COL_KR], w_in[:, COL_Q:COL_POOL],
                              w_in[:, COL_POOL:COL_HY], w_in[:, COL_HY:],
                              jnp.zeros((D_MODEL, HEAD_NOPE), F32), kr, kr[:, _ROPE_SWAP]],
                             axis=1).astype(BF16)
    wkv = w["w_kv_up"][l].reshape(KV_RANK, N_HEADS, HEAD_NOPE + HEAD_V)
    wka = jnp.concatenate([wkv[:, :, :HEAD_NOPE], jnp.zeros((KV_RANK, N_HEADS, HEAD_V), F32)],
                          axis=2).reshape(KV_RANK, N_HEADS * HEAD_PAD).astype(BF16)
    wvt = jnp.concatenate([wkv[:, :, HEAD_NOPE:],
                           jnp.zeros((KV_RANK, N_HEADS, V_SLOT - HEAD_V), F32)], axis=2)
    wvt = wvt.reshape(KV_RANK, VT_ROWS).T.astype(BF16)
    wq = w["w_q_up"][l].reshape(Q_RANK, N_HEADS, HEAD_NOPE + HEAD_ROPE)
    wqt = jnp.concatenate([wq.reshape(Q_RANK, -1),
                           wq[:, :, HEAD_NOPE:][:, :, _ROPE_SWAP].reshape(Q_RANK, -1)],
                          axis=1).T.astype(BF16)
    pool_bd = jnp.zeros((POOL_WIDTH, POOL_WIDTH), F32)
    for g in range(len(POOL_WINDOWS)):
        sl = slice(g * POOL_GROUP, (g + 1) * POOL_GROUP)
        pool_bd = pool_bd.at[sl, sl].set(w["pool_w"][l, g])
    pad_rows = lambda m, rows: jnp.zeros((rows, m.shape[1]), F32).at[:m.shape[0]].set(m)
    pad_cols = lambda m, cols: jnp.zeros((m.shape[0], cols), F32).at[:, :m.shape[1]].set(m)
    row = lambda v: v.reshape(1, -1)
    return {
        "g_mix": row(w["g_mix"][l]), "w_in": w_in_p, "g_kv": row(w["g_kv"][l]), "wka": wka,
        "wvt": wvt, "g_q": row(w["g_q"][l]), "wqt": wqt,
        "pool_w": pool_bd.astype(BF16), "pool_scale": row(w["pool_scale"][l]),
        "hy_conv_w": w["hy_conv_w"][l], "hy_conv_b": row(w["hy_conv_b"][l]),
        "hy_w1": pad_cols(pad_rows(w["hy_f_w1"][l], LANE), LANE),
        "hy_b1": pad_cols(row(w["hy_f_b1"][l]), LANE),
        "hy_f1": pad_cols(row(w["hy_f_freq1"][l]), LANE),
        "hy_w2": pad_cols(pad_rows(w["hy_f_w2"][l], LANE), LANE),
        "hy_b2": pad_cols(row(w["hy_f_b2"][l]), LANE),
        "hy_f2": pad_cols(row(w["hy_f_freq2"][l]), LANE),
        "hy_w3": pad_rows(w["hy_f_w3"][l], LANE),
        "hy_bias": w["hy_bias"][l].reshape(2, 1, HY_WIDTH),
        "g_out": row(w["g_out"][l]), "w_out": w["w_out"][l].astype(BF16),
        "g_mlp": row(w["g_mlp"][l]), "w_mlp1": w["w_mlp1"][l].astype(BF16),
        "w_mlp2": w["w_mlp2"][l].astype(BF16),
    }


def kernel(x, c, ctx, c_ctx, w_mod, b_mod, g_mix, g_mlp, w_in, g_q, w_q_up, g_kv, w_kv_up,
           pool_w, pool_scale, hy_conv_w, hy_conv_b, hy_f_w1, hy_f_b1, hy_f_freq1, hy_f_w2,
           hy_f_b2, hy_f_freq2, hy_f_w3, hy_bias, g_out, w_out, w_mlp1, w_mlp2, g_final):
    w = dict(g_mix=g_mix, g_mlp=g_mlp, w_in=w_in, g_q=g_q, w_q_up=w_q_up, g_kv=g_kv,
             w_kv_up=w_kv_up, pool_w=pool_w, pool_scale=pool_scale, hy_conv_w=hy_conv_w,
             hy_conv_b=hy_conv_b, hy_f_w1=hy_f_w1, hy_f_b1=hy_f_b1, hy_f_freq1=hy_f_freq1,
             hy_f_w2=hy_f_w2, hy_f_b2=hy_f_b2, hy_f_freq2=hy_f_freq2, hy_f_w3=hy_f_w3,
             hy_bias=hy_bias, g_out=g_out, w_out=w_out, w_mlp1=w_mlp1, w_mlp2=w_mlp2)
    depth = w_mod.shape[0]
    bsz, n, _ = x.shape
    n_ctx = ctx.shape[1]
    tile = min(TOKEN_TILE, n)
    tile_c = min(TOKEN_TILE, n_ctx)

    rows = -(-(bsz + 1) // 8) * 8
    cond = jnp.zeros((rows, D_MODEL), F32).at[:bsz].set(c).at[bsz].set(c_ctx)
    mods_all = _modulation(cond, w_mod, b_mod)
    cos_l, sin_l = _rope_tables(n)
    cos_c, sin_c = _identity_tables(n_ctx)
    g_fin = g_final.reshape(1, D_MODEL)

    xc = ctx
    for l in range(depth):
        last = l == depth - 1
        lw = _layer_weights(l, w)
        mods = mods_all[l, :bsz].reshape(bsz, 1, 6 * D_MODEL)
        mods_c = jnp.broadcast_to(mods_all[l, bsz].reshape(1, 1, 6 * D_MODEL),
                                  (bsz, 1, 6 * D_MODEL))
        q, k, vt, pool_u, hy_u = _inproj(x, (mods, 0), (mods, 1), lw, cos_l, sin_l, tile)
        qc, kc, vtc, pool_uc, hy_uc = _inproj(xc, (mods_c, 0), (mods_c, 1), lw, cos_c, sin_c,
                                              tile_c)
        attn = _attention(q, [(kc, vtc), (k, vt)])
        pool = _pool(pool_u, lw["pool_w"], lw["pool_scale"])
        hy = _hyena(hy_u, lw)
        x = _merge(x, attn, pool, hy, mods, lw, g_fin, last, tile)
        if not last:
            attn_c = _attention(qc, [(kc, vtc)])
            pool_c = _pool(pool_uc, lw["pool_w"], lw["pool_scale"])
            hy_c = _hyena(hy_uc, lw)
            xc = _merge(xc, attn_c, pool_c, hy_c, mods_c, lw, g_fin, False, tile_c)
    return x
```

```python
import functools
import math

import jax
import jax.numpy as jnp
import numpy as np
from jax import lax
from jax.experimental import pallas as pl
from jax.experimental.pallas import tpu as pltpu

F32 = jnp.float32
BF16 = jnp.bfloat16

D_MODEL = 1024
GRID_W = 64
EPS = 1e-6
N_HEADS = 8
HEAD_V = 64
HEAD_NOPE = 64
HEAD_ROPE = 32
HEAD_PAD = 128
Q_RANK = 256
KV_RANK = 128
MLA_WIDTH = N_HEADS * HEAD_V
V_SLOT = 80
VT_ROWS = N_HEADS * V_SLOT
Q_ROWS = N_HEADS * (HEAD_NOPE + 2 * HEAD_ROPE)
POOL_WIDTH = 256
HY_WIDTH = 256
MLA_SCALE = (HEAD_NOPE + HEAD_ROPE) ** -0.5
Q_SCALE = MLA_SCALE * math.log2(math.e)
ROPE_BASE = 10000.0
POOL_WINDOWS = (2, 4, 8, 16)
POOL_GROUP = POOL_WIDTH // len(POOL_WINDOWS)
POOL_PAD = 8
EDGE_ROWS = 16
HY_EMB = 33
HY_FFN = 64
HY_DECAY_TARGET = 1e-2
HY_DECAY_SHORT_PCT = 0.3
HY_DECAY_LONG_PCT = 1.5
D_FF = 4 * D_MODEL
COL_KV, COL_KR, COL_Q, COL_POOL, COL_HY, N_IN = 0, 128, 160, 416, 672, 1440
P_KV, P_Q, P_POOL, P_HY, P_KR, N_IN_P = 0, 128, 384, 640, 1408, 1536

LANE = 128
VMEM_LIMIT = 56 * 1024 * 1024
ATTN_VMEM_LIMIT = 60 * 1024 * 1024
TOKEN_TILE = 512
Q_TILE = 512
Q_TILES_PER_STEP = 2
CHUNK_UNROLL = 8
CONV_BLOCK = 512
FREQ_BLOCK = 128
FREQ_ROWS = 16
MLP_CHUNK = 1024
HIGHEST = lax.Precision.HIGHEST
HIGH = lax.Precision.HIGH


def _params(*sem):
    return pltpu.CompilerParams(dimension_semantics=sem, vmem_limit_bytes=VMEM_LIMIT)


def _const_spec(shape):
    zeros = (0,) * len(shape)
    return pl.BlockSpec(shape, lambda *_: zeros, pipeline_mode=pl.Buffered(1))


def _rms(x, g):
    return x * lax.rsqrt(jnp.mean(x * x, axis=-1, keepdims=True) + EPS) * g


def _bdot(a, b):
    return jnp.dot(a, b, preferred_element_type=F32)


def _split(x):
    hi = x.astype(BF16)
    return hi, (x - hi.astype(F32)).astype(BF16)


def _dot3(a, b):
    return _bdot(a[0], b[0]) + (_bdot(a[0], b[1]) + _bdot(a[1], b[0]))


def _modulation_kernel(c_ref, w_ref, b_ref, o_ref):
    c = c_ref[...]
    s = c / (1.0 + jnp.exp(-c))
    o_ref[0] = jnp.dot(s, w_ref[0], preferred_element_type=F32, precision=HIGHEST) + b_ref[0]


def _modulation(cond, w_mod, b_mod):
    depth = w_mod.shape[0]
    rows = cond.shape[0]
    return pl.pallas_call(
        _modulation_kernel,
        out_shape=jax.ShapeDtypeStruct((depth, rows, 6 * D_MODEL), F32),
        grid=(depth, 6),
        in_specs=[
            pl.BlockSpec((rows, D_MODEL), lambda l, j: (0, 0)),
            pl.BlockSpec((1, D_MODEL, D_MODEL), lambda l, j: (l, 0, j)),
            pl.BlockSpec((1, 1, D_MODEL), lambda l, j: (l, 0, j)),
        ],
        out_specs=pl.BlockSpec((1, rows, D_MODEL), lambda l, j: (l, 0, j)),
        compiler_params=_params("arbitrary", "arbitrary"),
        name="modulation",
    )(cond, w_mod, b_mod.reshape(depth, 1, 6 * D_MODEL))


def _inproj_kernel(x_ref, sh_ref, sc_ref, gmix_ref, win_ref, gkv_ref, wka_ref, wvt_ref,
                   gq_ref, wqt_ref, cos_ref, sin_ref, cost_ref, sint_ref,
                   qt_ref, k_ref, vt_ref, pool_ref, hy_ref):
    x = x_ref[0]
    h = (_rms(x, gmix_ref[...]) * (1.0 + sc_ref[0]) + sh_ref[0]).astype(BF16)
    proj = _bdot(h, win_ref[...])
    pool_ref[0] = proj[:, P_POOL:P_HY]
    hy_ref[0] = proj[:, P_HY:P_KR].astype(hy_ref.dtype)
    kvn = _rms(proj[:, P_KV:P_Q], gkv_ref[...]).astype(BF16)
    kpad = _bdot(kvn, wka_ref[...])
    vt = lax.dot_general(wvt_ref[...], kvn, (((1,), (1,)), ((), ())),
                         preferred_element_type=F32)
    row = lax.broadcasted_iota(jnp.int32, vt.shape, 0)
    vt = jnp.where(row % V_SLOT >= HEAD_V, 1.0, vt).astype(vt_ref.dtype)
    vt_ref[0, 0] = vt.reshape(N_HEADS, V_SLOT, vt.shape[1])
    qn = _rms(proj[:, P_Q:P_POOL], gq_ref[...]).astype(BF16)
    qt = lax.dot_general(wqt_ref[...], qn, (((1,), (1,)), ((), ())),
                         preferred_element_type=F32)
    krab = proj[:, P_KR:N_IN_P]
    kr = krab * cos_ref[...] + pltpu.roll(krab, HEAD_PAD - HEAD_ROPE, 1) * sin_ref[...]
    cost = cost_ref[...]
    sint = sint_ref[...]
    head_q = HEAD_NOPE + HEAD_ROPE
    pad = jnp.zeros((HEAD_PAD - head_q, qt.shape[1]), F32)
    for hd in range(N_HEADS):
        sl = slice(hd * HEAD_PAD, (hd + 1) * HEAD_PAD)
        k_ref[0, hd] = (kpad[:, sl] + kr).astype(k_ref.dtype)
        a = qt[hd * head_q:(hd + 1) * head_q, :]
        b = qt[N_HEADS * head_q + hd * HEAD_ROPE:N_HEADS * head_q + (hd + 1) * HEAD_ROPE, :]
        rope = a[HEAD_NOPE:, :] * cost + b * sint
        q_h = jnp.concatenate([a[:HEAD_NOPE, :], rope, pad], axis=0) * Q_SCALE
        qt_ref[0, hd] = q_h.astype(qt_ref.dtype)


def _inproj(x, shift, scale, lw, cos_t, sin_t, tile):
    bsz, n, _ = x.shape
    nt = n // tile
    width = N_HEADS * HEAD_PAD
    mod_spec = lambda k: pl.BlockSpec((1, 1, D_MODEL), lambda b, i, k=k: (b, 0, k))
    return pl.pallas_call(
        _inproj_kernel,
        out_shape=(
            jax.ShapeDtypeStruct((bsz, N_HEADS, HEAD_PAD, n), BF16),
            jax.ShapeDtypeStruct((bsz, N_HEADS, n, HEAD_PAD), BF16),
            jax.ShapeDtypeStruct((bsz, nt, N_HEADS, V_SLOT, tile), BF16),
            jax.ShapeDtypeStruct((bsz, n, POOL_WIDTH), F32),
            jax.ShapeDtypeStruct((bsz, n, 3 * HY_WIDTH), BF16),
        ),
        grid=(bsz, nt),
        in_specs=[
            pl.BlockSpec((1, tile, D_MODEL), lambda b, i: (b, i, 0)),
            mod_spec(shift[1]), mod_spec(scale[1]),
            _const_spec((1, D_MODEL)),
            _const_spec((D_MODEL, N_IN_P)),
            _const_spec((1, KV_RANK)),
            _const_spec((KV_RANK, width)),
            _const_spec((VT_ROWS, KV_RANK)),
            _const_spec((1, Q_RANK)),
            _const_spec((Q_ROWS, Q_RANK)),
            pl.BlockSpec((tile, HEAD_PAD), lambda b, i: (i, 0)),
            pl.BlockSpec((tile, HEAD_PAD), lambda b, i: (i, 0)),
            pl.BlockSpec((HEAD_ROPE, tile), lambda b, i: (0, i)),
            pl.BlockSpec((HEAD_ROPE, tile), lambda b, i: (0, i)),
        ],
        out_specs=(
            pl.BlockSpec((1, N_HEADS, HEAD_PAD, tile), lambda b, i: (b, 0, 0, i)),
            pl.BlockSpec((1, N_HEADS, tile, HEAD_PAD), lambda b, i: (b, 0, i, 0)),
            pl.BlockSpec((1, 1, N_HEADS, V_SLOT, tile), lambda b, i: (b, i, 0, 0, 0)),
            pl.BlockSpec((1, tile, POOL_WIDTH), lambda b, i: (b, i, 0)),
            pl.BlockSpec((1, tile, 3 * HY_WIDTH), lambda b, i: (b, i, 0)),
        ),
        compiler_params=_params("parallel", "parallel"),
        name="inproj",
    )(x, shift[0], scale[0], lw["g_mix"], lw["w_in"], lw["g_kv"], lw["wka"], lw["wvt"],
      lw["g_q"], lw["wqt"], cos_t, sin_t,
      cos_t[:, HEAD_NOPE:HEAD_NOPE + HEAD_ROPE].T, sin_t[:, HEAD_NOPE:HEAD_NOPE + HEAD_ROPE].T)


def _attention_kernel(*refs, seg_chunks):
    qt_ref, qt_next_ref = refs[0], refs[1]
    n_seg = len(seg_chunks)
    seg_refs = refs[2:2 + 2 * n_seg]
    o_ref = refs[2 + 2 * n_seg]
    out_ref, m_ref = refs[3 + 2 * n_seg], refs[4 + 2 * n_seg]
    s_bufs = refs[5 + 2 * n_seg:]
    tq = out_ref.shape[2]
    n_tiles = qt_ref.shape[3] // tq
    items = [(qt_ref, t, hd) for t in range(n_tiles) for hd in range(N_HEADS)]
    assert len(items) % 2 == 0
    step, last_step = pl.program_id(1), pl.num_programs(1) - 1
    step_id = pl.program_id(0) + step + len(items)

    def stage(item1, item2, parity1):
        buf1, buf2 = s_bufs[parity1], s_bufs[1 - parity1]
        m2 = m_ref[...] if item2 is not None else None
        m1 = jnp.full((1, tq), -jnp.inf, F32)
        acc2 = jnp.zeros((V_SLOT, tq), F32)
        base = 0
        for si, (n_chunks, chunk) in enumerate(seg_chunks):
            k_ref, vt_ref = seg_refs[2 * si], seg_refs[2 * si + 1]
            for c in range(n_chunks):
                rows = slice(base + c * chunk, base + (c + 1) * chunk)
                if item1 is not None:
                    q1_ref, t1, h1 = item1
                    s = _bdot(k_ref[0, h1, c * chunk:(c + 1) * chunk, :],
                              q1_ref[0, h1, :, t1 * tq:(t1 + 1) * tq])
                    buf1[rows, :] = s
                    m1 = jnp.maximum(m1, jnp.max(s, axis=0, keepdims=True))
                if item2 is not None:
                    p = jnp.exp2(buf2[rows, :] - m2).astype(BF16)
                    acc2 = acc2 + _bdot(vt_ref[0, c, item2[2]], p)
            base += n_chunks * chunk
        if item2 is not None:
            _, t2, h2 = item2
            out_ref[h2] = acc2[:HEAD_V, :] / acc2[HEAD_V:HEAD_V + 1, :]
            if h2 == N_HEADS - 1:
                o_ref[0, t2 * tq:(t2 + 1) * tq, :] = (
                    out_ref[...].reshape(MLA_WIDTH, tq).T.astype(o_ref.dtype))
        if item1 is not None:
            m_ref[...] = m1

    pl.when(step == 0)(functools.partial(stage, items[0], None, 0))
    for s in range(len(items) - 1):
        pl.when(step_id >= s)(functools.partial(stage, items[s + 1], items[s], (s + 1) % 2))
    next_first = (qt_next_ref, 0, 0)
    pl.when(step < last_step)(functools.partial(stage, next_first, items[-1], 0))
    pl.when(step == last_step)(functools.partial(stage, None, items[-1], 0))


def _attention(q, segments):
    bsz, _, _, n = q.shape
    tq = min(Q_TILE, n)
    step_q = min(Q_TILES_PER_STEP * tq, n)
    last = n // step_q - 1
    in_specs = [pl.BlockSpec((1, N_HEADS, HEAD_PAD, step_q), lambda b, i: (b, 0, 0, i)),
                pl.BlockSpec((1, N_HEADS, HEAD_PAD, step_q),
                             lambda b, i: (b, 0, 0, jnp.minimum(i + 1, last)))]
    args = [q, q]
    seg_chunks = []
    for k, vt in segments:
        nk = k.shape[2]
        n_chunks, chunk = vt.shape[1], vt.shape[4]
        assert n_chunks * chunk == nk
        seg_chunks.append((n_chunks, chunk))
        in_specs.append(pl.BlockSpec((1, N_HEADS, nk, HEAD_PAD), lambda b, i: (b, 0, 0, 0)))
        in_specs.append(pl.BlockSpec((1, n_chunks, N_HEADS, V_SLOT, chunk),
                                     lambda b, i: (b, 0, 0, 0, 0)))
        args += [k, vt]
    n_keys = sum(nc * ch for nc, ch in seg_chunks)
    return pl.pallas_call(
        functools.partial(_attention_kernel, seg_chunks=tuple(seg_chunks)),
        out_shape=jax.ShapeDtypeStruct((bsz, n, MLA_WIDTH), BF16),
        grid=(bsz, n // step_q),
        in_specs=in_specs,
        out_specs=pl.BlockSpec((1, step_q, MLA_WIDTH), lambda b, i: (b, i, 0)),
        scratch_shapes=[pltpu.VMEM((N_HEADS, HEAD_V, tq), F32), pltpu.VMEM((1, tq), F32),
                        pltpu.VMEM((n_keys, tq), F32), pltpu.VMEM((n_keys, tq), F32)],
        compiler_params=pltpu.CompilerParams(dimension_semantics=("arbitrary", "arbitrary"),
                                             vmem_limit_bytes=ATTN_VMEM_LIMIT),
        name="attention",
    )(*args)


def _shift_rows(a, k):
    n = a.shape[0]
    return pltpu.roll(a, k % n, 0)


def _pool_kernel(u_ref, inv_ref, w_ref, scale_ref, o_ref, ext_ref):
    n = u_ref.shape[1]
    u = u_ref[0]
    zeros = jnp.zeros((POOL_PAD, POOL_WIDTH), F32)
    ext_ref[0:POOL_PAD, :] = zeros
    ext_ref[POOL_PAD + n:, :] = zeros
    ext_ref[POOL_PAD:POOL_PAD + n, :] = u
    ext = ext_ref[...]
    s2 = ext + _shift_rows(ext, 1)
    s4 = _shift_rows(s2, 1) + _shift_rows(s2, -1)
    s8 = _shift_rows(s4, 2) + _shift_rows(s4, -2)
    s16 = _shift_rows(s8, 4) + _shift_rows(s8, -4)
    lane = lax.broadcasted_iota(jnp.int32, (n, POOL_WIDTH), 1)
    wsum = None
    for g, s in enumerate((s2, s4, s8, s16)):
        sg = s[POOL_PAD:POOL_PAD + n, :]
        wsum = sg if wsum is None else jnp.where(lane >= g * POOL_GROUP, sg, wsum)
    diff = (wsum * inv_ref[...] - u).astype(BF16)
    o_ref[0] = _bdot(diff, w_ref[...]) * scale_ref[...]


def _pool(u, w_bd, scale):
    bsz, n, _ = u.shape
    t = np.arange(n)
    inv = np.concatenate(
        [np.repeat((1.0 / (np.clip(t + w // 2, 0, n) - np.clip(t - w // 2, 0, n)))[:, None],
                   POOL_GROUP, axis=1) for w in POOL_WINDOWS], axis=1).astype(np.float32)
    return pl.pallas_call(
        _pool_kernel,
        out_shape=jax.ShapeDtypeStruct((bsz, n, POOL_WIDTH), F32),
        grid=(bsz,),
        in_specs=[
            pl.BlockSpec((1, n, POOL_WIDTH), lambda b: (b, 0, 0)),
            _const_spec((n, POOL_WIDTH)),
            _const_spec((POOL_WIDTH, POOL_WIDTH)),
            _const_spec((1, POOL_WIDTH)),
        ],
        out_specs=pl.BlockSpec((1, n, POOL_WIDTH), lambda b: (b, 0, 0)),
        scratch_shapes=[pltpu.VMEM((n + 2 * POOL_PAD, POOL_WIDTH), F32)],
        compiler_params=_params("parallel"),
        name="pool_mixer",
    )(u, jnp.asarray(inv), w_bd, scale)


def _filter_kernel(z_ref, w1_ref, b1_ref, f1_ref, w2_ref, b2_ref, f2_ref, w3h_ref, w3l_ref,
                   dl_ref, g_ref, asum_ref, *, n):
    i = pl.program_id(0)
    rows_per = z_ref.shape[0]
    z = z_ref[...]
    h = jnp.sin(f1_ref[...] * (jnp.dot(z, w1_ref[...], preferred_element_type=F32,
                                       precision=HIGHEST) + b1_ref[...]))
    h = jnp.sin(f2_ref[...] * (jnp.dot(h, w2_ref[...], preferred_element_type=F32,
                                       precision=HIGHEST) + b2_ref[...]))
    h = _dot3(_split(h), (w3h_ref[...], w3l_ref[...]))
    r = i * rows_per + lax.broadcasted_iota(jnp.int32, (rows_per, HY_WIDTH), 0)
    backward = r < n
    pos = jnp.where(backward, n - r, r - n)
    t = pos.astype(F32) * (1.0 / (n - 1))
    decay = jnp.exp(-t * dl_ref[...])
    valid = r > 0

    @pl.when(i == 0)
    def _():
        asum_ref[...] = jnp.zeros_like(asum_ref)

    for o in range(2):
        fwd = h[:, (2 * o) * HY_WIDTH:(2 * o + 1) * HY_WIDTH]
        bwd = h[:, (2 * o + 1) * HY_WIDTH:(2 * o + 2) * HY_WIDTH]
        g = jnp.where(valid, jnp.where(backward, bwd, fwd) * decay, 0.0)
        g_ref[o] = g
        asum_ref[o] += jnp.sum(jnp.abs(g), axis=0, keepdims=True)


def _hyena_filter(n, lw):
    rows = 2 * n
    tile = min(rows, 1024)
    r = np.arange(rows)
    pos = np.where(r < n, n - r, r - n).astype(np.float64)
    t = pos / (n - 1)
    bands = (HY_EMB - 1) // 2
    freqs = np.linspace(1e-4, bands - 1, bands)[None, :]
    wpos = 2.0 * math.pi * pos[:, None] / n
    z = np.zeros((rows, LANE), np.float32)
    z[:, 0] = t
    z[:, 1:1 + bands] = np.cos(freqs * wpos)
    z[:, 1 + bands:HY_EMB] = -np.sin(freqs * wpos)
    deltas = np.abs(np.linspace(math.log(HY_DECAY_TARGET) / HY_DECAY_LONG_PCT,
                                math.log(HY_DECAY_TARGET) / HY_DECAY_SHORT_PCT, HY_WIDTH))
    deltas = jnp.asarray(deltas[None, :], F32)
    return pl.pallas_call(
        functools.partial(_filter_kernel, n=n),
        out_shape=(jax.ShapeDtypeStruct((2, rows, HY_WIDTH), F32),
                   jax.ShapeDtypeStruct((2, 1, HY_WIDTH), F32)),
        grid=(rows // tile,),
        in_specs=[
            pl.BlockSpec((tile, LANE), lambda i: (i, 0)),
            _const_spec((LANE, LANE)), _const_spec((1, LANE)), _const_spec((1, LANE)),
            _const_spec((LANE, LANE)), _const_spec((1, LANE)), _const_spec((1, LANE)),
            _const_spec((LANE, 4 * HY_WIDTH)), _const_spec((LANE, 4 * HY_WIDTH)),
            _const_spec((1, HY_WIDTH)),
        ],
        out_specs=(pl.BlockSpec((2, tile, HY_WIDTH), lambda i: (0, i, 0)),
                   pl.BlockSpec((2, 1, HY_WIDTH), lambda i: (0, 0, 0))),
        compiler_params=_params("arbitrary"),
        name="hyena_filter",
    )(jnp.asarray(z), lw["hy_w1"], lw["hy_b1"], lw["hy_f1"], lw["hy_w2"], lw["hy_b2"],
      lw["hy_f2"], *_split(lw["hy_w3"]), deltas)


def _dft_tables(blk):
    f = np.arange(blk, dtype=np.float64)[:, None]
    m = np.arange(blk, dtype=np.float64)[None, :]
    theta = math.pi * (2.0 * f + 1.0) * m / (2.0 * blk)
    return np.cos(theta), np.sin(theta)


def _spectra_kernel(g_ref, asum_ref, fwdh_ref, fwdl_ref, k_ref, prev_ref):
    e = pl.program_id(1)
    blk = g_ref.shape[1]
    s = _dot3((fwdh_ref[...], fwdl_ref[...]), _split(g_ref[0]))
    s = s / asum_ref[0]

    @pl.when(e > 0)
    def _():
        prev = prev_ref[...]
        f = lax.broadcasted_iota(jnp.int32, (blk, HY_WIDTH), 0)
        sgn = jnp.where(f % 2 == 0, 1.0, -1.0).astype(F32)
        k_ref[0, 0, 0] = (s[:blk] - sgn * prev[blk:]).astype(k_ref.dtype)
        k_ref[0, 0, 1] = (s[blk:] + sgn * prev[:blk]).astype(k_ref.dtype)

    prev_ref[...] = s


def _hyena_spectra(g, asum, blk):
    rows = g.shape[1]
    nseg = rows // blk
    cos, sin = _dft_tables(blk)
    fwd = jnp.asarray(np.concatenate([cos, -sin], axis=0), F32)
    return pl.pallas_call(
        _spectra_kernel,
        out_shape=jax.ShapeDtypeStruct((2, nseg - 1, 2, blk, HY_WIDTH), BF16),
        grid=(2, nseg),
        in_specs=[
            pl.BlockSpec((1, blk, HY_WIDTH), lambda o, e: (o, e, 0)),
            pl.BlockSpec((1, 1, HY_WIDTH), lambda o, e: (o, 0, 0)),
            _const_spec((2 * blk, blk)), _const_spec((2 * blk, blk)),
        ],
        out_specs=pl.BlockSpec((1, 1, 2, blk, HY_WIDTH),
                               lambda o, e: (o, jnp.maximum(e - 1, 0), 0, 0, 0)),
        scratch_shapes=[pltpu.VMEM((2 * blk, HY_WIDTH), F32)],
        compiler_params=_params("arbitrary", "arbitrary"),
        name="hyena_spectra",
    )(g, asum, *_split(fwd))


def _short_conv_rows(src_ref, rows, w_ref, b_ref):
    r0, r1 = rows
    n = src_ref.shape[1]
    cur = src_ref[0, r0:r1, :].astype(F32)
    row = lax.broadcasted_iota(jnp.int32, cur.shape, 0)
    edge = jnp.zeros((1, cur.shape[1]), F32)
    before = src_ref[0, r0 - EDGE_ROWS:r0, :].astype(F32)[EDGE_ROWS - 1:, :] if r0 > 0 else edge
    after = src_ref[0, r1:r1 + EDGE_ROWS, :].astype(F32)[:1, :] if r1 < n else edge
    prev = jnp.where(row == 0, before, _shift_rows(cur, 1))
    nxt = jnp.where(row == r1 - r0 - 1, after, _shift_rows(cur, -1))
    return prev * w_ref[0:1, :] + cur * w_ref[1:2, :] + nxt * w_ref[2:3, :] + b_ref[...]


def _conv_kernel(u_ref, gate_ref, uw_ref, ub_ref, gw_ref, gb_ref, bias_ref, k_ref, fwd_ref,
                 inv_ref, o_ref, u32_ref, u16_ref, acc_ref, uf0_ref, uf1_ref, yf0_ref, yf1_ref,
                 *, conv_input):
    n = u_ref.shape[1]
    nf, blk = fwd_ref.shape[0], fwd_ref.shape[2]
    nb = n // blk
    fb = fwd_ref.shape[1] // 2
    uf_refs, yf_refs = (uf0_ref, uf1_ref), (yf0_ref, yf1_ref)
    step_id = pl.program_id(0) + nf + 2

    def forward(f):
        for j in range(nb):
            uf_refs[f % 2][j] = _bdot(fwd_ref[f], u16_ref[j * blk:(j + 1) * blk, :]).astype(BF16)

    def multiply(f):
        uf_ref, yf_ref = uf_refs[f % 2], yf_refs[f % 2]
        for c in range(fb // FREQ_ROWS):
            rs = slice(c * FREQ_ROWS, (c + 1) * FREQ_ROWS)
            rs_im = slice(fb + c * FREQ_ROWS, fb + (c + 1) * FREQ_ROWS)
            ks = slice(f * fb + c * FREQ_ROWS, f * fb + (c + 1) * FREQ_ROWS)
            for i in range(nb):
                yr = jnp.zeros((FREQ_ROWS, HY_WIDTH), BF16)
                yi = jnp.zeros((FREQ_ROWS, HY_WIDTH), BF16)
                for j in range(nb):
                    d = i - j + nb - 1
                    kr = k_ref[0, d, 0, ks, :]
                    ki = k_ref[0, d, 1, ks, :]
                    ur = uf_ref[j, rs, :]
                    ui = uf_ref[j, rs_im, :]
                    yr = yr + (kr * ur - ki * ui)
                    yi = yi + (kr * ui + ki * ur)
                yf_ref[i, rs, :] = yr
                yf_ref[i, rs_im, :] = yi

    def inverse(f):
        for i in range(nb):
            y = _bdot(inv_ref[f], yf_refs[f % 2][i])
            rows = slice(i * blk, (i + 1) * blk)
            acc_ref[rows, :] = y if f == 0 else acc_ref[rows, :] + y

    def prologue():
        for j in range(nb):
            rows = (j * blk, (j + 1) * blk)
            if conv_input:
                u = _short_conv_rows(u_ref, rows, uw_ref, ub_ref)
            else:
                u = u_ref[0, rows[0]:rows[1], :].astype(F32)
            u32_ref[rows[0]:rows[1], :] = u
            u16_ref[rows[0]:rows[1], :] = u.astype(BF16)
        forward(0)

    def stage(f):
        if f + 1 < nf:
            forward(f + 1)
        if f < nf:
            multiply(f)
        if f >= 1:
            inverse(f - 1)

    def epilogue():
        bias = bias_ref[0]
        for j in range(nb):
            rows = (j * blk, (j + 1) * blk)
            rs = slice(rows[0], rows[1])
            gate = _short_conv_rows(gate_ref, rows, gw_ref, gb_ref)
            conv = acc_ref[rs, :] + u32_ref[rs, :] * bias
            o_ref[0, rs, :] = (gate * conv).astype(o_ref.dtype)

    pl.when(step_id >= 0)(prologue)
    for f in range(nf + 1):
        pl.when(step_id >= f + 1)(functools.partial(stage, f))
    pl.when(step_id >= nf + 2)(epilogue)


def _hyena_conv(u_arr, u_col, gate_arr, gate_col, lw, khat, order, blk, conv_input, out_dtype):
    bsz, n, _ = gate_arr.shape
    nb = n // blk
    fb = min(FREQ_BLOCK, blk)
    nf = blk // fb
    cos, sin = _dft_tables(blk)
    fwd = np.concatenate([cos.reshape(nf, fb, blk), -sin.reshape(nf, fb, blk)], axis=1)
    inv = np.concatenate([cos.T.reshape(blk, nf, fb), -sin.T.reshape(blk, nf, fb)], axis=2)
    inv = np.transpose(inv, (1, 0, 2)) / blk
    cw, cb = lw["hy_conv_w"], lw["hy_conv_b"]
    col = lambda c: (lambda b, c=c: (0, c))
    once = pl.Buffered(1)
    return pl.pallas_call(
        functools.partial(_conv_kernel, conv_input=conv_input),
        out_shape=jax.ShapeDtypeStruct((bsz, n, HY_WIDTH), out_dtype),
        grid=(bsz,),
        in_specs=[
            pl.BlockSpec((1, n, HY_WIDTH), lambda b, c=u_col: (b, 0, c)),
            pl.BlockSpec((1, n, HY_WIDTH), lambda b, c=gate_col: (b, 0, c)),
            pl.BlockSpec((3, HY_WIDTH), col(u_col if conv_input else 0)),
            pl.BlockSpec((1, HY_WIDTH), col(u_col if conv_input else 0)),
            pl.BlockSpec((3, HY_WIDTH), col(gate_col)),
            pl.BlockSpec((1, HY_WIDTH), col(gate_col)),
            pl.BlockSpec((1, 1, HY_WIDTH), lambda b, o=order: (o, 0, 0)),
            pl.BlockSpec((1, 2 * nb - 1, 2, blk, HY_WIDTH), lambda b, o=order: (o, 0, 0, 0, 0),
                         pipeline_mode=once),
            pl.BlockSpec((nf, 2 * fb, blk), lambda b: (0, 0, 0), pipeline_mode=once),
            pl.BlockSpec((nf, blk, 2 * fb), lambda b: (0, 0, 0), pipeline_mode=once),
        ],
        out_specs=pl.BlockSpec((1, n, HY_WIDTH), lambda b: (b, 0, 0)),
        scratch_shapes=[
            pltpu.VMEM((n, HY_WIDTH), F32),
            pltpu.VMEM((n, HY_WIDTH), BF16),
            pltpu.VMEM((n, HY_WIDTH), F32),
            pltpu.VMEM((nb, 2 * fb, HY_WIDTH), BF16),
            pltpu.VMEM((nb, 2 * fb, HY_WIDTH), BF16),
            pltpu.VMEM((nb, 2 * fb, HY_WIDTH), BF16),
            pltpu.VMEM((nb, 2 * fb, HY_WIDTH), BF16),
        ],
        compiler_params=_params("parallel"),
        name="hyena_conv%d" % order,
    )(u_arr, gate_arr, cw, cb, cw, cb, lw["hy_bias"], khat,
      jnp.asarray(fwd, BF16), jnp.asarray(inv, BF16))


def _hyena(hy_u, lw):
    n = hy_u.shape[1]
    blk = min(CONV_BLOCK, n)
    g, asum = _hyena_filter(n, lw)
    khat = _hyena_spectra(g, asum, blk)
    z = _hyena_conv(hy_u, 0, hy_u, 1, lw, khat, 0, blk, True, BF16)
    return _hyena_conv(z, 0, hy_u, 2, lw, khat, 1, blk, False, F32)


def _merge_kernel(x_ref, attn_ref, pool_ref, hy_ref, g1_ref, sh2_ref, sc2_ref, g2_ref,
                  gout_ref, wout_ref, gmlp_ref, w1_ref, w2_ref, gfin_ref, o_ref, *, final_norm):
    gout = gout_ref[...]
    a = _rms(attn_ref[0].astype(F32), gout[:, :MLA_WIDTH]).astype(BF16)
    p = _rms(pool_ref[0], gout[:, MLA_WIDTH:MLA_WIDTH + POOL_WIDTH]).astype(BF16)
    hh = _rms(hy_ref[0], gout[:, MLA_WIDTH + POOL_WIDTH:]).astype(BF16)
    y = (_bdot(a, wout_ref[0:MLA_WIDTH, :])
         + _bdot(p, wout_ref[MLA_WIDTH:MLA_WIDTH + POOL_WIDTH, :])
         + _bdot(hh, wout_ref[MLA_WIDTH + POOL_WIDTH:, :]))
    x1 = x_ref[0] + g1_ref[0] * y
    h2 = (_rms(x1, gmlp_ref[...]) * (1.0 + sc2_ref[0]) + sh2_ref[0]).astype(BF16)
    y2 = jnp.zeros_like(x1)
    for c in range(D_FF // MLP_CHUNK):
        cs = slice(c * MLP_CHUNK, (c + 1) * MLP_CHUNK)
        hid = jnp.maximum(_bdot(h2, w1_ref[:, cs]), 0.0)
        y2 = y2 + _bdot((hid * hid).astype(BF16), w2_ref[cs, :])
    x2 = x1 + g2_ref[0] * y2
    if final_norm:
        x2 = _rms(x2, gfin_ref[...])
    o_ref[0] = x2


def _merge(x, attn, pool, hy, mods, lw, g_final, final_norm, tile):
    bsz, n, _ = x.shape
    nt = n // tile
    tok = lambda w: pl.BlockSpec((1, tile, w), lambda b, i: (b, i, 0))
    mod_spec = lambda k: pl.BlockSpec((1, 1, D_MODEL), lambda b, i, k=k: (b, 0, k))
    return pl.pallas_call(
        functools.partial(_merge_kernel, final_norm=final_norm),
        out_shape=jax.ShapeDtypeStruct((bsz, n, D_MODEL), F32),
        grid=(bsz, nt),
        in_specs=[
            tok(D_MODEL), tok(MLA_WIDTH), tok(POOL_WIDTH), tok(HY_WIDTH),
            mod_spec(2), mod_spec(3), mod_spec(4), mod_spec(5),
            _const_spec((1, D_MODEL)), _const_spec((D_MODEL, D_MODEL)),
            _const_spec((1, D_MODEL)), _const_spec((D_MODEL, D_FF)),
            _const_spec((D_FF, D_MODEL)), _const_spec((1, D_MODEL)),
        ],
        out_specs=tok(D_MODEL),
        compiler_params=_params("parallel", "parallel"),
        name="merge_mlp",
    )(x, attn, pool, hy, mods, mods, mods, mods, lw["g_out"], lw["w_out"], lw["g_mlp"],
      lw["w_mlp1"], lw["w_mlp2"], g_final)


_ROPE_SWAP = np.concatenate([np.arange(8, 16), np.arange(0, 8), np.arange(24, 32), np.arange(16, 24)])


def _rope_tables(n):
    idx = np.arange(n)
    r = (idx // GRID_W).astype(np.float32)
    c = (idx % GRID_W).astype(np.float32)
    n_freq = HEAD_ROPE // 4
    inv = (ROPE_BASE ** (-np.arange(n_freq, dtype=np.float32) / n_freq)).astype(np.float32)
    ar, ac = r[:, None] * inv, c[:, None] * inv
    cos = np.zeros((n, HEAD_PAD), np.float32)
    sin = np.zeros((n, HEAD_PAD), np.float32)
    cos[:, :HEAD_NOPE] = 1.0
    cos[:, HEAD_NOPE:HEAD_NOPE + HEAD_ROPE] = np.concatenate(
        [np.cos(ar), np.cos(ar), np.cos(ac), np.cos(ac)], axis=1)
    sin[:, HEAD_NOPE:HEAD_NOPE + HEAD_ROPE] = np.concatenate(
        [-np.sin(ar), np.sin(ar), -np.sin(ac), np.sin(ac)], axis=1)
    return jnp.asarray(cos), jnp.asarray(sin)


def _identity_tables(n):
    cos = np.zeros((n, HEAD_PAD), np.float32)
    cos[:, :HEAD_NOPE + HEAD_ROPE] = 1.0
    return jnp.asarray(cos), jnp.zeros((n, HEAD_PAD), F32)


def _layer_weights(l, w):
    w_in = w["w_in"][l]
    kr = w_in[:, COL_KR:COL_Q]
    w_in_p = jnp.concatenate([w_in[:, COL_KV:COL_KR], w_in[:, COL_Q:COL_POOL],
                              w_in[:, COL_POOL:COL_HY], w_in[:, COL_HY:],
                              jnp.zeros((D_MODEL, HEAD_NOPE), F32), kr, kr[:, _ROPE_SWAP]],
                             axis=1).astype(BF16)
    wkv = w["w_kv_up"][l].reshape(KV_RANK, N_HEADS, HEAD_NOPE + HEAD_V)
    wka = jnp.concatenate([wkv[:, :, :HEAD_NOPE], jnp.zeros((KV_RANK, N_HEADS, HEAD_V), F32)],
                          axis=2).reshape(KV_RANK, N_HEADS * HEAD_PAD).astype(BF16)
    wvt = jnp.concatenate([wkv[:, :, HEAD_NOPE:],
                           jnp.zeros((KV_RANK, N_HEADS, V_SLOT - HEAD_V), F32)], axis=2)
    wvt = wvt.reshape(KV_RANK, VT_ROWS).T.astype(BF16)
    wq = w["w_q_up"][l].reshape(Q_RANK, N_HEADS, HEAD_NOPE + HEAD_ROPE)
    wqt = jnp.concatenate([wq.reshape(Q_RANK, -1),
                           wq[:, :, HEAD_NOPE:][:, :, _ROPE_SWAP].reshape(Q_RANK, -1)],
                          axis=1).T.astype(BF16)
    pool_bd = jnp.zeros((POOL_WIDTH, POOL_WIDTH), F32)
    for g in range(len(POOL_WINDOWS)):
        sl = slice(g * POOL_GROUP, (g + 1) * POOL_GROUP)
        pool_bd = pool_bd.at[sl, sl].set(w["pool_w"][l, g])
    pad_rows = lambda m, rows: jnp.zeros((rows, m.shape[1]), F32).at[:m.shape[0]].set(m)
    pad_cols = lambda m, cols: jnp.zeros((m.shape[0], cols), F32).at[:, :m.shape[1]].set(m)
    row = lambda v: v.reshape(1, -1)
    return {
        "g_mix": row(w["g_mix"][l]), "w_in": w_in_p, "g_kv": row(w["g_kv"][l]), "wka": wka,
        "wvt": wvt, "g_q": row(w["g_q"][l]), "wqt": wqt,
        "pool_w": pool_bd.astype(BF16), "pool_scale": row(w["pool_scale"][l]),
        "hy_conv_w": w["hy_conv_w"][l], "hy_conv_b": row(w["hy_conv_b"][l]),
        "hy_w1": pad_cols(pad_rows(w["hy_f_w1"][l], LANE), LANE),
        "hy_b1": pad_cols(row(w["hy_f_b1"][l]), LANE),
        "hy_f1": pad_cols(row(w["hy_f_freq1"][l]), LANE),
        "hy_w2": pad_cols(pad_rows(w["hy_f_w2"][l], LANE), LANE),
        "hy_b2": pad_cols(row(w["hy_f_b2"][l]), LANE),
        "hy_f2": pad_cols(row(w["hy_f_freq2"][l]), LANE),
        "hy_w3": pad_rows(w["hy_f_w3"][l], LANE),
        "hy_bias": w["hy_bias"][l].reshape(2, 1, HY_WIDTH),
        "g_out": row(w["g_out"][l]), "w_out": w["w_out"][l].astype(BF16),
        "g_mlp": row(w["g_mlp"][l]), "w_mlp1": w["w_mlp1"][l].astype(BF16),
        "w_mlp2": w["w_mlp2"][l].astype(BF16),
    }


def kernel(x, c, ctx, c_ctx, w_mod, b_mod, g_mix, g_mlp, w_in, g_q, w_q_up, g_kv, w_kv_up,
           pool_w, pool_scale, hy_conv_w, hy_conv_b, hy_f_w1, hy_f_b1, hy_f_freq1, hy_f_w2,
           hy_f_b2, hy_f_freq2, hy_f_w3, hy_bias, g_out, w_out, w_mlp1, w_mlp2, g_final):
    w = dict(g_mix=g_mix, g_mlp=g_mlp, w_in=w_in, g_q=g_q, w_q_up=w_q_up, g_kv=g_kv,
             w_kv_up=w_kv_up, pool_w=pool_w, pool_scale=pool_scale, hy_conv_w=hy_conv_w,
             hy_conv_b=hy_conv_b, hy_f_w1=hy_f_w1, hy_f_b1=hy_f_b1, hy_f_freq1=hy_f_freq1,
             hy_f_w2=hy_f_w2, hy_f_b2=hy_f_b2, hy_f_freq2=hy_f_freq2, hy_f_w3=hy_f_w3,
             hy_bias=hy_bias, g_out=g_out, w_out=w_out, w_mlp1=w_mlp1, w_mlp2=w_mlp2)
    depth = w_mod.shape[0]
    bsz, n, _ = x.shape
    n_ctx = ctx.shape[1]
    tile = min(TOKEN_TILE, n)
    tile_c = min(TOKEN_TILE, n_ctx)

    rows = -(-(bsz + 1) // 8) * 8
    cond = jnp.zeros((rows, D_MODEL), F32).at[:bsz].set(c).at[bsz].set(c_ctx)
    mods_all = _modulation(cond, w_mod, b_mod)
    cos_l, sin_l = _rope_tables(n)
    cos_c, sin_c = _identity_tables(n_ctx)
    g_fin = g_final.reshape(1, D_MODEL)

    xc = ctx
    for l in range(depth):
        last = l == depth - 1
        lw = _layer_weights(l, w)
        mods = mods_all[l, :bsz].reshape(bsz, 1, 6 * D_MODEL)
        mods_c = jnp.broadcast_to(mods_all[l, bsz].reshape(1, 1, 6 * D_MODEL),
                                  (bsz, 1, 6 * D_MODEL))
        q, k, vt, pool_u, hy_u = _inproj(x, (mods, 0), (mods, 1), lw, cos_l, sin_l, tile)
        qc, kc, vtc, pool_uc, hy_uc = _inproj(xc, (mods_c, 0), (mods_c, 1), lw, cos_c, sin_c,
                                              tile_c)
        attn = _attention(q, [(kc, vtc), (k, vt)])
        pool = _pool(pool_u, lw["pool_w"], lw["pool_scale"])
        hy = _hyena(hy_u, lw)
        x = _merge(x, attn, pool, hy, mods, lw, g_fin, last, tile)
        if not last:
            attn_c = _attention(qc, [(kc, vtc)])
            pool_c = _pool(pool_uc, lw["pool_w"], lw["pool_scale"])
            hy_c = _hyena(hy_uc, lw)
            xc = _merge(xc, attn_c, pool_c, hy_c, mods_c, lw, g_fin, False, tile_c)
    return x
```

```python
import functools
import math

import jax
import jax.numpy as jnp
import numpy as np
from jax import lax
from jax.experimental import pallas as pl
from jax.experimental.pallas import tpu as pltpu

F32 = jnp.float32
BF16 = jnp.bfloat16

D_MODEL = 1024
GRID_W = 64
EPS = 1e-6
N_HEADS = 8
HEAD_V = 64
HEAD_NOPE = 64
HEAD_ROPE = 32
HEAD_PAD = 128
Q_RANK = 256
KV_RANK = 128
MLA_WIDTH = N_HEADS * HEAD_V
V_SLOT = 80
VT_ROWS = N_HEADS * V_SLOT
Q_ROWS = N_HEADS * (HEAD_NOPE + 2 * HEAD_ROPE)
POOL_WIDTH = 256
HY_WIDTH = 256
MLA_SCALE = (HEAD_NOPE + HEAD_ROPE) ** -0.5
Q_SCALE = MLA_SCALE * math.log2(math.e)
ROPE_BASE = 10000.0
POOL_WINDOWS = (2, 4, 8, 16)
POOL_GROUP = POOL_WIDTH // len(POOL_WINDOWS)
POOL_PAD = 8
EDGE_ROWS = 16
HY_EMB = 33
HY_FFN = 64
HY_DECAY_TARGET = 1e-2
HY_DECAY_SHORT_PCT = 0.3
HY_DECAY_LONG_PCT = 1.5
D_FF = 4 * D_MODEL
COL_KV, COL_KR, COL_Q, COL_POOL, COL_HY, N_IN = 0, 128, 160, 416, 672, 1440
P_KV, P_Q, P_POOL, P_HY, P_KR, N_IN_P = 0, 128, 384, 640, 1408, 1536

LANE = 128
VMEM_LIMIT = 56 * 1024 * 1024
ATTN_VMEM_LIMIT = 60 * 1024 * 1024
TOKEN_TILE = 512
Q_TILE = 512
Q_TILES_PER_STEP = 2
CHUNK_UNROLL = 8
CONV_BLOCK = 512
FREQ_BLOCK = 256
FREQ_ROWS = 16
MLP_CHUNK = 1024
HIGHEST = lax.Precision.HIGHEST
HIGH = lax.Precision.HIGH


def _params(*sem):
    return pltpu.CompilerParams(dimension_semantics=sem, vmem_limit_bytes=VMEM_LIMIT)


def _const_spec(shape):
    zeros = (0,) * len(shape)
    return pl.BlockSpec(shape, lambda *_: zeros, pipeline_mode=pl.Buffered(1))


def _rms(x, g):
    return x * lax.rsqrt(jnp.mean(x * x, axis=-1, keepdims=True) + EPS) * g


def _bdot(a, b):
    return jnp.dot(a, b, preferred_element_type=F32)


def _split(x):
    hi = x.astype(BF16)
    return hi, (x - hi.astype(F32)).astype(BF16)


def _dot3(a, b):
    return _bdot(a[0], b[0]) + (_bdot(a[0], b[1]) + _bdot(a[1], b[0]))


def _modulation_kernel(c_ref, w_ref, b_ref, o_ref):
    c = c_ref[...]
    s = c / (1.0 + jnp.exp(-c))
    o_ref[0] = jnp.dot(s, w_ref[0], preferred_element_type=F32, precision=HIGHEST) + b_ref[0]


def _modulation(cond, w_mod, b_mod):
    depth = w_mod.shape[0]
    rows = cond.shape[0]
    return pl.pallas_call(
        _modulation_kernel,
        out_shape=jax.ShapeDtypeStruct((depth, rows, 6 * D_MODEL), F32),
        grid=(depth, 6),
        in_specs=[
            pl.BlockSpec((rows, D_MODEL), lambda l, j: (0, 0)),
            pl.BlockSpec((1, D_MODEL, D_MODEL), lambda l, j: (l, 0, j)),
            pl.BlockSpec((1, 1, D_MODEL), lambda l, j: (l, 0, j)),
        ],
        out_specs=pl.BlockSpec((1, rows, D_MODEL), lambda l, j: (l, 0, j)),
        compiler_params=_params("arbitrary", "arbitrary"),
        name="modulation",
    )(cond, w_mod, b_mod.reshape(depth, 1, 6 * D_MODEL))


def _inproj_kernel(x_ref, sh_ref, sc_ref, gmix_ref, win_ref, gkv_ref, wka_ref, wvt_ref,
                   gq_ref, wqt_ref, cos_ref, sin_ref, cost_ref, sint_ref,
                   qt_ref, k_ref, vt_ref, pool_ref, hy_ref):
    x = x_ref[0]
    h = (_rms(x, gmix_ref[...]) * (1.0 + sc_ref[0]) + sh_ref[0]).astype(BF16)
    proj = _bdot(h, win_ref[...])
    pool_ref[0] = proj[:, P_POOL:P_HY]
    hy_ref[0] = proj[:, P_HY:P_KR].astype(hy_ref.dtype)
    kvn = _rms(proj[:, P_KV:P_Q], gkv_ref[...]).astype(BF16)
    kpad = _bdot(kvn, wka_ref[...])
    vt = lax.dot_general(wvt_ref[...], kvn, (((1,), (1,)), ((), ())),
                         preferred_element_type=F32)
    row = lax.broadcasted_iota(jnp.int32, vt.shape, 0)
    vt = jnp.where(row % V_SLOT >= HEAD_V, 1.0, vt).astype(vt_ref.dtype)
    vt_ref[0, 0] = vt.reshape(N_HEADS, V_SLOT, vt.shape[1])
    qn = _rms(proj[:, P_Q:P_POOL], gq_ref[...]).astype(BF16)
    qt = lax.dot_general(wqt_ref[...], qn, (((1,), (1,)), ((), ())),
                         preferred_element_type=F32)
    krab = proj[:, P_KR:N_IN_P]
    kr = krab * cos_ref[...] + pltpu.roll(krab, HEAD_PAD - HEAD_ROPE, 1) * sin_ref[...]
    cost = cost_ref[...]
    sint = sint_ref[...]
    head_q = HEAD_NOPE + HEAD_ROPE
    pad = jnp.zeros((HEAD_PAD - head_q, qt.shape[1]), F32)
    for hd in range(N_HEADS):
        sl = slice(hd * HEAD_PAD, (hd + 1) * HEAD_PAD)
        k_ref[0, hd] = (kpad[:, sl] + kr).astype(k_ref.dtype)
        a = qt[hd * head_q:(hd + 1) * head_q, :]
        b = qt[N_HEADS * head_q + hd * HEAD_ROPE:N_HEADS * head_q + (hd + 1) * HEAD_ROPE, :]
        rope = a[HEAD_NOPE:, :] * cost + b * sint
        q_h = jnp.concatenate([a[:HEAD_NOPE, :], rope, pad], axis=0) * Q_SCALE
        qt_ref[0, hd] = q_h.astype(qt_ref.dtype)


def _inproj(x, shift, scale, lw, cos_t, sin_t, tile):
    bsz, n, _ = x.shape
    nt = n // tile
    width = N_HEADS * HEAD_PAD
    mod_spec = lambda k: pl.BlockSpec((1, 1, D_MODEL), lambda b, i, k=k: (b, 0, k))
    return pl.pallas_call(
        _inproj_kernel,
        out_shape=(
            jax.ShapeDtypeStruct((bsz, N_HEADS, HEAD_PAD, n), BF16),
            jax.ShapeDtypeStruct((bsz, N_HEADS, n, HEAD_PAD), BF16),
            jax.ShapeDtypeStruct((bsz, nt, N_HEADS, V_SLOT, tile), BF16),
            jax.ShapeDtypeStruct((bsz, n, POOL_WIDTH), F32),
            jax.ShapeDtypeStruct((bsz, n, 3 * HY_WIDTH), BF16),
        ),
        grid=(bsz, nt),
        in_specs=[
            pl.BlockSpec((1, tile, D_MODEL), lambda b, i: (b, i, 0)),
            mod_spec(shift[1]), mod_spec(scale[1]),
            _const_spec((1, D_MODEL)),
            _const_spec((D_MODEL, N_IN_P)),
            _const_spec((1, KV_RANK)),
            _const_spec((KV_RANK, width)),
            _const_spec((VT_ROWS, KV_RANK)),
            _const_spec((1, Q_RANK)),
            _const_spec((Q_ROWS, Q_RANK)),
            pl.BlockSpec((tile, HEAD_PAD), lambda b, i: (i, 0)),
            pl.BlockSpec((tile, HEAD_PAD), lambda b, i: (i, 0)),
            pl.BlockSpec((HEAD_ROPE, tile), lambda b, i: (0, i)),
            pl.BlockSpec((HEAD_ROPE, tile), lambda b, i: (0, i)),
        ],
        out_specs=(
            pl.BlockSpec((1, N_HEADS, HEAD_PAD, tile), lambda b, i: (b, 0, 0, i)),
            pl.BlockSpec((1, N_HEADS, tile, HEAD_PAD), lambda b, i: (b, 0, i, 0)),
            pl.BlockSpec((1, 1, N_HEADS, V_SLOT, tile), lambda b, i: (b, i, 0, 0, 0)),
            pl.BlockSpec((1, tile, POOL_WIDTH), lambda b, i: (b, i, 0)),
            pl.BlockSpec((1, tile, 3 * HY_WIDTH), lambda b, i: (b, i, 0)),
        ),
        compiler_params=_params("parallel", "parallel"),
        name="inproj",
    )(x, shift[0], scale[0], lw["g_mix"], lw["w_in"], lw["g_kv"], lw["wka"], lw["wvt"],
      lw["g_q"], lw["wqt"], cos_t, sin_t,
      cos_t[:, HEAD_NOPE:HEAD_NOPE + HEAD_ROPE].T, sin_t[:, HEAD_NOPE:HEAD_NOPE + HEAD_ROPE].T)


def _attention_kernel(*refs, seg_chunks):
    qt_ref, qt_next_ref = refs[0], refs[1]
    n_seg = len(seg_chunks)
    seg_refs = refs[2:2 + 2 * n_seg]
    o_ref = refs[2 + 2 * n_seg]
    out_ref, m_ref = refs[3 + 2 * n_seg], refs[4 + 2 * n_seg]
    s_bufs = refs[5 + 2 * n_seg:]
    tq = out_ref.shape[2]
    n_tiles = qt_ref.shape[3] // tq
    items = [(qt_ref, t, hd) for t in range(n_tiles) for hd in range(N_HEADS)]
    assert len(items) % 2 == 0
    step, last_step = pl.program_id(1), pl.num_programs(1) - 1
    step_id = pl.program_id(0) + step + len(items)

    def stage(item1, item2, parity1):
        buf1, buf2 = s_bufs[parity1], s_bufs[1 - parity1]
        m2 = m_ref[...] if item2 is not None else None
        m1 = jnp.full((1, tq), -jnp.inf, F32)
        acc2 = jnp.zeros((V_SLOT, tq), F32)
        base = 0
        for si, (n_chunks, chunk) in enumerate(seg_chunks):
            k_ref, vt_ref = seg_refs[2 * si], seg_refs[2 * si + 1]
            for c in range(n_chunks):
                rows = slice(base + c * chunk, base + (c + 1) * chunk)
                if item1 is not None:
                    q1_ref, t1, h1 = item1
                    s = _bdot(k_ref[0, h1, c * chunk:(c + 1) * chunk, :],
                              q1_ref[0, h1, :, t1 * tq:(t1 + 1) * tq])
                    buf1[rows, :] = s
                    m1 = jnp.maximum(m1, jnp.max(s, axis=0, keepdims=True))
                if item2 is not None:
                    p = jnp.exp2(buf2[rows, :] - m2).astype(BF16)
                    acc2 = acc2 + _bdot(vt_ref[0, c, item2[2]], p)
            base += n_chunks * chunk
        if item2 is not None:
            _, t2, h2 = item2
            out_ref[h2] = acc2[:HEAD_V, :] / acc2[HEAD_V:HEAD_V + 1, :]
            if h2 == N_HEADS - 1:
                o_ref[0, t2 * tq:(t2 + 1) * tq, :] = (
                    out_ref[...].reshape(MLA_WIDTH, tq).T.astype(o_ref.dtype))
        if item1 is not None:
            m_ref[...] = m1

    pl.when(step == 0)(functools.partial(stage, items[0], None, 0))
    for s in range(len(items) - 1):
        pl.when(step_id >= s)(functools.partial(stage, items[s + 1], items[s], (s + 1) % 2))
    next_first = (qt_next_ref, 0, 0)
    pl.when(step < last_step)(functools.partial(stage, next_first, items[-1], 0))
    pl.when(step == last_step)(functools.partial(stage, None, items[-1], 0))


def _attention(q, segments):
    bsz, _, _, n = q.shape
    tq = min(Q_TILE, n)
    step_q = min(Q_TILES_PER_STEP * tq, n)
    last = n // step_q - 1
    in_specs = [pl.BlockSpec((1, N_HEADS, HEAD_PAD, step_q), lambda b, i: (b, 0, 0, i)),
                pl.BlockSpec((1, N_HEADS, HEAD_PAD, step_q),
                             lambda b, i: (b, 0, 0, jnp.minimum(i + 1, last)))]
    args = [q, q]
    seg_chunks = []
    for k, vt in segments:
        nk = k.shape[2]
        n_chunks, chunk = vt.shape[1], vt.shape[4]
        assert n_chunks * chunk == nk
        seg_chunks.append((n_chunks, chunk))
        in_specs.append(pl.BlockSpec((1, N_HEADS, nk, HEAD_PAD), lambda b, i: (b, 0, 0, 0)))
        in_specs.append(pl.BlockSpec((1, n_chunks, N_HEADS, V_SLOT, chunk),
                                     lambda b, i: (b, 0, 0, 0, 0)))
        args += [k, vt]
    n_keys = sum(nc * ch for nc, ch in seg_chunks)
    return pl.pallas_call(
        functools.partial(_attention_kernel, seg_chunks=tuple(seg_chunks)),
        out_shape=jax.ShapeDtypeStruct((bsz, n, MLA_WIDTH), BF16),
        grid=(bsz, n // step_q),
        in_specs=in_specs,
        out_specs=pl.BlockSpec((1, step_q, MLA_WIDTH), lambda b, i: (b, i, 0)),
        scratch_shapes=[pltpu.VMEM((N_HEADS, HEAD_V, tq), F32), pltpu.VMEM((1, tq), F32),
                        pltpu.VMEM((n_keys, tq), F32), pltpu.VMEM((n_keys, tq), F32)],
        compiler_params=pltpu.CompilerParams(dimension_semantics=("arbitrary", "arbitrary"),
                                             vmem_limit_bytes=ATTN_VMEM_LIMIT),
        name="attention",
    )(*args)


def _shift_rows(a, k):
    n = a.shape[0]
    return pltpu.roll(a, k % n, 0)


def _pool_kernel(u_ref, inv_ref, w_ref, scale_ref, o_ref, ext_ref):
    n = u_ref.shape[1]
    u = u_ref[0]
    zeros = jnp.zeros((POOL_PAD, POOL_WIDTH), F32)
    ext_ref[0:POOL_PAD, :] = zeros
    ext_ref[POOL_PAD + n:, :] = zeros
    ext_ref[POOL_PAD:POOL_PAD + n, :] = u
    ext = ext_ref[...]
    s2 = ext + _shift_rows(ext, 1)
    s4 = _shift_rows(s2, 1) + _shift_rows(s2, -1)
    s8 = _shift_rows(s4, 2) + _shift_rows(s4, -2)
    s16 = _shift_rows(s8, 4) + _shift_rows(s8, -4)
    lane = lax.broadcasted_iota(jnp.int32, (n, POOL_WIDTH), 1)
    wsum = None
    for g, s in enumerate((s2, s4, s8, s16)):
        sg = s[POOL_PAD:POOL_PAD + n, :]
        wsum = sg if wsum is None else jnp.where(lane >= g * POOL_GROUP, sg, wsum)
    diff = (wsum * inv_ref[...] - u).astype(BF16)
    o_ref[0] = _bdot(diff, w_ref[...]) * scale_ref[...]


def _pool(u, w_bd, scale):
    bsz, n, _ = u.shape
    t = np.arange(n)
    inv = np.concatenate(
        [np.repeat((1.0 / (np.clip(t + w // 2, 0, n) - np.clip(t - w // 2, 0, n)))[:, None],
                   POOL_GROUP, axis=1) for w in POOL_WINDOWS], axis=1).astype(np.float32)
    return pl.pallas_call(
        _pool_kernel,
        out_shape=jax.ShapeDtypeStruct((bsz, n, POOL_WIDTH), F32),
        grid=(bsz,),
        in_specs=[
            pl.BlockSpec((1, n, POOL_WIDTH), lambda b: (b, 0, 0)),
            _const_spec((n, POOL_WIDTH)),
            _const_spec((POOL_WIDTH, POOL_WIDTH)),
            _const_spec((1, POOL_WIDTH)),
        ],
        out_specs=pl.BlockSpec((1, n, POOL_WIDTH), lambda b: (b, 0, 0)),
        scratch_shapes=[pltpu.VMEM((n + 2 * POOL_PAD, POOL_WIDTH), F32)],
        compiler_params=_params("parallel"),
        name="pool_mixer",
    )(u, jnp.asarray(inv), w_bd, scale)


def _filter_kernel(z_ref, w1_ref, b1_ref, f1_ref, w2_ref, b2_ref, f2_ref, w3h_ref, w3l_ref,
                   dl_ref, g_ref, asum_ref, *, n):
    i = pl.program_id(0)
    rows_per = z_ref.shape[0]
    z = z_ref[...]
    h = jnp.sin(f1_ref[...] * (jnp.dot(z, w1_ref[...], preferred_element_type=F32,
                                       precision=HIGHEST) + b1_ref[...]))
    h = jnp.sin(f2_ref[...] * (jnp.dot(h, w2_ref[...], preferred_element_type=F32,
                                       precision=HIGHEST) + b2_ref[...]))
    h = _dot3(_split(h), (w3h_ref[...], w3l_ref[...]))
    r = i * rows_per + lax.broadcasted_iota(jnp.int32, (rows_per, HY_WIDTH), 0)
    backward = r < n
    pos = jnp.where(backward, n - r, r - n)
    t = pos.astype(F32) * (1.0 / (n - 1))
    decay = jnp.exp(-t * dl_ref[...])
    valid = r > 0

    @pl.when(i == 0)
    def _():
        asum_ref[...] = jnp.zeros_like(asum_ref)

    for o in range(2):
        fwd = h[:, (2 * o) * HY_WIDTH:(2 * o + 1) * HY_WIDTH]
        bwd = h[:, (2 * o + 1) * HY_WIDTH:(2 * o + 2) * HY_WIDTH]
        g = jnp.where(valid, jnp.where(backward, bwd, fwd) * decay, 0.0)
        g_ref[o] = g
        asum_ref[o] += jnp.sum(jnp.abs(g), axis=0, keepdims=True)


def _hyena_filter(n, lw):
    rows = 2 * n
    tile = min(rows, 1024)
    r = np.arange(rows)
    pos = np.where(r < n, n - r, r - n).astype(np.float64)
    t = pos / (n - 1)
    bands = (HY_EMB - 1) // 2
    freqs = np.linspace(1e-4, bands - 1, bands)[None, :]
    wpos = 2.0 * math.pi * pos[:, None] / n
    z = np.zeros((rows, LANE), np.float32)
    z[:, 0] = t
    z[:, 1:1 + bands] = np.cos(freqs * wpos)
    z[:, 1 + bands:HY_EMB] = -np.sin(freqs * wpos)
    deltas = np.abs(np.linspace(math.log(HY_DECAY_TARGET) / HY_DECAY_LONG_PCT,
                                math.log(HY_DECAY_TARGET) / HY_DECAY_SHORT_PCT, HY_WIDTH))
    deltas = jnp.asarray(deltas[None, :], F32)
    return pl.pallas_call(
        functools.partial(_filter_kernel, n=n),
        out_shape=(jax.ShapeDtypeStruct((2, rows, HY_WIDTH), F32),
                   jax.ShapeDtypeStruct((2, 1, HY_WIDTH), F32)),
        grid=(rows // tile,),
        in_specs=[
            pl.BlockSpec((tile, LANE), lambda i: (i, 0)),
            _const_spec((LANE, LANE)), _const_spec((1, LANE)), _const_spec((1, LANE)),
            _const_spec((LANE, LANE)), _const_spec((1, LANE)), _const_spec((1, LANE)),
            _const_spec((LANE, 4 * HY_WIDTH)), _const_spec((LANE, 4 * HY_WIDTH)),
            _const_spec((1, HY_WIDTH)),
        ],
        out_specs=(pl.BlockSpec((2, tile, HY_WIDTH), lambda i: (0, i, 0)),
                   pl.BlockSpec((2, 1, HY_WIDTH), lambda i: (0, 0, 0))),
        compiler_params=_params("arbitrary"),
        name="hyena_filter",
    )(jnp.asarray(z), lw["hy_w1"], lw["hy_b1"], lw["hy_f1"], lw["hy_w2"], lw["hy_b2"],
      lw["hy_f2"], *_split(lw["hy_w3"]), deltas)


def _dft_tables(blk):
    f = np.arange(blk, dtype=np.float64)[:, None]
    m = np.arange(blk, dtype=np.float64)[None, :]
    theta = math.pi * (2.0 * f + 1.0) * m / (2.0 * blk)
    return np.cos(theta), np.sin(theta)


def _spectra_kernel(g_ref, asum_ref, fwdh_ref, fwdl_ref, k_ref, prev_ref):
    e = pl.program_id(1)
    blk = g_ref.shape[1]
    s = _dot3((fwdh_ref[...], fwdl_ref[...]), _split(g_ref[0]))
    s = s / asum_ref[0]

    @pl.when(e > 0)
    def _():
        prev = prev_ref[...]
        f = lax.broadcasted_iota(jnp.int32, (blk, HY_WIDTH), 0)
        sgn = jnp.where(f % 2 == 0, 1.0, -1.0).astype(F32)
        k_ref[0, 0, 0] = (s[:blk] - sgn * prev[blk:]).astype(k_ref.dtype)
        k_ref[0, 0, 1] = (s[blk:] + sgn * prev[:blk]).astype(k_ref.dtype)

    prev_ref[...] = s


def _hyena_spectra(g, asum, blk):
    rows = g.shape[1]
    nseg = rows // blk
    cos, sin = _dft_tables(blk)
    fwd = jnp.asarray(np.concatenate([cos, -sin], axis=0), F32)
    return pl.pallas_call(
        _spectra_kernel,
        out_shape=jax.ShapeDtypeStruct((2, nseg - 1, 2, blk, HY_WIDTH), BF16),
        grid=(2, nseg),
        in_specs=[
            pl.BlockSpec((1, blk, HY_WIDTH), lambda o, e: (o, e, 0)),
            pl.BlockSpec((1, 1, HY_WIDTH), lambda o, e: (o, 0, 0)),
            _const_spec((2 * blk, blk)), _const_spec((2 * blk, blk)),
        ],
        out_specs=pl.BlockSpec((1, 1, 2, blk, HY_WIDTH),
                               lambda o, e: (o, jnp.maximum(e - 1, 0), 0, 0, 0)),
        scratch_shapes=[pltpu.VMEM((2 * blk, HY_WIDTH), F32)],
        compiler_params=_params("arbitrary", "arbitrary"),
        name="hyena_spectra",
    )(g, asum, *_split(fwd))


def _short_conv_rows(src_ref, rows, w_ref, b_ref):
    r0, r1 = rows
    n = src_ref.shape[1]
    cur = src_ref[0, r0:r1, :].astype(F32)
    row = lax.broadcasted_iota(jnp.int32, cur.shape, 0)
    edge = jnp.zeros((1, cur.shape[1]), F32)
    before = src_ref[0, r0 - EDGE_ROWS:r0, :].astype(F32)[EDGE_ROWS - 1:, :] if r0 > 0 else edge
    after = src_ref[0, r1:r1 + EDGE_ROWS, :].astype(F32)[:1, :] if r1 < n else edge
    prev = jnp.where(row == 0, before, _shift_rows(cur, 1))
    nxt = jnp.where(row == r1 - r0 - 1, after, _shift_rows(cur, -1))
    return prev * w_ref[0:1, :] + cur * w_ref[1:2, :] + nxt * w_ref[2:3, :] + b_ref[...]


def _conv_kernel(u_ref, gate_ref, uw_ref, ub_ref, gw_ref, gb_ref, bias_ref, k_ref, fwd_ref,
                 inv_ref, o_ref, u32_ref, u16_ref, acc_ref, uf0_ref, uf1_ref, yf0_ref, yf1_ref,
                 *, conv_input):
    n = u_ref.shape[1]
    nf, blk = fwd_ref.shape[0], fwd_ref.shape[2]
    nb = n // blk
    fb = fwd_ref.shape[1] // 2
    uf_refs, yf_refs = (uf0_ref, uf1_ref), (yf0_ref, yf1_ref)
    step_id = pl.program_id(0) + nf + 2

    def forward(f):
        for j in range(nb):
            uf_refs[f % 2][j] = _bdot(fwd_ref[f], u16_ref[j * blk:(j + 1) * blk, :]).astype(BF16)

    def multiply(f):
        uf_ref, yf_ref = uf_refs[f % 2], yf_refs[f % 2]
        for c in range(fb // FREQ_ROWS):
            rs = slice(c * FREQ_ROWS, (c + 1) * FREQ_ROWS)
            rs_im = slice(fb + c * FREQ_ROWS, fb + (c + 1) * FREQ_ROWS)
            ks = slice(f * fb + c * FREQ_ROWS, f * fb + (c + 1) * FREQ_ROWS)
            for i in range(nb):
                yr = jnp.zeros((FREQ_ROWS, HY_WIDTH), BF16)
                yi = jnp.zeros((FREQ_ROWS, HY_WIDTH), BF16)
                for j in range(nb):
                    d = i - j + nb - 1
                    kr = k_ref[0, d, 0, ks, :]
                    ki = k_ref[0, d, 1, ks, :]
                    ur = uf_ref[j, rs, :]
                    ui = uf_ref[j, rs_im, :]
                    yr = yr + (kr * ur - ki * ui)
                    yi = yi + (kr * ui + ki * ur)
                yf_ref[i, rs, :] = yr
                yf_ref[i, rs_im, :] = yi

    def inverse(f):
        for i in range(nb):
            y = _bdot(inv_ref[f], yf_refs[f % 2][i])
            rows = slice(i * blk, (i + 1) * blk)
            acc_ref[rows, :] = y if f == 0 else acc_ref[rows, :] + y

    def prologue():
        for j in range(nb):
            rows = (j * blk, (j + 1) * blk)
            if conv_input:
                u = _short_conv_rows(u_ref, rows, uw_ref, ub_ref)
            else:
                u = u_ref[0, rows[0]:rows[1], :].astype(F32)
            u32_ref[rows[0]:rows[1], :] = u
            u16_ref[rows[0]:rows[1], :] = u.astype(BF16)
        forward(0)

    def stage(f):
        if f + 1 < nf:
            forward(f + 1)
        if f < nf:
            multiply(f)
        if f >= 1:
            inverse(f - 1)

    def epilogue():
        bias = bias_ref[0]
        for j in range(nb):
            rows = (j * blk, (j + 1) * blk)
            rs = slice(rows[0], rows[1])
            gate = _short_conv_rows(gate_ref, rows, gw_ref, gb_ref)
            conv = acc_ref[rs, :] + u32_ref[rs, :] * bias
            o_ref[0, rs, :] = (gate * conv).astype(o_ref.dtype)

    pl.when(step_id >= 0)(prologue)
    for f in range(nf + 1):
        pl.when(step_id >= f + 1)(functools.partial(stage, f))
    pl.when(step_id >= nf + 2)(epilogue)


def _hyena_conv(u_arr, u_col, gate_arr, gate_col, lw, khat, order, blk, conv_input, out_dtype):
    bsz, n, _ = gate_arr.shape
    nb = n // blk
    fb = min(FREQ_BLOCK, blk)
    nf = blk // fb
    cos, sin = _dft_tables(blk)
    fwd = np.concatenate([cos.reshape(nf, fb, blk), -sin.reshape(nf, fb, blk)], axis=1)
    inv = np.concatenate([cos.T.reshape(blk, nf, fb), -sin.T.reshape(blk, nf, fb)], axis=2)
    inv = np.transpose(inv, (1, 0, 2)) / blk
    cw, cb = lw["hy_conv_w"], lw["hy_conv_b"]
    col = lambda c: (lambda b, c=c: (0, c))
    once = pl.Buffered(1)
    return pl.pallas_call(
        functools.partial(_conv_kernel, conv_input=conv_input),
        out_shape=jax.ShapeDtypeStruct((bsz, n, HY_WIDTH), out_dtype),
        grid=(bsz,),
        in_specs=[
            pl.BlockSpec((1, n, HY_WIDTH), lambda b, c=u_col: (b, 0, c)),
            pl.BlockSpec((1, n, HY_WIDTH), lambda b, c=gate_col: (b, 0, c)),
            pl.BlockSpec((3, HY_WIDTH), col(u_col if conv_input else 0)),
            pl.BlockSpec((1, HY_WIDTH), col(u_col if conv_input else 0)),
            pl.BlockSpec((3, HY_WIDTH), col(gate_col)),
            pl.BlockSpec((1, HY_WIDTH), col(gate_col)),
            pl.BlockSpec((1, 1, HY_WIDTH), lambda b, o=order: (o, 0, 0)),
            pl.BlockSpec((1, 2 * nb - 1, 2, blk, HY_WIDTH), lambda b, o=order: (o, 0, 0, 0, 0),
                         pipeline_mode=once),
            pl.BlockSpec((nf, 2 * fb, blk), lambda b: (0, 0, 0), pipeline_mode=once),
            pl.BlockSpec((nf, blk, 2 * fb), lambda b: (0, 0, 0), pipeline_mode=once),
        ],
        out_specs=pl.BlockSpec((1, n, HY_WIDTH), lambda b: (b, 0, 0)),
        scratch_shapes=[
            pltpu.VMEM((n, HY_WIDTH), F32),
            pltpu.VMEM((n, HY_WIDTH), BF16),
            pltpu.VMEM((n, HY_WIDTH), F32),
            pltpu.VMEM((nb, 2 * fb, HY_WIDTH), BF16),
            pltpu.VMEM((nb, 2 * fb, HY_WIDTH), BF16),
            pltpu.VMEM((nb, 2 * fb, HY_WIDTH), BF16),
            pltpu.VMEM((nb, 2 * fb, HY_WIDTH), BF16),
        ],
        compiler_params=_params("parallel"),
        name="hyena_conv%d" % order,
    )(u_arr, gate_arr, cw, cb, cw, cb, lw["hy_bias"], khat,
      jnp.asarray(fwd, BF16), jnp.asarray(inv, BF16))


def _hyena(hy_u, lw):
    n = hy_u.shape[1]
    blk = min(CONV_BLOCK, n)
    g, asum = _hyena_filter(n, lw)
    khat = _hyena_spectra(g, asum, blk)
    z = _hyena_conv(hy_u, 0, hy_u, 1, lw, khat, 0, blk, True, BF16)
    return _hyena_conv(z, 0, hy_u, 2, lw, khat, 1, blk, False, F32)


def _merge_kernel(x_ref, attn_ref, pool_ref, hy_ref, g1_ref, sh2_ref, sc2_ref, g2_ref,
                  gout_ref, wout_ref, gmlp_ref, w1_ref, w2_ref, gfin_ref, o_ref, *, final_norm):
    gout = gout_ref[...]
    a = _rms(attn_ref[0].astype(F32), gout[:, :MLA_WIDTH]).astype(BF16)
    p = _rms(pool_ref[0], gout[:, MLA_WIDTH:MLA_WIDTH + POOL_WIDTH]).astype(BF16)
    hh = _rms(hy_ref[0], gout[:, MLA_WIDTH + POOL_WIDTH:]).astype(BF16)
    y = (_bdot(a, wout_ref[0:MLA_WIDTH, :])
         + _bdot(p, wout_ref[MLA_WIDTH:MLA_WIDTH + POOL_WIDTH, :])
         + _bdot(hh, wout_ref[MLA_WIDTH + POOL_WIDTH:, :]))
    x1 = x_ref[0] + g1_ref[0] * y
    h2 = (_rms(x1, gmlp_ref[...]) * (1.0 + sc2_ref[0]) + sh2_ref[0]).astype(BF16)
    y2 = jnp.zeros_like(x1)
    for c in range(D_FF // MLP_CHUNK):
        cs = slice(c * MLP_CHUNK, (c + 1) * MLP_CHUNK)
        hid = jnp.maximum(_bdot(h2, w1_ref[:, cs]), 0.0)
        y2 = y2 + _bdot((hid * hid).astype(BF16), w2_ref[cs, :])
    x2 = x1 + g2_ref[0] * y2
    if final_norm:
        x2 = _rms(x2, gfin_ref[...])
    o_ref[0] = x2


def _merge(x, attn, pool, hy, mods, lw, g_final, final_norm, tile):
    bsz, n, _ = x.shape
    nt = n // tile
    tok = lambda w: pl.BlockSpec((1, tile, w), lambda b, i: (b, i, 0))
    mod_spec = lambda k: pl.BlockSpec((1, 1, D_MODEL), lambda b, i, k=k: (b, 0, k))
    return pl.pallas_call(
        functools.partial(_merge_kernel, final_norm=final_norm),
        out_shape=jax.ShapeDtypeStruct((bsz, n, D_MODEL), F32),
        grid=(bsz, nt),
        in_specs=[
            tok(D_MODEL), tok(MLA_WIDTH), tok(POOL_WIDTH), tok(HY_WIDTH),
            mod_spec(2), mod_spec(3), mod_spec(4), mod_spec(5),
            _const_spec((1, D_MODEL)), _const_spec((D_MODEL, D_MODEL)),
            _const_spec((1, D_MODEL)), _const_spec((D_MODEL, D_FF)),
            _const_spec((D_FF, D_MODEL)), _const_spec((1, D_MODEL)),
        ],
        out_specs=tok(D_MODEL),
        compiler_params=_params("parallel", "parallel"),
        name="merge_mlp",
    )(x, attn, pool, hy, mods, mods, mods, mods, lw["g_out"], lw["w_out"], lw["g_mlp"],
      lw["w_mlp1"], lw["w_mlp2"], g_final)


_ROPE_SWAP = np.concatenate([np.arange(8, 16), np.arange(0, 8), np.arange(24, 32), np.arange(16, 24)])


def _rope_tables(n):
    idx = np.arange(n)
    r = (idx // GRID_W).astype(np.float32)
    c = (idx % GRID_W).astype(np.float32)
    n_freq = HEAD_ROPE // 4
    inv = (ROPE_BASE ** (-np.arange(n_freq, dtype=np.float32) / n_freq)).astype(np.float32)
    ar, ac = r[:, None] * inv, c[:, None] * inv
    cos = np.zeros((n, HEAD_PAD), np.float32)
    sin = np.zeros((n, HEAD_PAD), np.float32)
    cos[:, :HEAD_NOPE] = 1.0
    cos[:, HEAD_NOPE:HEAD_NOPE + HEAD_ROPE] = np.concatenate(
        [np.cos(ar), np.cos(ar), np.cos(ac), np.cos(ac)], axis=1)
    sin[:, HEAD_NOPE:HEAD_NOPE + HEAD_ROPE] = np.concatenate(
        [-np.sin(ar), np.sin(ar), -np.sin(ac), np.sin(ac)], axis=1)
    return jnp.asarray(cos), jnp.asarray(sin)


def _identity_tables(n):
    cos = np.zeros((n, HEAD_PAD), np.float32)
    cos[:, :HEAD_NOPE + HEAD_ROPE] = 1.0
    return jnp.asarray(cos), jnp.zeros((n, HEAD_PAD), F32)


def _layer_weights(l, w):
    w_in = w["w_in"][l]
    kr = w_in[:, COL_KR:COL_Q]
    w_in_p = jnp.concatenate([w_in[:, COL_KV:COL_KR], w_in[:, COL_Q:COL_POOL],
                              w_in[:, COL_POOL:COL_HY], w_in[:, COL_HY:],
                              jnp.zeros((D_MODEL, HEAD_NOPE), F32), kr, kr[:, _ROPE_SWAP]],
                             axis=1).astype(BF16)
    wkv = w["w_kv_up"][l].reshape(KV_RANK, N_HEADS, HEAD_NOPE + HEAD_V)
    wka = jnp.concatenate([wkv[:, :, :HEAD_NOPE], jnp.zeros((KV_RANK, N_HEADS, HEAD_V), F32)],
                          axis=2).reshape(KV_RANK, N_HEADS * HEAD_PAD).astype(BF16)
    wvt = jnp.concatenate([wkv[:, :, HEAD_NOPE:],
                           jnp.zeros((KV_RANK, N_HEADS, V_SLOT - HEAD_V), F32)], axis=2)
    wvt = wvt.reshape(KV_RANK, VT_ROWS).T.astype(BF16)
    wq = w["w_q_up"][l].reshape(Q_RANK, N_HEADS, HEAD_NOPE + HEAD_ROPE)
    wqt = jnp.concatenate([wq.reshape(Q_RANK, -1),
                           wq[:, :, HEAD_NOPE:][:, :, _ROPE_SWAP].reshape(Q_RANK, -1)],
                          axis=1).T.astype(BF16)
    pool_bd = jnp.zeros((POOL_WIDTH, POOL_WIDTH), F32)
    for g in range(len(POOL_WINDOWS)):
        sl = slice(g * POOL_GROUP, (g + 1) * POOL_GROUP)
        pool_bd = pool_bd.at[sl, sl].set(w["pool_w"][l, g])
    pad_rows = lambda m, rows: jnp.zeros((rows, m.shape[1]), F32).at[:m.shape[0]].set(m)
    pad_cols = lambda m, cols: jnp.zeros((m.shape[0], cols), F32).at[:, :m.shape[1]].set(m)
    row = lambda v: v.reshape(1, -1)
    return {
        "g_mix": row(w["g_mix"][l]), "w_in": w_in_p, "g_kv": row(w["g_kv"][l]), "wka": wka,
        "wvt": wvt, "g_q": row(w["g_q"][l]), "wqt": wqt,
        "pool_w": pool_bd.astype(BF16), "pool_scale": row(w["pool_scale"][l]),
        "hy_conv_w": w["hy_conv_w"][l], "hy_conv_b": row(w["hy_conv_b"][l]),
        "hy_w1": pad_cols(pad_rows(w["hy_f_w1"][l], LANE), LANE),
        "hy_b1": pad_cols(row(w["hy_f_b1"][l]), LANE),
        "hy_f1": pad_cols(row(w["hy_f_freq1"][l]), LANE),
        "hy_w2": pad_cols(pad_rows(w["hy_f_w2"][l], LANE), LANE),
        "hy_b2": pad_cols(row(w["hy_f_b2"][l]), LANE),
        "hy_f2": pad_cols(row(w["hy_f_freq2"][l]), LANE),
        "hy_w3": pad_rows(w["hy_f_w3"][l], LANE),
        "hy_bias": w["hy_bias"][l].reshape(2, 1, HY_WIDTH),
        "g_out": row(w["g_out"][l]), "w_out": w["w_out"][l].astype(BF16),
        "g_mlp": row(w["g_mlp"][l]), "w_mlp1": w["w_mlp1"][l].astype(BF16),
        "w_mlp2": w["w_mlp2"][l].astype(BF16),
    }


def kernel(x, c, ctx, c_ctx, w_mod, b_mod, g_mix, g_mlp, w_in, g_q, w_q_up, g_kv, w_kv_up,
           pool_w, pool_scale, hy_conv_w, hy_conv_b, hy_f_w1, hy_f_b1, hy_f_freq1, hy_f_w2,
           hy_f_b2, hy_f_freq2, hy_f_w3, hy_bias, g_out, w_out, w_mlp1, w_mlp2, g_final):
    w = dict(g_mix=g_mix, g_mlp=g_mlp, w_in=w_in, g_q=g_q, w_q_up=w_q_up, g_kv=g_kv,
             w_kv_up=w_kv_up, pool_w=pool_w, pool_scale=pool_scale, hy_conv_w=hy_conv_w,
             hy_conv_b=hy_conv_b, hy_f_w1=hy_f_w1, hy_f_b1=hy_f_b1, hy_f_freq1=hy_f_freq1,
             hy_f_w2=hy_f_w2, hy_f_b2=hy_f_b2, hy_f_freq2=hy_f_freq2, hy_f_w3=hy_f_w3,
             hy_bias=hy_bias, g_out=g_out, w_out=w_out, w_mlp1=w_mlp1, w_mlp2=w_mlp2)
    depth = w_mod.shape[0]
    bsz, n, _ = x.shape
    n_ctx = ctx.shape[1]
    tile = min(TOKEN_TILE, n)
    tile_c = min(TOKEN_TILE, n_ctx)

    rows = -(-(bsz + 1) // 8) * 8
    cond = jnp.zeros((rows, D_MODEL), F32).at[:bsz].set(c).at[bsz].set(c_ctx)
    mods_all = _modulation(cond, w_mod, b_mod)
    cos_l, sin_l = _rope_tables(n)
    cos_c, sin_c = _identity_tables(n_ctx)
    g_fin = g_final.reshape(1, D_MODEL)

    xc = ctx
    for l in range(depth):
        last = l == depth - 1
        lw = _layer_weights(l, w)
        mods = mods_all[l, :bsz].reshape(bsz, 1, 6 * D_MODEL)
        mods_c = jnp.broadcast_to(mods_all[l, bsz].reshape(1, 1, 6 * D_MODEL),
                                  (bsz, 1, 6 * D_MODEL))
        q, k, vt, pool_u, hy_u = _inproj(x, (mods, 0), (mods, 1), lw, cos_l, sin_l, tile)
        qc, kc, vtc, pool_uc, hy_uc = _inproj(xc, (mods_c, 0), (mods_c, 1), lw, cos_c, sin_c,
                                              tile_c)
        attn = _attention(q, [(kc, vtc), (k, vt)])
        pool = _pool(pool_u, lw["pool_w"], lw["pool_scale"])
        hy = _hyena(hy_u, lw)
        x = _merge(x, attn, pool, hy, mods, lw, g_fin, last, tile)
        if not last:
            attn_c = _attention(qc, [(kc, vtc)])
            pool_c = _pool(pool_uc, lw["pool_w"], lw["pool_scale"])
            hy_c = _hyena(hy_uc, lw)
            xc = _merge(xc, attn_c, pool_c, hy_c, mods_c, lw, g_fin, False, tile_c)
    return x
```

```python
import functools
import math

import jax
import jax.numpy as jnp
import numpy as np
from jax import lax
from jax.experimental import pallas as pl
from jax.experimental.pallas import tpu as pltpu

F32 = jnp.float32
BF16 = jnp.bfloat16

D_MODEL = 1024
GRID_W = 64
EPS = 1e-6
N_HEADS = 8
HEAD_V = 64
HEAD_NOPE = 64
HEAD_ROPE = 32
HEAD_PAD = 128
Q_RANK = 256
KV_RANK = 128
MLA_WIDTH = N_HEADS * HEAD_V
V_SLOT = 80
VT_ROWS = N_HEADS * V_SLOT
Q_ROWS = N_HEADS * (HEAD_NOPE + 2 * HEAD_ROPE)
POOL_WIDTH = 256
HY_WIDTH = 256
MLA_SCALE = (HEAD_NOPE + HEAD_ROPE) ** -0.5
Q_SCALE = MLA_SCALE * math.log2(math.e)
ROPE_BASE = 10000.0
POOL_WINDOWS = (2, 4, 8, 16)
POOL_GROUP = POOL_WIDTH // len(POOL_WINDOWS)
POOL_PAD = 8
EDGE_ROWS = 16
HY_EMB = 33
HY_FFN = 64
HY_DECAY_TARGET = 1e-2
HY_DECAY_SHORT_PCT = 0.3
HY_DECAY_LONG_PCT = 1.5
D_FF = 4 * D_MODEL
COL_KV, COL_KR, COL_Q, COL_POOL, COL_HY, N_IN = 0, 128, 160, 416, 672, 1440
P_KV, P_Q, P_POOL, P_HY, P_KR, N_IN_P = 0, 128, 384, 640, 1408, 1536

LANE = 128
VMEM_LIMIT = 56 * 1024 * 1024
ATTN_VMEM_LIMIT = 60 * 1024 * 1024
TOKEN_TILE = 512
Q_TILE = 512
Q_TILES_PER_STEP = 2
CHUNK_UNROLL = 8
CONV_BLOCK = 512
FREQ_BLOCK = 256
FREQ_ROWS = 16
MLP_CHUNK = 1024
HIGHEST = lax.Precision.HIGHEST
HIGH = lax.Precision.HIGH


def _params(*sem):
    return pltpu.CompilerParams(dimension_semantics=sem, vmem_limit_bytes=VMEM_LIMIT)


def _const_spec(shape):
    zeros = (0,) * len(shape)
    return pl.BlockSpec(shape, lambda *_: zeros, pipeline_mode=pl.Buffered(1))


def _rms(x, g):
    return x * lax.rsqrt(jnp.mean(x * x, axis=-1, keepdims=True) + EPS) * g


def _bdot(a, b):
    return jnp.dot(a, b, preferred_element_type=F32)


def _split(x):
    hi = x.astype(BF16)
    return hi, (x - hi.astype(F32)).astype(BF16)


def _dot3(a, b):
    return _bdot(a[0], b[0]) + (_bdot(a[0], b[1]) + _bdot(a[1], b[0]))


def _modulation_kernel(c_ref, w_ref, b_ref, o_ref):
    c = c_ref[...]
    s = c / (1.0 + jnp.exp(-c))
    o_ref[0] = jnp.dot(s, w_ref[0], preferred_element_type=F32, precision=HIGHEST) + b_ref[0]


def _modulation(cond, w_mod, b_mod):
    depth = w_mod.shape[0]
    rows = cond.shape[0]
    return pl.pallas_call(
        _modulation_kernel,
        out_shape=jax.ShapeDtypeStruct((depth, rows, 6 * D_MODEL), F32),
        grid=(depth, 6),
        in_specs=[
            pl.BlockSpec((rows, D_MODEL), lambda l, j: (0, 0)),
            pl.BlockSpec((1, D_MODEL, D_MODEL), lambda l, j: (l, 0, j)),
            pl.BlockSpec((1, 1, D_MODEL), lambda l, j: (l, 0, j)),
        ],
        out_specs=pl.BlockSpec((1, rows, D_MODEL), lambda l, j: (l, 0, j)),
        compiler_params=_params("arbitrary", "arbitrary"),
        name="modulation",
    )(cond, w_mod, b_mod.reshape(depth, 1, 6 * D_MODEL))


def _inproj_kernel(x_ref, sh_ref, sc_ref, gmix_ref, win_ref, gkv_ref, wka_ref, wvt_ref,
                   gq_ref, wqt_ref, cos_ref, sin_ref, cost_ref, sint_ref,
                   qt_ref, k_ref, vt_ref, pool_ref, hy_ref):
    x = x_ref[0]
    h = (_rms(x, gmix_ref[...]) * (1.0 + sc_ref[0]) + sh_ref[0]).astype(BF16)
    proj = _bdot(h, win_ref[...])
    pool_ref[0] = proj[:, P_POOL:P_HY]
    hy_ref[0] = proj[:, P_HY:P_KR].astype(hy_ref.dtype)
    kvn = _rms(proj[:, P_KV:P_Q], gkv_ref[...]).astype(BF16)
    kpad = _bdot(kvn, wka_ref[...])
    vt = lax.dot_general(wvt_ref[...], kvn, (((1,), (1,)), ((), ())),
                         preferred_element_type=F32)
    row = lax.broadcasted_iota(jnp.int32, vt.shape, 0)
    vt = jnp.where(row % V_SLOT >= HEAD_V, 1.0, vt).astype(vt_ref.dtype)
    vt_ref[0, 0] = vt.reshape(N_HEADS, V_SLOT, vt.shape[1])
    qn = _rms(proj[:, P_Q:P_POOL], gq_ref[...]).astype(BF16)
    qt = lax.dot_general(wqt_ref[...], qn, (((1,), (1,)), ((), ())),
                         preferred_element_type=F32)
    krab = proj[:, P_KR:N_IN_P]
    kr = krab * cos_ref[...] + pltpu.roll(krab, HEAD_PAD - HEAD_ROPE, 1) * sin_ref[...]
    cost = cost_ref[...]
    sint = sint_ref[...]
    head_q = HEAD_NOPE + HEAD_ROPE
    pad = jnp.zeros((HEAD_PAD - head_q, qt.shape[1]), F32)
    for hd in range(N_HEADS):
        sl = slice(hd * HEAD_PAD, (hd + 1) * HEAD_PAD)
        k_ref[0, hd] = (kpad[:, sl] + kr).astype(k_ref.dtype)
        a = qt[hd * head_q:(hd + 1) * head_q, :]
        b = qt[N_HEADS * head_q + hd * HEAD_ROPE:N_HEADS * head_q + (hd + 1) * HEAD_ROPE, :]
        rope = a[HEAD_NOPE:, :] * cost + b * sint
        q_h = jnp.concatenate([a[:HEAD_NOPE, :], rope, pad], axis=0) * Q_SCALE
        qt_ref[0, hd] = q_h.astype(qt_ref.dtype)


def _inproj(x, shift, scale, lw, cos_t, sin_t, tile):
    bsz, n, _ = x.shape
    nt = n // tile
    width = N_HEADS * HEAD_PAD
    mod_spec = lambda k: pl.BlockSpec((1, 1, D_MODEL), lambda b, i, k=k: (b, 0, k))
    return pl.pallas_call(
        _inproj_kernel,
        out_shape=(
            jax.ShapeDtypeStruct((bsz, N_HEADS, HEAD_PAD, n), BF16),
            jax.ShapeDtypeStruct((bsz, N_HEADS, n, HEAD_PAD), BF16),
            jax.ShapeDtypeStruct((bsz, nt, N_HEADS, V_SLOT, tile), BF16),
            jax.ShapeDtypeStruct((bsz, n, POOL_WIDTH), F32),
            jax.ShapeDtypeStruct((bsz, n, 3 * HY_WIDTH), BF16),
        ),
        grid=(bsz, nt),
        in_specs=[
            pl.BlockSpec((1, tile, D_MODEL), lambda b, i: (b, i, 0)),
            mod_spec(shift[1]), mod_spec(scale[1]),
            _const_spec((1, D_MODEL)),
            _const_spec((D_MODEL, N_IN_P)),
            _const_spec((1, KV_RANK)),
            _const_spec((KV_RANK, width)),
            _const_spec((VT_ROWS, KV_RANK)),
            _const_spec((1, Q_RANK)),
            _const_spec((Q_ROWS, Q_RANK)),
            pl.BlockSpec((tile, HEAD_PAD), lambda b, i: (i, 0)),
            pl.BlockSpec((tile, HEAD_PAD), lambda b, i: (i, 0)),
            pl.BlockSpec((HEAD_ROPE, tile), lambda b, i: (0, i)),
            pl.BlockSpec((HEAD_ROPE, tile), lambda b, i: (0, i)),
        ],
        out_specs=(
            pl.BlockSpec((1, N_HEADS, HEAD_PAD, tile), lambda b, i: (b, 0, 0, i)),
            pl.BlockSpec((1, N_HEADS, tile, HEAD_PAD), lambda b, i: (b, 0, i, 0)),
            pl.BlockSpec((1, 1, N_HEADS, V_SLOT, tile), lambda b, i: (b, i, 0, 0, 0)),
            pl.BlockSpec((1, tile, POOL_WIDTH), lambda b, i: (b, i, 0)),
            pl.BlockSpec((1, tile, 3 * HY_WIDTH), lambda b, i: (b, i, 0)),
        ),
        compiler_params=_params("parallel", "parallel"),
        name="inproj",
    )(x, shift[0], scale[0], lw["g_mix"], lw["w_in"], lw["g_kv"], lw["wka"], lw["wvt"],
      lw["g_q"], lw["wqt"], cos_t, sin_t,
      cos_t[:, HEAD_NOPE:HEAD_NOPE + HEAD_ROPE].T, sin_t[:, HEAD_NOPE:HEAD_NOPE + HEAD_ROPE].T)


def _attention_kernel(*refs, seg_chunks):
    qt_ref, qt_next_ref = refs[0], refs[1]
    n_seg = len(seg_chunks)
    seg_refs = refs[2:2 + 2 * n_seg]
    o_ref = refs[2 + 2 * n_seg]
    out_ref, m_ref = refs[3 + 2 * n_seg], refs[4 + 2 * n_seg]
    s_bufs = refs[5 + 2 * n_seg:]
    tq = out_ref.shape[2]
    n_tiles = qt_ref.shape[3] // tq
    items = [(qt_ref, t, hd) for t in range(n_tiles) for hd in range(N_HEADS)]
    assert len(items) % 2 == 0
    step, last_step = pl.program_id(1), pl.num_programs(1) - 1
    step_id = pl.program_id(0) + step + len(items)

    def stage(item1, item2, parity1):
        buf1, buf2 = s_bufs[parity1], s_bufs[1 - parity1]
        m2 = m_ref[...] if item2 is not None else None
        m1 = jnp.full((1, tq), -jnp.inf, F32)
        acc2 = jnp.zeros((V_SLOT, tq), F32)
        base = 0
        for si, (n_chunks, chunk) in enumerate(seg_chunks):
            k_ref, vt_ref = seg_refs[2 * si], seg_refs[2 * si + 1]
            for c in range(n_chunks):
                rows = slice(base + c * chunk, base + (c + 1) * chunk)
                if item1 is not None:
                    q1_ref, t1, h1 = item1
                    s = _bdot(k_ref[0, h1, c * chunk:(c + 1) * chunk, :],
                              q1_ref[0, h1, :, t1 * tq:(t1 + 1) * tq])
                    buf1[rows, :] = s
                    m1 = jnp.maximum(m1, jnp.max(s, axis=0, keepdims=True))
                if item2 is not None:
                    p = jnp.exp2(buf2[rows, :] - m2).astype(BF16)
                    acc2 = acc2 + _bdot(vt_ref[0, c, item2[2]], p)
            base += n_chunks * chunk
        if item2 is not None:
            _, t2, h2 = item2
            out_ref[h2] = acc2[:HEAD_V, :] / acc2[HEAD_V:HEAD_V + 1, :]
            if h2 == N_HEADS - 1:
                o_ref[0, t2 * tq:(t2 + 1) * tq, :] = (
                    out_ref[...].reshape(MLA_WIDTH, tq).T.astype(o_ref.dtype))
        if item1 is not None:
            m_ref[...] = m1

    pl.when(step == 0)(functools.partial(stage, items[0], None, 0))
    for s in range(len(items) - 1):
        pl.when(step_id >= s)(functools.partial(stage, items[s + 1], items[s], (s + 1) % 2))
    next_first = (qt_next_ref, 0, 0)
    pl.when(step < last_step)(functools.partial(stage, next_first, items[-1], 0))
    pl.when(step == last_step)(functools.partial(stage, None, items[-1], 0))


def _attention(q, segments):
    bsz, _, _, n = q.shape
    tq = min(Q_TILE, n)
    step_q = min(Q_TILES_PER_STEP * tq, n)
    last = n // step_q - 1
    in_specs = [pl.BlockSpec((1, N_HEADS, HEAD_PAD, step_q), lambda b, i: (b, 0, 0, i)),
                pl.BlockSpec((1, N_HEADS, HEAD_PAD, step_q),
                             lambda b, i: (b, 0, 0, jnp.minimum(i + 1, last)))]
    args = [q, q]
    seg_chunks = []
    for k, vt in segments:
        nk = k.shape[2]
        n_chunks, chunk = vt.shape[1], vt.shape[4]
        assert n_chunks * chunk == nk
        seg_chunks.append((n_chunks, chunk))
        in_specs.append(pl.BlockSpec((1, N_HEADS, nk, HEAD_PAD), lambda b, i: (b, 0, 0, 0)))
        in_specs.append(pl.BlockSpec((1, n_chunks, N_HEADS, V_SLOT, chunk),
                                     lambda b, i: (b, 0, 0, 0, 0)))
        args += [k, vt]
    n_keys = sum(nc * ch for nc, ch in seg_chunks)
    return pl.pallas_call(
        functools.partial(_attention_kernel, seg_chunks=tuple(seg_chunks)),
        out_shape=jax.ShapeDtypeStruct((bsz, n, MLA_WIDTH), BF16),
        grid=(bsz, n // step_q),
        in_specs=in_specs,
        out_specs=pl.BlockSpec((1, step_q, MLA_WIDTH), lambda b, i: (b, i, 0)),
        scratch_shapes=[pltpu.VMEM((N_HEADS, HEAD_V, tq), F32), pltpu.VMEM((1, tq), F32),
                        pltpu.VMEM((n_keys, tq), F32), pltpu.VMEM((n_keys, tq), F32)],
        compiler_params=pltpu.CompilerParams(dimension_semantics=("arbitrary", "arbitrary"),
                                             vmem_limit_bytes=ATTN_VMEM_LIMIT),
        name="attention",
    )(*args)


def _shift_rows(a, k):
    n = a.shape[0]
    return pltpu.roll(a, k % n, 0)


def _pool_kernel(u_ref, inv_ref, w_ref, scale_ref, o_ref, ext_ref):
    n = u_ref.shape[1]
    u = u_ref[0]
    zeros = jnp.zeros((POOL_PAD, POOL_WIDTH), F32)
    ext_ref[0:POOL_PAD, :] = zeros
    ext_ref[POOL_PAD + n:, :] = zeros
    ext_ref[POOL_PAD:POOL_PAD + n, :] = u
    ext = ext_ref[...]
    s2 = ext + _shift_rows(ext, 1)
    s4 = _shift_rows(s2, 1) + _shift_rows(s2, -1)
    s8 = _shift_rows(s4, 2) + _shift_rows(s4, -2)
    s16 = _shift_rows(s8, 4) + _shift_rows(s8, -4)
    lane = lax.broadcasted_iota(jnp.int32, (n, POOL_WIDTH), 1)
    wsum = None
    for g, s in enumerate((s2, s4, s8, s16)):
        sg = s[POOL_PAD:POOL_PAD + n, :]
        wsum = sg if wsum is None else jnp.where(lane >= g * POOL_GROUP, sg, wsum)
    diff = (wsum * inv_ref[...] - u).astype(BF16)
    o_ref[0] = _bdot(diff, w_ref[...]) * scale_ref[...]


def _pool(u, w_bd, scale):
    bsz, n, _ = u.shape
    t = np.arange(n)
    inv = np.concatenate(
        [np.repeat((1.0 / (np.clip(t + w // 2, 0, n) - np.clip(t - w // 2, 0, n)))[:, None],
                   POOL_GROUP, axis=1) for w in POOL_WINDOWS], axis=1).astype(np.float32)
    return pl.pallas_call(
        _pool_kernel,
        out_shape=jax.ShapeDtypeStruct((bsz, n, POOL_WIDTH), F32),
        grid=(bsz,),
        in_specs=[
            pl.BlockSpec((1, n, POOL_WIDTH), lambda b: (b, 0, 0)),
            _const_spec((n, POOL_WIDTH)),
            _const_spec((POOL_WIDTH, POOL_WIDTH)),
            _const_spec((1, POOL_WIDTH)),
        ],
        out_specs=pl.BlockSpec((1, n, POOL_WIDTH), lambda b: (b, 0, 0)),
        scratch_shapes=[pltpu.VMEM((n + 2 * POOL_PAD, POOL_WIDTH), F32)],
        compiler_params=_params("parallel"),
        name="pool_mixer",
    )(u, jnp.asarray(inv), w_bd, scale)


def _filter_kernel(z_ref, w1_ref, b1_ref, f1_ref, w2_ref, b2_ref, f2_ref, w3h_ref, w3l_ref,
                   dl_ref, g_ref, asum_ref, *, n):
    i = pl.program_id(0)
    rows_per = z_ref.shape[0]
    z = z_ref[...]
    h = jnp.sin(f1_ref[...] * (jnp.dot(z, w1_ref[...], preferred_element_type=F32,
                                       precision=HIGHEST) + b1_ref[...]))
    h = jnp.sin(f2_ref[...] * (jnp.dot(h, w2_ref[...], preferred_element_type=F32,
                                       precision=HIGHEST) + b2_ref[...]))
    h = _dot3(_split(h), (w3h_ref[0], w3l_ref[0]))
    r = i * rows_per + lax.broadcasted_iota(jnp.int32, (rows_per, HY_WIDTH), 0)
    pos = jnp.where(r < n, n - r, r - n)
    t = pos.astype(F32) * (1.0 / (n - 1))
    decay = jnp.exp(-t * dl_ref[...])
    valid = r > 0

    @pl.when(i == 0)
    def _():
        asum_ref[...] = jnp.zeros_like(asum_ref)

    for o in range(2):
        g = jnp.where(valid, h[:, o * HY_WIDTH:(o + 1) * HY_WIDTH] * decay, 0.0)
        g_ref[o] = g
        asum_ref[o] += jnp.sum(jnp.abs(g), axis=0, keepdims=True)


def _hyena_filter(n, lw):
    rows = 2 * n
    tile = min(n, 1024)
    n_back = n // tile
    r = np.arange(rows)
    pos = np.where(r < n, n - r, r - n).astype(np.float64)
    t = pos / (n - 1)
    bands = (HY_EMB - 1) // 2
    freqs = np.linspace(1e-4, bands - 1, bands)[None, :]
    wpos = 2.0 * math.pi * pos[:, None] / n
    z = np.zeros((rows, LANE), np.float32)
    z[:, 0] = t
    z[:, 1:1 + bands] = np.cos(freqs * wpos)
    z[:, 1 + bands:HY_EMB] = -np.sin(freqs * wpos)
    deltas = np.abs(np.linspace(math.log(HY_DECAY_TARGET) / HY_DECAY_LONG_PCT,
                                math.log(HY_DECAY_TARGET) / HY_DECAY_SHORT_PCT, HY_WIDTH))
    deltas = jnp.asarray(deltas[None, :], F32)
    w3 = lw["hy_w3"].reshape(LANE, 2, 2, HY_WIDTH).transpose(2, 0, 1, 3).reshape(2, LANE, -1)
    w3_spec = pl.BlockSpec((1, LANE, 2 * HY_WIDTH), lambda i: (jnp.where(i < n_back, 1, 0), 0, 0))
    return pl.pallas_call(
        functools.partial(_filter_kernel, n=n),
        out_shape=(jax.ShapeDtypeStruct((2, rows, HY_WIDTH), F32),
                   jax.ShapeDtypeStruct((2, 1, HY_WIDTH), F32)),
        grid=(rows // tile,),
        in_specs=[
            pl.BlockSpec((tile, LANE), lambda i: (i, 0)),
            _const_spec((LANE, LANE)), _const_spec((1, LANE)), _const_spec((1, LANE)),
            _const_spec((LANE, LANE)), _const_spec((1, LANE)), _const_spec((1, LANE)),
            w3_spec, w3_spec,
            _const_spec((1, HY_WIDTH)),
        ],
        out_specs=(pl.BlockSpec((2, tile, HY_WIDTH), lambda i: (0, i, 0)),
                   pl.BlockSpec((2, 1, HY_WIDTH), lambda i: (0, 0, 0))),
        compiler_params=_params("arbitrary"),
        name="hyena_filter",
    )(jnp.asarray(z), lw["hy_w1"], lw["hy_b1"], lw["hy_f1"], lw["hy_w2"], lw["hy_b2"],
      lw["hy_f2"], *_split(w3), deltas)


def _dft_tables(blk):
    f = np.arange(blk, dtype=np.float64)[:, None]
    m = np.arange(blk, dtype=np.float64)[None, :]
    theta = math.pi * (2.0 * f + 1.0) * m / (2.0 * blk)
    return np.cos(theta), np.sin(theta)


def _spectra_kernel(g_ref, asum_ref, fwdh_ref, fwdl_ref, k_ref, prev_ref):
    e = pl.program_id(1)
    blk = g_ref.shape[1]
    s = _dot3((fwdh_ref[...], fwdl_ref[...]), _split(g_ref[0]))
    s = s / asum_ref[0]

    @pl.when(e > 0)
    def _():
        prev = prev_ref[...]
        f = lax.broadcasted_iota(jnp.int32, (blk, HY_WIDTH), 0)
        sgn = jnp.where(f % 2 == 0, 1.0, -1.0).astype(F32)
        k_ref[0, 0, 0] = (s[:blk] - sgn * prev[blk:]).astype(k_ref.dtype)
        k_ref[0, 0, 1] = (s[blk:] + sgn * prev[:blk]).astype(k_ref.dtype)

    prev_ref[...] = s


def _hyena_spectra(g, asum, blk):
    rows = g.shape[1]
    nseg = rows // blk
    cos, sin = _dft_tables(blk)
    fwd = jnp.asarray(np.concatenate([cos, -sin], axis=0), F32)
    return pl.pallas_call(
        _spectra_kernel,
        out_shape=jax.ShapeDtypeStruct((2, nseg - 1, 2, blk, HY_WIDTH), BF16),
        grid=(2, nseg),
        in_specs=[
            pl.BlockSpec((1, blk, HY_WIDTH), lambda o, e: (o, e, 0)),
            pl.BlockSpec((1, 1, HY_WIDTH), lambda o, e: (o, 0, 0)),
            _const_spec((2 * blk, blk)), _const_spec((2 * blk, blk)),
        ],
        out_specs=pl.BlockSpec((1, 1, 2, blk, HY_WIDTH),
                               lambda o, e: (o, jnp.maximum(e - 1, 0), 0, 0, 0)),
        scratch_shapes=[pltpu.VMEM((2 * blk, HY_WIDTH), F32)],
        compiler_params=_params("arbitrary", "arbitrary"),
        name="hyena_spectra",
    )(g, asum, *_split(fwd))


def _short_conv_rows(src_ref, rows, w_ref, b_ref):
    r0, r1 = rows
    n = src_ref.shape[1]
    cur = src_ref[0, r0:r1, :].astype(F32)
    row = lax.broadcasted_iota(jnp.int32, cur.shape, 0)
    edge = jnp.zeros((1, cur.shape[1]), F32)
    before = src_ref[0, r0 - EDGE_ROWS:r0, :].astype(F32)[EDGE_ROWS - 1:, :] if r0 > 0 else edge
    after = src_ref[0, r1:r1 + EDGE_ROWS, :].astype(F32)[:1, :] if r1 < n else edge
    prev = jnp.where(row == 0, before, _shift_rows(cur, 1))
    nxt = jnp.where(row == r1 - r0 - 1, after, _shift_rows(cur, -1))
    return prev * w_ref[0:1, :] + cur * w_ref[1:2, :] + nxt * w_ref[2:3, :] + b_ref[...]


def _conv_kernel(u_ref, gate_ref, uw_ref, ub_ref, gw_ref, gb_ref, bias_ref, k_ref, fwd_ref,
                 inv_ref, o_ref, u32_ref, u16_ref, acc_ref, uf0_ref, uf1_ref, yf0_ref, yf1_ref,
                 *, conv_input):
    n = u_ref.shape[1]
    nf, blk = fwd_ref.shape[0], fwd_ref.shape[2]
    nb = n // blk
    fb = fwd_ref.shape[1] // 2
    uf_refs, yf_refs = (uf0_ref, uf1_ref), (yf0_ref, yf1_ref)
    step_id = pl.program_id(0) + nf + 2

    def forward(f):
        for j in range(nb):
            uf_refs[f % 2][j] = _bdot(fwd_ref[f], u16_ref[j * blk:(j + 1) * blk, :]).astype(BF16)

    def multiply(f):
        uf_ref, yf_ref = uf_refs[f % 2], yf_refs[f % 2]
        for c in range(fb // FREQ_ROWS):
            rs = slice(c * FREQ_ROWS, (c + 1) * FREQ_ROWS)
            rs_im = slice(fb + c * FREQ_ROWS, fb + (c + 1) * FREQ_ROWS)
            ks = slice(f * fb + c * FREQ_ROWS, f * fb + (c + 1) * FREQ_ROWS)
            for i in range(nb):
                yr = jnp.zeros((FREQ_ROWS, HY_WIDTH), BF16)
                yi = jnp.zeros((FREQ_ROWS, HY_WIDTH), BF16)
                for j in range(nb):
                    d = i - j + nb - 1
                    kr = k_ref[0, d, 0, ks, :]
                    ki = k_ref[0, d, 1, ks, :]
                    ur = uf_ref[j, rs, :]
                    ui = uf_ref[j, rs_im, :]
                    yr = yr + (kr * ur - ki * ui)
                    yi = yi + (kr * ui + ki * ur)
                yf_ref[i, rs, :] = yr
                yf_ref[i, rs_im, :] = yi

    def inverse(f):
        for i in range(nb):
            y = _bdot(inv_ref[f], yf_refs[f % 2][i])
            rows = slice(i * blk, (i + 1) * blk)
            acc_ref[rows, :] = y if f == 0 else acc_ref[rows, :] + y

    def prologue():
        for j in range(nb):
            rows = (j * blk, (j + 1) * blk)
            if conv_input:
                u = _short_conv_rows(u_ref, rows, uw_ref, ub_ref)
            else:
                u = u_ref[0, rows[0]:rows[1], :].astype(F32)
            u32_ref[rows[0]:rows[1], :] = u
            u16_ref[rows[0]:rows[1], :] = u.astype(BF16)
        forward(0)

    def stage(f):
        if f + 1 < nf:
            forward(f + 1)
        if f < nf:
            multiply(f)
        if f >= 1:
            inverse(f - 1)
        if f == nf:
            epilogue()

    def epilogue():
        bias = bias_ref[0]
        for j in range(nb):
            rows = (j * blk, (j + 1) * blk)
            rs = slice(rows[0], rows[1])
            gate = _short_conv_rows(gate_ref, rows, gw_ref, gb_ref)
            conv = acc_ref[rs, :] + u32_ref[rs, :] * bias
            o_ref[0, rs, :] = (gate * conv).astype(o_ref.dtype)

    pl.when(step_id >= 0)(prologue)
    for f in range(nf + 1):
        pl.when(step_id >= f + 1)(functools.partial(stage, f))


def _hyena_conv(u_arr, u_col, gate_arr, gate_col, lw, khat, order, blk, conv_input, out_dtype):
    bsz, n, _ = gate_arr.shape
    nb = n // blk
    fb = min(FREQ_BLOCK, blk)
    nf = blk // fb
    cos, sin = _dft_tables(blk)
    fwd = np.concatenate([cos.reshape(nf, fb, blk), -sin.reshape(nf, fb, blk)], axis=1)
    inv = np.concatenate([cos.T.reshape(blk, nf, fb), -sin.T.reshape(blk, nf, fb)], axis=2)
    inv = np.transpose(inv, (1, 0, 2)) / blk
    cw, cb = lw["hy_conv_w"], lw["hy_conv_b"]
    col = lambda c: (lambda b, c=c: (0, c))
    once = pl.Buffered(1)
    return pl.pallas_call(
        functools.partial(_conv_kernel, conv_input=conv_input),
        out_shape=jax.ShapeDtypeStruct((bsz, n, HY_WIDTH), out_dtype),
        grid=(bsz,),
        in_specs=[
            pl.BlockSpec((1, n, HY_WIDTH), lambda b, c=u_col: (b, 0, c)),
            pl.BlockSpec((1, n, HY_WIDTH), lambda b, c=gate_col: (b, 0, c)),
            pl.BlockSpec((3, HY_WIDTH), col(u_col if conv_input else 0)),
            pl.BlockSpec((1, HY_WIDTH), col(u_col if conv_input else 0)),
            pl.BlockSpec((3, HY_WIDTH), col(gate_col)),
            pl.BlockSpec((1, HY_WIDTH), col(gate_col)),
            pl.BlockSpec((1, 1, HY_WIDTH), lambda b, o=order: (o, 0, 0)),
            pl.BlockSpec((1, 2 * nb - 1, 2, blk, HY_WIDTH), lambda b, o=order: (o, 0, 0, 0, 0),
                         pipeline_mode=once),
            pl.BlockSpec((nf, 2 * fb, blk), lambda b: (0, 0, 0), pipeline_mode=once),
            pl.BlockSpec((nf, blk, 2 * fb), lambda b: (0, 0, 0), pipeline_mode=once),
        ],
        out_specs=pl.BlockSpec((1, n, HY_WIDTH), lambda b: (b, 0, 0)),
        scratch_shapes=[
            pltpu.VMEM((n, HY_WIDTH), F32),
            pltpu.VMEM((n, HY_WIDTH), BF16),
            pltpu.VMEM((n, HY_WIDTH), F32),
            pltpu.VMEM((nb, 2 * fb, HY_WIDTH), BF16),
            pltpu.VMEM((nb, 2 * fb, HY_WIDTH), BF16),
            pltpu.VMEM((nb, 2 * fb, HY_WIDTH), BF16),
            pltpu.VMEM((nb, 2 * fb, HY_WIDTH), BF16),
        ],
        compiler_params=_params("parallel"),
        name="hyena_conv%d" % order,
    )(u_arr, gate_arr, cw, cb, cw, cb, lw["hy_bias"], khat,
      jnp.asarray(fwd, BF16), jnp.asarray(inv, BF16))


def _hyena(hy_u, lw):
    n = hy_u.shape[1]
    blk = min(CONV_BLOCK, n)
    g, asum = _hyena_filter(n, lw)
    khat = _hyena_spectra(g, asum, blk)
    z = _hyena_conv(hy_u, 0, hy_u, 1, lw, khat, 0, blk, True, BF16)
    return _hyena_conv(z, 0, hy_u, 2, lw, khat, 1, blk, False, F32)


def _merge_kernel(x_ref, attn_ref, pool_ref, hy_ref, g1_ref, sh2_ref, sc2_ref, g2_ref,
                  gout_ref, wout_ref, gmlp_ref, w1_ref, w2_ref, gfin_ref, o_ref, *, final_norm):
    gout = gout_ref[...]
    a = _rms(attn_ref[0].astype(F32), gout[:, :MLA_WIDTH]).astype(BF16)
    p = _rms(pool_ref[0], gout[:, MLA_WIDTH:MLA_WIDTH + POOL_WIDTH]).astype(BF16)
    hh = _rms(hy_ref[0], gout[:, MLA_WIDTH + POOL_WIDTH:]).astype(BF16)
    y = (_bdot(a, wout_ref[0:MLA_WIDTH, :])
         + _bdot(p, wout_ref[MLA_WIDTH:MLA_WIDTH + POOL_WIDTH, :])
         + _bdot(hh, wout_ref[MLA_WIDTH + POOL_WIDTH:, :]))
    x1 = x_ref[0] + g1_ref[0] * y
    h2 = (_rms(x1, gmlp_ref[...]) * (1.0 + sc2_ref[0]) + sh2_ref[0]).astype(BF16)
    y2 = jnp.zeros_like(x1)
    for c in range(D_FF // MLP_CHUNK):
        cs = slice(c * MLP_CHUNK, (c + 1) * MLP_CHUNK)
        hid = jnp.maximum(_bdot(h2, w1_ref[:, cs]), 0.0)
        y2 = y2 + _bdot((hid * hid).astype(BF16), w2_ref[cs, :])
    x2 = x1 + g2_ref[0] * y2
    if final_norm:
        x2 = _rms(x2, gfin_ref[...])
    o_ref[0] = x2


def _merge(x, attn, pool, hy, mods, lw, g_final, final_norm, tile):
    bsz, n, _ = x.shape
    nt = n // tile
    tok = lambda w: pl.BlockSpec((1, tile, w), lambda b, i: (b, i, 0))
    mod_spec = lambda k: pl.BlockSpec((1, 1, D_MODEL), lambda b, i, k=k: (b, 0, k))
    return pl.pallas_call(
        functools.partial(_merge_kernel, final_norm=final_norm),
        out_shape=jax.ShapeDtypeStruct((bsz, n, D_MODEL), F32),
        grid=(bsz, nt),
        in_specs=[
            tok(D_MODEL), tok(MLA_WIDTH), tok(POOL_WIDTH), tok(HY_WIDTH),
            mod_spec(2), mod_spec(3), mod_spec(4), mod_spec(5),
            _const_spec((1, D_MODEL)), _const_spec((D_MODEL, D_MODEL)),
            _const_spec((1, D_MODEL)), _const_spec((D_MODEL, D_FF)),
            _const_spec((D_FF, D_MODEL)), _const_spec((1, D_MODEL)),
        ],
        out_specs=tok(D_MODEL),
        compiler_params=_params("parallel", "parallel"),
        name="merge_mlp",
    )(x, attn, pool, hy, mods, mods, mods, mods, lw["g_out"], lw["w_out"], lw["g_mlp"],
      lw["w_mlp1"], lw["w_mlp2"], g_final)


_ROPE_SWAP = np.concatenate([np.arange(8, 16), np.arange(0, 8), np.arange(24, 32), np.arange(16, 24)])


def _rope_tables(n):
    idx = np.arange(n)
    r = (idx // GRID_W).astype(np.float32)
    c = (idx % GRID_W).astype(np.float32)
    n_freq = HEAD_ROPE // 4
    inv = (ROPE_BASE ** (-np.arange(n_freq, dtype=np.float32) / n_freq)).astype(np.float32)
    ar, ac = r[:, None] * inv, c[:, None] * inv
    cos = np.zeros((n, HEAD_PAD), np.float32)
    sin = np.zeros((n, HEAD_PAD), np.float32)
    cos[:, :HEAD_NOPE] = 1.0
    cos[:, HEAD_NOPE:HEAD_NOPE + HEAD_ROPE] = np.concatenate(
        [np.cos(ar), np.cos(ar), np.cos(ac), np.cos(ac)], axis=1)
    sin[:, HEAD_NOPE:HEAD_NOPE + HEAD_ROPE] = np.concatenate(
        [-np.sin(ar), np.sin(ar), -np.sin(ac), np.sin(ac)], axis=1)
    return jnp.asarray(cos), jnp.asarray(sin)


def _identity_tables(n):
    cos = np.zeros((n, HEAD_PAD), np.float32)
    cos[:, :HEAD_NOPE + HEAD_ROPE] = 1.0
    return jnp.asarray(cos), jnp.zeros((n, HEAD_PAD), F32)


def _layer_weights(l, w):
    w_in = w["w_in"][l]
    kr = w_in[:, COL_KR:COL_Q]
    w_in_p = jnp.concatenate([w_in[:, COL_KV:COL_KR], w_in[:, COL_Q:COL_POOL],
                              w_in[:, COL_POOL:COL_HY], w_in[:, COL_HY:],
                              jnp.zeros((D_MODEL, HEAD_NOPE), F32), kr, kr[:, _ROPE_SWAP]],
                             axis=1).astype(BF16)
    wkv = w["w_kv_up"][l].reshape(KV_RANK, N_HEADS, HEAD_NOPE + HEAD_V)
    wka = jnp.concatenate([wkv[:, :, :HEAD_NOPE], jnp.zeros((KV_RANK, N_HEADS, HEAD_V), F32)],
                          axis=2).reshape(KV_RANK, N_HEADS * HEAD_PAD).astype(BF16)
    wvt = jnp.concatenate([wkv[:, :, HEAD_NOPE:],
                           jnp.zeros((KV_RANK, N_HEADS, V_SLOT - HEAD_V), F32)], axis=2)
    wvt = wvt.reshape(KV_RANK, VT_ROWS).T.astype(BF16)
    wq = w["w_q_up"][l].reshape(Q_RANK, N_HEADS, HEAD_NOPE + HEAD_ROPE)
    wqt = jnp.concatenate([wq.reshape(Q_RANK, -1),
                           wq[:, :, HEAD_NOPE:][:, :, _ROPE_SWAP].reshape(Q_RANK, -1)],
                          axis=1).T.astype(BF16)
    pool_bd = jnp.zeros((POOL_WIDTH, POOL_WIDTH), F32)
    for g in range(len(POOL_WINDOWS)):
        sl = slice(g * POOL_GROUP, (g + 1) * POOL_GROUP)
        pool_bd = pool_bd.at[sl, sl].set(w["pool_w"][l, g])
    pad_rows = lambda m, rows: jnp.zeros((rows, m.shape[1]), F32).at[:m.shape[0]].set(m)
    pad_cols = lambda m, cols: jnp.zeros((m.shape[0], cols), F32).at[:, :m.shape[1]].set(m)
    row = lambda v: v.reshape(1, -1)
    return {
        "g_mix": row(w["g_mix"][l]), "w_in": w_in_p, "g_kv": row(w["g_kv"][l]), "wka": wka,
        "wvt": wvt, "g_q": row(w["g_q"][l]), "wqt": wqt,
        "pool_w": pool_bd.astype(BF16), "pool_scale": row(w["pool_scale"][l]),
        "hy_conv_w": w["hy_conv_w"][l], "hy_conv_b": row(w["hy_conv_b"][l]),
        "hy_w1": pad_cols(pad_rows(w["hy_f_w1"][l], LANE), LANE),
        "hy_b1": pad_cols(row(w["hy_f_b1"][l]), LANE),
        "hy_f1": pad_cols(row(w["hy_f_freq1"][l]), LANE),
        "hy_w2": pad_cols(pad_rows(w["hy_f_w2"][l], LANE), LANE),
        "hy_b2": pad_cols(row(w["hy_f_b2"][l]), LANE),
        "hy_f2": pad_cols(row(w["hy_f_freq2"][l]), LANE),
        "hy_w3": pad_rows(w["hy_f_w3"][l], LANE),
        "hy_bias": w["hy_bias"][l].reshape(2, 1, HY_WIDTH),
        "g_out": row(w["g_out"][l]), "w_out": w["w_out"][l].astype(BF16),
        "g_mlp": row(w["g_mlp"][l]), "w_mlp1": w["w_mlp1"][l].astype(BF16),
        "w_mlp2": w["w_mlp2"][l].astype(BF16),
    }


def kernel(x, c, ctx, c_ctx, w_mod, b_mod, g_mix, g_mlp, w_in, g_q, w_q_up, g_kv, w_kv_up,
           pool_w, pool_scale, hy_conv_w, hy_conv_b, hy_f_w1, hy_f_b1, hy_f_freq1, hy_f_w2,
           hy_f_b2, hy_f_freq2, hy_f_w3, hy_bias, g_out, w_out, w_mlp1, w_mlp2, g_final):
    w = dict(g_mix=g_mix, g_mlp=g_mlp, w_in=w_in, g_q=g_q, w_q_up=w_q_up, g_kv=g_kv,
             w_kv_up=w_kv_up, pool_w=pool_w, pool_scale=pool_scale, hy_conv_w=hy_conv_w,
             hy_conv_b=hy_conv_b, hy_f_w1=hy_f_w1, hy_f_b1=hy_f_b1, hy_f_freq1=hy_f_freq1,
             hy_f_w2=hy_f_w2, hy_f_b2=hy_f_b2, hy_f_freq2=hy_f_freq2, hy_f_w3=hy_f_w3,
             hy_bias=hy_bias, g_out=g_out, w_out=w_out, w_mlp1=w_mlp1, w_mlp2=w_mlp2)
    depth = w_mod.shape[0]
    bsz, n, _ = x.shape
    n_ctx = ctx.shape[1]
    tile = min(TOKEN_TILE, n)
    tile_c = min(TOKEN_TILE, n_ctx)

    rows = -(-(bsz + 1) // 8) * 8
    cond = jnp.zeros((rows, D_MODEL), F32).at[:bsz].set(c).at[bsz].set(c_ctx)
    mods_all = _modulation(cond, w_mod, b_mod)
    cos_l, sin_l = _rope_tables(n)
    cos_c, sin_c = _identity_tables(n_ctx)
    g_fin = g_final.reshape(1, D_MODEL)

    xc = ctx
    for l in range(depth):
        last = l == depth - 1
        lw = _layer_weights(l, w)
        mods = mods_all[l, :bsz].reshape(bsz, 1, 6 * D_MODEL)
        mods_c = jnp.broadcast_to(mods_all[l, bsz].reshape(1, 1, 6 * D_MODEL),
                                  (bsz, 1, 6 * D_MODEL))
        q, k, vt, pool_u, hy_u = _inproj(x, (mods, 0), (mods, 1), lw, cos_l, sin_l, tile)
        qc, kc, vtc, pool_uc, hy_uc = _inproj(xc, (mods_c, 0), (mods_c, 1), lw, cos_c, sin_c,
                                              tile_c)
        attn = _attention(q, [(kc, vtc), (k, vt)])
        pool = _pool(pool_u, lw["pool_w"], lw["pool_scale"])
        hy = _hyena(hy_u, lw)
        x = _merge(x, attn, pool, hy, mods, lw, g_fin, last, tile)
        if not last:
            attn_c = _attention(qc, [(kc, vtc)])
            pool_c = _pool(pool_uc, lw["pool_w"], lw["pool_scale"])
            hy_c = _hyena(hy_uc, lw)
            xc = _merge(xc, attn_c, pool_c, hy_c, mods_c, lw, g_fin, False, tile_c)
    return x
```
